```python
import math
import jax, jax.numpy as jnp
from jax import lax
import numpy as np

D_MODEL = 1024
BATCH = 32
SEQ = 2048
DEPTH = 1
DEC_BATCH = 8
DEC_SEQ = 8192
PAST_LEN = 128

BRANCH_W = 512
N_Q_HEADS = 8
N_KV_HEADS = 2
HEAD_DIM = 64
WINDOW = 128
BLOCK = 128
N_BUCKETS = 32
MAX_DISTANCE = 128
HYENA_WIDTH = BRANCH_W
HYENA_ORDER = 2
SHORT_CONV = 3
POS_EMB_DIM = 33
FILTER_HIDDEN = 64
DECAY_TARGET = 1e-2
FAST_DECAY_PCT = 0.3
SLOW_DECAY_PCT = 1.5
N_MEM_HEADS = 4
MEM_HEAD_DIM = BRANCH_W // N_MEM_HEADS
N_MEM = 256
N_BRANCHES = 3
N_EXPERTS = 32
TOP_K = 4
D_EXPERT = 1024
SWIGLU_LIMIT = 7.0
SWIGLU_ALPHA = 1.702
MOE_CHUNK = 512
LN_EPS = 1e-5
DN_ALPHA = (2 * DEPTH) ** 0.25
DN_BETA = (8 * DEPTH) ** -0.25

Q_W = N_Q_HEADS * HEAD_DIM
KV_W = N_KV_HEADS * HEAD_DIM
MQ_W = N_MEM_HEADS * MEM_HEAD_DIM
IN_SPLITS = (Q_W, Q_W + KV_W, Q_W + 2 * KV_W, Q_W + 2 * KV_W + 3 * HYENA_WIDTH,
             Q_W + 2 * KV_W + 3 * HYENA_WIDTH + MQ_W)
IN_W = IN_SPLITS[-1] + N_BRANCHES * D_MODEL
FILTER_OUT = 2 * HYENA_ORDER * HYENA_WIDTH

kernel_name = 'hybrid_gated_encoder_wattn_hyena_memxattn_moe'


def layer_norm(x, g, b):
    xf = x.astype(jnp.float32)
    mu = xf.mean(-1, keepdims=True)
    var = jnp.square(xf - mu).mean(-1, keepdims=True)
    return ((xf - mu) * lax.rsqrt(var + LN_EPS) * g.astype(jnp.float32) + b.astype(jnp.float32)).astype(x.dtype)


def t5_bucket(rel):
    nb = N_BUCKETS // 2
    max_exact = nb // 2
    ret = jnp.where(rel > 0, nb, 0)
    n = jnp.abs(rel)
    nf = jnp.maximum(n, 1).astype(jnp.float32)
    large = max_exact + (jnp.log(nf / max_exact) / math.log(MAX_DISTANCE / max_exact)
                         * (nb - max_exact)).astype(jnp.int32)
    large = jnp.minimum(large, nb - 1)
    return ret + jnp.where(n < max_exact, n, large)


def window_attention(q, k, v, sink, rel_bias):
    B, L = q.shape[:2]
    nb = L // BLOCK
    G = N_Q_HEADS // N_KV_HEADS
    qb = q.reshape(B, nb, BLOCK, N_KV_HEADS, G, HEAD_DIM)

    def band(t):
        tp = jnp.pad(t, ((0, 0), (BLOCK, BLOCK), (0, 0), (0, 0))).reshape(B, nb + 2, BLOCK, N_KV_HEADS, HEAD_DIM)
        return jnp.concatenate([tp[:, :-2], tp[:, 1:-1], tp[:, 2:]], axis=2)

    kb, vb = band(k), band(v)
    s = jnp.einsum('bnqgrd,bnkgd->bngrqk', qb, kb, preferred_element_type=jnp.float32) * (HEAD_DIM ** -0.5)
    qi = jnp.arange(BLOCK, dtype=jnp.int32)[:, None]
    kj = jnp.arange(3 * BLOCK, dtype=jnp.int32)[None, :]
    rel = kj - BLOCK - qi
    bias = rel_bias[t5_bucket(rel)].astype(jnp.float32)
    bias = jnp.transpose(bias, (2, 0, 1)).reshape(N_KV_HEADS, G, BLOCK, 3 * BLOCK)
    kpos = (jnp.arange(nb, dtype=jnp.int32)[:, None, None] - 1) * BLOCK + kj[None]
    valid = (jnp.abs(rel)[None] <= WINDOW) & (kpos >= 0) & (kpos < L)
    s = jnp.where(valid[None, :, None, None], s + bias, jnp.float32(-1e30))
    sk = sink.astype(jnp.float32).reshape(N_KV_HEADS, G)[None, None, :, :, None, None]
    m = jnp.maximum(s.max(-1, keepdims=True), sk)
    p = jnp.exp(s - m)
    denom = p.sum(-1, keepdims=True) + jnp.exp(sk - m)
    p = (p / denom).astype(v.dtype)
    o = jnp.einsum('bngrqk,bnkgd->bnqgrd', p, vb)
    return o.reshape(B, L, Q_W)


def memory_attention(qm, mem, w_mem_kv):
    B, L = qm.shape[:2]
    M = mem.shape[1]
    q = qm.reshape(B, L, N_MEM_HEADS, MEM_HEAD_DIM)
    kv = (mem @ w_mem_kv).reshape(B, M, 2, N_MEM_HEADS, MEM_HEAD_DIM)
    s = jnp.einsum('blhd,bmhd->bhlm', q, kv[:, :, 0], preferred_element_type=jnp.float32) * (MEM_HEAD_DIM ** -0.5)
    p = jax.nn.softmax(s, axis=-1).astype(qm.dtype)
    o = jnp.einsum('bhlm,bmhd->blhd', p, kv[:, :, 1])
    return o.reshape(B, L, MQ_W)


def hyena_filters(L, w1, b1, w2, b2, w3, b3, freq):
    f32 = jnp.float32
    bands = (POS_EMB_DIM - 1) // 2
    t = jnp.linspace(0.0, 1.0, L, dtype=f32)[:, None]
    w = 2.0 * math.pi * jnp.arange(L, dtype=f32)[:, None] / L
    f = jnp.linspace(1e-4, bands - 1, bands, dtype=f32)[None]
    z = jnp.concatenate([t, jnp.cos(f * w), -jnp.sin(f * w)], axis=-1)
    h = jnp.sin(freq[0].astype(f32) * (z @ w1.astype(f32) + b1.astype(f32)))
    h = jnp.sin(freq[1].astype(f32) * (h @ w2.astype(f32) + b2.astype(f32)))
    h = (h @ w3.astype(f32) + b3.astype(f32)).reshape(L, 2, HYENA_ORDER, HYENA_WIDTH)
    deltas = jnp.abs(jnp.linspace(math.log(DECAY_TARGET) / FAST_DECAY_PCT,
                                  math.log(DECAY_TARGET) / SLOW_DECAY_PCT, HYENA_WIDTH, dtype=f32))
    h = h * jnp.exp(-t[:, :, None, None] * deltas)
    fwd, bwd = h[:, 0], h[:, 1]
    k = jnp.concatenate([fwd, jnp.zeros_like(fwd[:1]), bwd[:0:-1]], axis=0)
    k = k / jnp.sum(jnp.abs(k), axis=0, keepdims=True)
    return jnp.fft.rfft(k, axis=0)


def short_conv(u, w, b):
    L = u.shape[1]
    pad = SHORT_CONV // 2
    up = jnp.pad(u, ((0, 0), (pad, pad), (0, 0)))
    out = b
    for i in range(SHORT_CONV):
        out = out + up[:, i:i + L] * w[i]
    return out


def hyena(hy, conv_w, conv_b, kf, skip_d):
    L = hy.shape[1]
    uc = short_conv(hy, conv_w, conv_b)
    v, x1, x2 = jnp.split(uc, 3, axis=-1)

    def longconv(z, kfo, d):
        zf = z.astype(jnp.float32)
        y = jnp.fft.irfft(jnp.fft.rfft(zf, n=2 * L, axis=1) * kfo, n=2 * L, axis=1)[:, :L]
        return (y + zf * d.astype(jnp.float32)).astype(z.dtype)

    z = x1 * longconv(v, kf[:, 0], skip_d[0])
    z = x2 * longconv(z, kf[:, 1], skip_d[1])
    return z


def moe(x, w_router, b_router, w_gu, b_gu, w_down, b_down):
    B, L, D = x.shape
    T = B * L
    xt = x.reshape(T, D)
    logits = (xt @ w_router).astype(jnp.float32) + b_router.astype(jnp.float32)
    top_v, top_i = lax.top_k(logits, TOP_K)
    gate = jax.nn.softmax(top_v, axis=-1)
    A = T * TOP_K
    flat_e = top_i.reshape(A).astype(jnp.int32)
    idx = jnp.arange(A, dtype=jnp.int32)
    se, order = lax.sort((flat_e, idx), num_keys=2)
    stok = order // TOP_K
    sg = gate.reshape(A)[order]
    counts = jnp.zeros((N_EXPERTS,), jnp.int32).at[flat_e].add(1)
    starts = jnp.cumsum(counts) - counts
    pcounts = (counts + MOE_CHUNK - 1) // MOE_CHUNK * MOE_CHUNK
    pends = jnp.cumsum(pcounts)
    pstarts = pends - pcounts
    dest = pstarts[se] + idx - starts[se]
    n_chunks = -(-A // MOE_CHUNK) + N_EXPERTS
    P = n_chunks * MOE_CHUNK
    slot_tok = jnp.full((P,), T, jnp.int32).at[dest].set(stok)
    slot_gate = jnp.zeros((P,), jnp.float32).at[dest].set(sg)
    x_slots = jnp.concatenate([xt, jnp.zeros((1, D), xt.dtype)], axis=0)[slot_tok].reshape(n_chunks, MOE_CHUNK, D)
    chunk_e = jnp.minimum(jnp.searchsorted(pends, jnp.arange(n_chunks, dtype=jnp.int32) * MOE_CHUNK, side='right'),
                          N_EXPERTS - 1).astype(jnp.int32)

    def expert_block(args):
        xc, gc, e = args
        gu = xc @ w_gu[e] + b_gu[e]
        g = jnp.minimum(gu[:, 0::2], SWIGLU_LIMIT)
        u = jnp.clip(gu[:, 1::2], -SWIGLU_LIMIT, SWIGLU_LIMIT)
        h = (u + 1.0) * (g * jax.nn.sigmoid(g * SWIGLU_ALPHA))
        y = h @ w_down[e] + b_down[e]
        return y * gc[:, None].astype(y.dtype)

    y_slots = lax.map(expert_block, (x_slots, slot_gate.reshape(n_chunks, MOE_CHUNK), chunk_e)).reshape(P, D)
    out = jax.ops.segment_sum(y_slots, slot_tok, num_segments=T + 1)[:T]
    return out.astype(x.dtype).reshape(B, L, D)


def encoder_layer(x, mem, rel_bias, w_in, attn_sink, conv_w, conv_b, filt_w1, filt_b1, filt_w2, filt_b2,
                  filt_w3, filt_b3, filt_freq, hyena_skip, w_mem_kv, w_branch, w_out, ln1_g, ln1_b,
                  w_router, b_router, w_gate_up, b_gate_up, w_down, b_down, ln2_g, ln2_b):
    B, L, D = x.shape
    h = x @ w_in
    q, k, v, hy, qm, g = jnp.split(h, IN_SPLITS, axis=-1)
    a = window_attention(q.reshape(B, L, N_Q_HEADS, HEAD_DIM), k.reshape(B, L, N_KV_HEADS, HEAD_DIM),
                         v.reshape(B, L, N_KV_HEADS, HEAD_DIM), attn_sink, rel_bias)
    kf = hyena_filters(L, filt_w1, filt_b1, filt_w2, filt_b2, filt_w3, filt_b3, filt_freq)
    hb = hyena(hy, conv_w, conv_b, kf, hyena_skip)
    c = memory_attention(qm, mem, w_mem_kv)
    proj = jnp.einsum('blnc,ncd->blnd', jnp.stack([a, hb, c], axis=2), w_branch)
    gates = jax.nn.sigmoid(g.reshape(B, L, N_BRANCHES, D))
    mix = jnp.sum(gates * proj, axis=2) @ w_out
    x = layer_norm(DN_ALPHA * x + mix, ln1_g, ln1_b)
    x = layer_norm(DN_ALPHA * x + moe(x, w_router, b_router, w_gate_up, b_gate_up, w_down, b_down), ln2_g, ln2_b)
    return x


def encoder_trunk(x, mem, ln_in_g, ln_in_b, rel_bias, w_in, attn_sink, conv_w, conv_b, filt_w1, filt_b1,
                  filt_w2, filt_b2, filt_w3, filt_b3, filt_freq, hyena_skip, w_mem_kv, w_branch, w_out,
                  ln1_g, ln1_b, w_router, b_router, w_gate_up, b_gate_up, w_down, b_down, ln2_g, ln2_b):
    x = layer_norm(x, ln_in_g, ln_in_b)
    for l in range(DEPTH):
        x = encoder_layer(x, mem, rel_bias, w_in[l], attn_sink[l], conv_w[l], conv_b[l], filt_w1[l], filt_b1[l],
                          filt_w2[l], filt_b2[l], filt_w3[l], filt_b3[l], filt_freq[l], hyena_skip[l],
                          w_mem_kv[l], w_branch[l], w_out[l], ln1_g[l], ln1_b[l], w_router[l], b_router[l],
                          w_gate_up[l], b_gate_up[l], w_down[l], b_down[l], ln2_g[l], ln2_b[l])
    return x


def setup_inputs(seed: int = 0) -> dict:
    key = jax.random.key(seed)
    ks = jax.random.split(key, 32)
    f32 = jnp.float32
    nrm = lambda k, shape, s: jax.random.normal(k, shape, f32) * s
    D = D_MODEL
    in_scale = np.ones((IN_W,), np.float32)
    in_scale[Q_W + KV_W:Q_W + 2 * KV_W] = DN_BETA
    kv_scale = np.ones((2 * MQ_W,), np.float32)
    kv_scale[MQ_W:] = DN_BETA
    return {
        'x_prompt': nrm(ks[0], (BATCH, SEQ, D), 1.0),
        'x_sample': nrm(ks[1], (DEC_BATCH, DEC_SEQ, D), 1.0),
        'mem_prompt': nrm(ks[2], (BATCH, N_MEM, D), 1.0),
        'mem_sample': nrm(ks[3], (DEC_BATCH, N_MEM, D), 1.0),
        'ln_in_g': 1.0 + nrm(ks[4], (D,), 0.02),
        'ln_in_b': nrm(ks[5], (D,), 0.02),
        'rel_bias': nrm(ks[6], (N_BUCKETS, N_Q_HEADS), 0.5),
        'w_in': nrm(ks[7], (DEPTH, D, IN_W), D ** -0.5) * jnp.asarray(in_scale),
        'attn_sink': nrm(ks[8], (DEPTH, N_Q_HEADS), 0.5),
        'conv_w': nrm(ks[9], (DEPTH, SHORT_CONV, 3 * HYENA_WIDTH), SHORT_CONV ** -0.5),
        'conv_b': nrm(ks[10], (DEPTH, 3 * HYENA_WIDTH), 0.02),
        'filt_w1': nrm(ks[11], (DEPTH, POS_EMB_DIM, FILTER_HIDDEN), POS_EMB_DIM ** -0.5),
        'filt_b1': nrm(ks[12], (DEPTH, FILTER_HIDDEN), 0.02),
        'filt_w2': nrm(ks[13], (DEPTH, FILTER_HIDDEN, FILTER_HIDDEN), FILTER_HIDDEN ** -0.5),
        'filt_b2': nrm(ks[14], (DEPTH, FILTER_HIDDEN), 0.02),
        'filt_w3': nrm(ks[15], (DEPTH, FILTER_HIDDEN, FILTER_OUT), FILTER_HIDDEN ** -0.5),
        'filt_b3': nrm(ks[16], (DEPTH, FILTER_OUT), 0.02),
        'filt_freq': 1.0 + nrm(ks[17], (DEPTH, 2, FILTER_HIDDEN), 0.02),
        'hyena_skip': nrm(ks[18], (DEPTH, HYENA_ORDER, HYENA_WIDTH), 0.5),
        'w_mem_kv': nrm(ks[19], (DEPTH, D, 2 * MQ_W), D ** -0.5) * jnp.asarray(kv_scale),
        'w_branch': nrm(ks[20], (DEPTH, N_BRANCHES, BRANCH_W, D), BRANCH_W ** -0.5 * DN_BETA),
        'w_out': nrm(ks[21], (DEPTH, D, D), D ** -0.5 * DN_BETA),
        'ln1_g': 1.0 + nrm(ks[22], (DEPTH, D), 0.02),
        'ln1_b': nrm(ks[23], (DEPTH, D), 0.02),
        'w_router': nrm(ks[24], (DEPTH, D, N_EXPERTS), D ** -0.5),
        'b_router': nrm(ks[25], (DEPTH, N_EXPERTS), 0.01),
        'w_gate_up': nrm(ks[26], (DEPTH, N_EXPERTS, D, 2 * D_EXPERT), D ** -0.5 * DN_BETA),
        'b_gate_up': nrm(ks[27], (DEPTH, N_EXPERTS, 2 * D_EXPERT), 0.01),
        'w_down': nrm(ks[28], (DEPTH, N_EXPERTS, D_EXPERT, D), D_EXPERT ** -0.5 * DN_BETA),
        'b_down': nrm(ks[29], (DEPTH, N_EXPERTS, D), 0.01),
        'ln2_g': 1.0 + nrm(ks[30], (DEPTH, D), 0.02),
        'ln2_b': nrm(ks[31], (DEPTH, D), 0.02),
    }


def reference(x_prompt, x_sample, mem_prompt, mem_sample, ln_in_g, ln_in_b, rel_bias, w_in, attn_sink, conv_w,
              conv_b, filt_w1, filt_b1, filt_w2, filt_b2, filt_w3, filt_b3, filt_freq, hyena_skip, w_mem_kv,
              w_branch, w_out, ln1_g, ln1_b, w_router, b_router, w_gate_up, b_gate_up, w_down, b_down,
              ln2_g, ln2_b):
    y_prompt = encoder_trunk(x_prompt, mem_prompt, ln_in_g, ln_in_b, rel_bias, w_in, attn_sink, conv_w, conv_b,
                             filt_w1, filt_b1, filt_w2, filt_b2, filt_w3, filt_b3, filt_freq, hyena_skip,
                             w_mem_kv, w_branch, w_out, ln1_g, ln1_b, w_router, b_router, w_gate_up,
                             b_gate_up, w_down, b_down, ln2_g, ln2_b)
    y_sample = encoder_trunk(x_sample, mem_sample, ln_in_g, ln_in_b, rel_bias, w_in, attn_sink, conv_w, conv_b,
                             filt_w1, filt_b1, filt_w2, filt_b2, filt_w3, filt_b3, filt_freq, hyena_skip,
                             w_mem_kv, w_branch, w_out, ln1_g, ln1_b, w_router, b_router, w_gate_up,
                             b_gate_up, w_down, b_down, ln2_g, ln2_b)
    return (y_prompt, y_sample)
```

```python
import functools
import math

import jax
import jax.numpy as jnp
from jax import lax
from jax.experimental import pallas as pl
from jax.experimental.pallas import tpu as pltpu

F32 = jnp.float32
BF16 = jnp.bfloat16
HI = lax.Precision.HIGHEST

D_MODEL = 1024
BRANCH_W = 512
N_Q_HEADS = 8
N_KV_HEADS = 2
HEAD_DIM = 64
WINDOW = 128
BLOCK = 128
N_BUCKETS = 32
MAX_DISTANCE = 128
HYENA_ORDER = 2
SHORT_CONV = 3
POS_EMB_DIM = 33
FILTER_HIDDEN = 64
DECAY_TARGET = 1e-2
FAST_DECAY_PCT = 0.3
SLOW_DECAY_PCT = 1.5
N_MEM_HEADS = 4
MEM_HEAD_DIM = BRANCH_W // N_MEM_HEADS
N_BRANCHES = 3
N_EXPERTS = 32
TOP_K = 4
D_EXPERT = 1024
SWIGLU_LIMIT = 7.0
SWIGLU_ALPHA = 1.702
LN_EPS = 1e-5
DEPTH = 1
DN_ALPHA = (2 * DEPTH) ** 0.25

Q_W = N_Q_HEADS * HEAD_DIM
KV_W = N_KV_HEADS * HEAD_DIM
IN_W = Q_W + 2 * KV_W + 3 * BRANCH_W + BRANCH_W + N_BRANCHES * D_MODEL

H_G = 0
H_Q = 6
H_QM = 7
H_HY = 16
H_K = 44
H_V = 45

FFT_N2 = 128
MOE_TILE = 512
LANE = 128
VMEM_LIMIT = 52 * 1024 * 1024


def _cparams(sem):
    return pltpu.CompilerParams(dimension_semantics=sem, vmem_limit_bytes=VMEM_LIMIT)


def _ln(x, g, b):
    mu = jnp.mean(x, axis=-1, keepdims=True)
    xc = x - mu
    var = jnp.mean(xc * xc, axis=-1, keepdims=True)
    return xc * lax.rsqrt(var + LN_EPS) * g + b


def _inproj_kernel(x_ref, g_ref, b_ref, w_ref, o_ref):
    xn = _ln(x_ref[...], g_ref[...], b_ref[...])
    o_ref[...] = jnp.dot(xn.astype(BF16), w_ref[...], preferred_element_type=F32).astype(o_ref.dtype)


def _inproj(x2, ln_g, ln_b, w_p, tm=512):
    T, D = x2.shape
    N = w_p.shape[1]
    tn = N // 2
    return pl.pallas_call(
        _inproj_kernel,
        grid=(N // tn, T // tm),
        in_specs=[
            pl.BlockSpec((tm, D), lambda j, i: (i, 0)),
            pl.BlockSpec((1, D), lambda j, i: (0, 0)),
            pl.BlockSpec((1, D), lambda j, i: (0, 0)),
            pl.BlockSpec((D, tn), lambda j, i: (0, j)),
        ],
        out_specs=pl.BlockSpec((tm, tn), lambda j, i: (i, j)),
        out_shape=jax.ShapeDtypeStruct((T, N), BF16),
        compiler_params=_cparams(("parallel", "parallel")),
        name="inproj",
    )(x2, ln_g, ln_b, w_p)


def _wattn_kernel(sink_ref, q_ref, kp_ref, kc_ref, kn_ref, vp_ref, vc_ref, vn_ref, bias_ref, o_ref, *, seq):
    n = pl.program_id(1)
    q = q_ref[0]
    kcat = jnp.concatenate([kp_ref[0], kc_ref[0], kn_ref[0]], axis=0)
    vcat = jnp.concatenate([vp_ref[0], vc_ref[0], vn_ref[0]], axis=0)
    qi = lax.broadcasted_iota(jnp.int32, (BLOCK, 3 * BLOCK), 0)
    kj = lax.broadcasted_iota(jnp.int32, (BLOCK, 3 * BLOCK), 1)
    rel = kj - BLOCK - qi
    kpos = (n - 1) * BLOCK + kj
    valid = (jnp.abs(rel) <= WINDOW) & (kpos >= 0) & (kpos < seq)
    group = N_Q_HEADS // N_KV_HEADS
    outs = []
    for h in range(N_Q_HEADS):
        g = h // group
        qh = q[:, h * HEAD_DIM:(h + 1) * HEAD_DIM]
        kh = kcat[:, g * HEAD_DIM:(g + 1) * HEAD_DIM]
        vh = vcat[:, g * HEAD_DIM:(g + 1) * HEAD_DIM]
        s = lax.dot_general(qh, kh, (((1,), (1,)), ((), ())), preferred_element_type=F32) * (HEAD_DIM ** -0.5)
        s = jnp.where(valid, s + bias_ref[h], F32(-1e30))
        sk = sink_ref[h]
        m = jnp.maximum(jnp.max(s, axis=-1, keepdims=True), sk)
        p = jnp.exp(s - m)
        denom = jnp.sum(p, axis=-1, keepdims=True) + jnp.exp(sk - m)
        o = jnp.dot(p.astype(BF16), vh, preferred_element_type=F32) / denom
        outs.append(o)
    o_ref[0] = jnp.concatenate(outs, axis=1).astype(o_ref.dtype)


def _wattn(h3, sink, bias):
    B, L, _ = h3.shape
    nb = L // BLOCK
    kv_spec = lambda col, shift: pl.BlockSpec(
        (1, BLOCK, KV_W), lambda b, n: (b, jnp.clip(n + shift, 0, nb - 1), col))
    return pl.pallas_call(
        functools.partial(_wattn_kernel, seq=L),
        grid=(B, nb),
        in_specs=[
            pl.BlockSpec(memory_space=pltpu.SMEM),
            pl.BlockSpec((1, BLOCK, Q_W), lambda b, n: (b, n, H_Q)),
            kv_spec(H_K, -1), kv_spec(H_K, 0), kv_spec(H_K, 1),
            kv_spec(H_V, -1), kv_spec(H_V, 0), kv_spec(H_V, 1),
            pl.BlockSpec((N_Q_HEADS, BLOCK, 3 * BLOCK), lambda b, n: (0, 0, 0)),
        ],
        out_specs=pl.BlockSpec((1, BLOCK, Q_W), lambda b, n: (b, n, 0)),
        out_shape=jax.ShapeDtypeStruct((B, L, Q_W), BF16),
        compiler_params=_cparams(("parallel", "parallel")),
        name="wattn",
    )(sink, h3, h3, h3, h3, h3, h3, h3, bias)


def _t5_bucket(rel):
    nb = N_BUCKETS // 2
    max_exact = nb // 2
    ret = jnp.where(rel > 0, nb, 0)
    n = jnp.abs(rel)
    nf = jnp.maximum(n, 1).astype(F32)
    large = max_exact + (jnp.log(nf / max_exact) / math.log(MAX_DISTANCE / max_exact)
                         * (nb - max_exact)).astype(jnp.int32)
    large = jnp.minimum(large, nb - 1)
    return ret + jnp.where(n < max_exact, n, large)


def _bias_table(rel_bias):
    qi = jnp.arange(BLOCK, dtype=jnp.int32)[:, None]
    kj = jnp.arange(3 * BLOCK, dtype=jnp.int32)[None, :]
    bias = rel_bias[_t5_bucket(kj - BLOCK - qi)].astype(F32)
    return jnp.transpose(bias, (2, 0, 1))


def _memattn_kernel(q_ref, mem_ref, w_ref, o_ref, kv_ref):
    @pl.when(pl.program_id(1) == 0)
    def _():
        kv_ref[...] = jnp.dot(mem_ref[0].astype(BF16), w_ref[...],
                              preferred_element_type=F32).astype(kv_ref.dtype)

    q = q_ref[0]
    outs = []
    for h in range(N_MEM_HEADS):
        lo = h * MEM_HEAD_DIM
        qh = q[:, lo:lo + MEM_HEAD_DIM]
        kh = kv_ref[:, lo:lo + MEM_HEAD_DIM]
        vh = kv_ref[:, BRANCH_W + lo:BRANCH_W + lo + MEM_HEAD_DIM]
        s = lax.dot_general(qh, kh, (((1,), (1,)), ((), ())), preferred_element_type=F32) * (MEM_HEAD_DIM ** -0.5)
        m = jnp.max(s, axis=-1, keepdims=True)
        p = jnp.exp(s - m)
        denom = jnp.sum(p, axis=-1, keepdims=True)
        outs.append(jnp.dot(p.astype(BF16), vh, preferred_element_type=F32) / denom)
    o_ref[0] = jnp.concatenate(outs, axis=1).astype(o_ref.dtype)


def _memattn(h3, mem, w_kv, tm=512):
    B, L, _ = h3.shape
    M = mem.shape[1]
    tm = min(tm, L)
    return pl.pallas_call(
        _memattn_kernel,
        grid=(B, L // tm),
        in_specs=[
            pl.BlockSpec((1, tm, BRANCH_W), lambda b, l: (b, l, H_QM)),
            pl.BlockSpec((1, M, D_MODEL), lambda b, l: (b, 0, 0)),
            pl.BlockSpec((D_MODEL, 2 * BRANCH_W), lambda b, l: (0, 0)),
        ],
        out_specs=pl.BlockSpec((1, tm, BRANCH_W), lambda b, l: (b, l, 0)),
        out_shape=jax.ShapeDtypeStruct((B, L, BRANCH_W), BF16),
        scratch_shapes=[pltpu.VMEM((M, 2 * BRANCH_W), BF16)],
        compiler_params=_cparams(("parallel", "arbitrary")),
        name="memattn",
    )(h3, mem, w_kv)


HALO = 16


def _sconv_kernel(cur_ref, prev_ref, next_ref, w_ref, b_ref, o_ref, *, nt):
    t = pl.program_id(2)
    u = cur_ref[0].astype(F32)
    tl = u.shape[0]
    pr = prev_ref[0].astype(F32)[HALO - 1:HALO]
    nx = next_ref[0].astype(F32)[0:1]
    pr = jnp.where(t == 0, 0.0, pr)
    nx = jnp.where(t == nt - 1, 0.0, nx)
    row = lax.broadcasted_iota(jnp.int32, u.shape, 0)
    up = jnp.where(row == 0, pr, pltpu.roll(u, 1, 0))
    un = jnp.where(row == tl - 1, nx, pltpu.roll(u, tl - 1, 0))
    w = w_ref[...]
    o_ref[0, 0] = (b_ref[...] + up * w[0:1] + u * w[1:2] + un * w[2:3]).astype(o_ref.dtype)


def _sconv(h3, conv_w, conv_b, tl=1024, cb=256):
    B, L, _ = h3.shape
    tl = min(tl, L)
    nt = L // tl
    ncb = 3 * BRANCH_W // cb
    per = BRANCH_W // cb
    return pl.pallas_call(
        functools.partial(_sconv_kernel, nt=nt),
        grid=(B, ncb, nt),
        in_specs=[
            pl.BlockSpec((1, tl, cb), lambda b, j, t: (b, t, H_HY + j)),
            pl.BlockSpec((1, HALO, cb), lambda b, j, t: (b, jnp.maximum(t * (tl // HALO) - 1, 0), H_HY + j)),
            pl.BlockSpec((1, HALO, cb), lambda b, j, t: (b, jnp.minimum((t + 1) * (tl // HALO), L // HALO - 1), H_HY + j)),
            pl.BlockSpec((SHORT_CONV, cb), lambda b, j, t: (0, j)),
            pl.BlockSpec((1, cb), lambda b, j, t: (0, j)),
        ],
        out_specs=pl.BlockSpec((1, 1, tl, cb), lambda b, j, t: (j // per, b, t, j % per)),
        out_shape=jax.ShapeDtypeStruct((3, B, L, BRANCH_W), BF16),
        compiler_params=_cparams(("parallel", "parallel", "parallel")),
        name="sconv",
    )(h3, h3, h3, conv_w, conv_b)


def _filt_kernel(w1_ref, b1_ref, w2_ref, b2_ref, w3_ref, b3_ref, fr_ref, fq_ref, dl_ref, k_ref, s_ref, *, seq, tr):
    i = pl.program_id(0)
    m = i * tr + lax.broadcasted_iota(jnp.int32, (tr, LANE), 0)
    lane = lax.broadcasted_iota(jnp.int32, (tr, LANE), 1)
    p = jnp.where(m < seq, m, 2 * seq - m).astype(F32)
    t = p / F32(seq - 1)
    w = F32(2.0 * math.pi) * p / F32(seq)
    arg = fq_ref[...] * w
    bands = (POS_EMB_DIM - 1) // 2
    z = jnp.where(lane == 0, t,
                  jnp.where(lane <= bands, jnp.cos(arg),
                            jnp.where(lane <= 2 * bands, -jnp.sin(arg), 0.0)))
    fr = fr_ref[...]
    h = jnp.sin(fr[0:1] * (jnp.dot(z, w1_ref[...], precision=HI, preferred_element_type=F32) + b1_ref[...]))
    h = jnp.sin(fr[1:2] * (jnp.dot(h, w2_ref[...], precision=HI, preferred_element_type=F32) + b2_ref[...]))
    h = jnp.dot(h, w3_ref[...], precision=HI, preferred_element_type=F32) + b3_ref[...]
    mc = m[:, 0:1]
    tc = t[:, 0:1]
    decay = jnp.exp(-tc * dl_ref[...])
    oc = HYENA_ORDER * BRANCH_W

    @pl.when(i == 0)
    def _():
        s_ref[...] = jnp.zeros_like(s_ref)

    for o in range(HYENA_ORDER):
        fwd = h[:, o * BRANCH_W:(o + 1) * BRANCH_W]
        bwd = h[:, oc + o * BRANCH_W:oc + (o + 1) * BRANCH_W]
        k = jnp.where(mc < seq, fwd, bwd) * decay
        k = jnp.where(mc == seq, 0.0, k)
        k_ref[o] = k
        s_ref[o] += jnp.sum(jnp.abs(k), axis=0, keepdims=True)


def _filters(seq, w1, b1, w2, b2, w3, b3, freq, tr=256):
    n = 2 * seq
    tr = min(tr, n)
    bands = (POS_EMB_DIM - 1) // 2
    hid = LANE
    w1p = jnp.zeros((LANE, hid), F32).at[:POS_EMB_DIM, :FILTER_HIDDEN].set(w1.astype(F32))
    b1p = jnp.zeros((1, hid), F32).at[0, :FILTER_HIDDEN].set(b1.astype(F32))
    w2p = jnp.zeros((hid, hid), F32).at[:FILTER_HIDDEN, :FILTER_HIDDEN].set(w2.astype(F32))
    b2p = jnp.zeros((1, hid), F32).at[0, :FILTER_HIDDEN].set(b2.astype(F32))
    w3p = jnp.zeros((hid, w3.shape[1]), F32).at[:FILTER_HIDDEN].set(w3.astype(F32))
    b3p = b3.astype(F32)[None]
    frp = jnp.zeros((2, hid), F32).at[:, :FILTER_HIDDEN].set(freq.astype(F32))
    f = jnp.linspace(1e-4, bands - 1, bands, dtype=F32)
    fq = jnp.zeros((1, LANE), F32).at[0, 1:1 + bands].set(f).at[0, 1 + bands:1 + 2 * bands].set(f)
    deltas = jnp.abs(jnp.linspace(math.log(DECAY_TARGET) / FAST_DECAY_PCT,
                                  math.log(DECAY_TARGET) / SLOW_DECAY_PCT, BRANCH_W, dtype=F32))[None]
    full = lambda a: pl.BlockSpec(a.shape, lambda i: (0,) * a.ndim)
    args = (w1p, b1p, w2p, b2p, w3p, b3p, frp, fq, deltas)
    return pl.pallas_call(
        functools.partial(_filt_kernel, seq=seq, tr=tr),
        grid=(n // tr,),
        in_specs=[full(a) for a in args],
        out_specs=[pl.BlockSpec((HYENA_ORDER, tr, BRANCH_W), lambda i: (0, i, 0)),
                   pl.BlockSpec((HYENA_ORDER, 1, BRANCH_W), lambda i: (0, 0, 0))],
        out_shape=[jax.ShapeDtypeStruct((HYENA_ORDER, n, BRANCH_W), F32),
                   jax.ShapeDtypeStruct((HYENA_ORDER, 1, BRANCH_W), F32)],
        compiler_params=_cparams(("arbitrary",)),
        name="hyena_filter",
    )(*args)


def _dft_tables(seq):
    n1 = 2 * seq // FFT_N2
    n1h = seq // FFT_N2
    n = 2 * seq
    k1 = jnp.arange(n1, dtype=jnp.int32)
    th = (2.0 * math.pi / n1) * ((k1[:, None] * k1[None, :]) % n1).astype(F32)
    c1, s1 = jnp.cos(th), jnp.sin(th)
    ch, sh = c1[:, :n1h], s1[:, :n1h]
    f1 = jnp.concatenate([jnp.concatenate([ch, sh], 1), jnp.concatenate([-sh, ch], 1)], 0)
    f1_real = jnp.concatenate([c1, -s1], 0)
    cht, sht = ch.T / n, sh.T / n
    g1 = jnp.concatenate([jnp.concatenate([cht, -sht], 1), jnp.concatenate([sht, cht], 1)], 0)
    n2 = jnp.arange(FFT_N2, dtype=jnp.int32)
    kk = k1[:, None, None] + n1 * n2[None, :, None]
    ph = (2.0 * math.pi / n) * ((kk * n2[None, None, :]) % n).astype(F32)
    c2, s2 = jnp.cos(ph), jnp.sin(ph)
    m2 = jnp.concatenate([jnp.concatenate([c2, s2], 2), jnp.concatenate([-s2, c2], 2)], 1)
    return f1, f1_real, g1, m2, jnp.swapaxes(m2, 1, 2)


def _s1_kernel(z_ref, f_ref, o_ref, *, hi):
    nb, r, w = z_ref.shape
    z = z_ref[...].reshape(nb * r, w) if nb > 1 else z_ref[0]
    if hi:
        res = jnp.dot(f_ref[...], z, precision=HI, preferred_element_type=F32)
    else:
        res = jnp.dot(f_ref[...], z.astype(BF16), preferred_element_type=F32)
    half = res.shape[0] // 2
    o_ref[0, 0] = res[:half].astype(o_ref.dtype)
    o_ref[0, 1] = res[half:].astype(o_ref.dtype)


def _fft_s1(z, f, *, pair, hi=False, n2t=8):
    bx, lz, c = z.shape
    r = lz // FFT_N2
    nb = 2 if pair else 1
    n1 = f.shape[0] // 2
    w = n2t * c
    zz = z.reshape(bx, r, FFT_N2 * c)
    out_dtype = F32 if hi else BF16
    return pl.pallas_call(
        functools.partial(_s1_kernel, hi=hi),
        grid=(bx // nb, FFT_N2 // n2t),
        in_specs=[pl.BlockSpec((nb, r, w), lambda p, j: (p, 0, j)),
                  pl.BlockSpec(f.shape, lambda p, j: (0, 0))],
        out_specs=pl.BlockSpec((1, 2, n1, w), lambda p, j: (p, 0, 0, j)),
        out_shape=jax.ShapeDtypeStruct((bx // nb, 2, n1, FFT_N2 * c), out_dtype),
        compiler_params=_cparams(("parallel", "parallel")),
        name="fft_stage1",
    )(zz, f.astype(F32 if hi else BF16))


def _mid_kernel(a_ref, m_ref, mt_ref, kf_ref, o_ref):
    k1t = a_ref.shape[2]
    h = FFT_N2
    for j in range(k1t):
        a = jnp.concatenate([a_ref[0, 0, j], a_ref[0, 1, j]], axis=0)
        x = jnp.dot(m_ref[j], a, preferred_element_type=F32)
        xr, xi = x[:h], x[h:]
        kr, ki = kf_ref[0, 0, j], kf_ref[0, 1, j]
        y = jnp.concatenate([xr * kr - xi * ki, xr * ki + xi * kr], axis=0).astype(BF16)
        b = jnp.dot(mt_ref[j], y, preferred_element_type=F32)
        o_ref[0, 0, j] = b[:h].astype(o_ref.dtype)
        o_ref[0, 1, j] = b[h:].astype(o_ref.dtype)


def _mid_fwd_kernel(a_ref, m_ref, o_ref):
    k1t = a_ref.shape[2]
    h = FFT_N2
    for j in range(k1t):
        a = jnp.concatenate([a_ref[0, 0, j], a_ref[0, 1, j]], axis=0)
        x = jnp.dot(m_ref[j], a, precision=HI, preferred_element_type=F32)
        o_ref[0, 0, j] = x[:h]
        o_ref[0, 1, j] = x[h:]


def _fft_mid(a, m2, m2t, kf, order, k1t=8):
    p, _, n1, wc = a.shape
    c = wc // FFT_N2
    k1t = min(k1t, n1)
    a5 = a.reshape(p, 2, n1, FFT_N2, c)
    blk = (1, 2, k1t, FFT_N2, c)
    out = pl.pallas_call(
        _mid_kernel,
        grid=(n1 // k1t, p),
        in_specs=[pl.BlockSpec(blk, lambda k, q: (q, 0, k, 0, 0)),
                  pl.BlockSpec((k1t, 2 * FFT_N2, 2 * FFT_N2), lambda k, q: (k, 0, 0)),
                  pl.BlockSpec((k1t, 2 * FFT_N2, 2 * FFT_N2), lambda k, q: (k, 0, 0)),
                  pl.BlockSpec(blk, lambda k, q: (order, 0, k, 0, 0))],
        out_specs=pl.BlockSpec(blk, lambda k, q: (q, 0, k, 0, 0)),
        out_shape=jax.ShapeDtypeStruct(a5.shape, BF16),
        compiler_params=_cparams(("parallel", "parallel")),
        name="fft_mid",
    )(a5, m2.astype(BF16), m2t.astype(BF16), kf)
    return out.reshape(a.shape)


def _fft_mid_fwd(a, m2, k1t=8):
    p, _, n1, wc = a.shape
    c = wc // FFT_N2
    k1t = min(k1t, n1)
    a5 = a.reshape(p, 2, n1, FFT_N2, c)
    blk = (1, 2, k1t, FFT_N2, c)
    return pl.pallas_call(
        _mid_fwd_kernel,
        grid=(n1 // k1t, p),
        in_specs=[pl.BlockSpec(blk, lambda k, q: (q, 0, k, 0, 0)),
                  pl.BlockSpec((k1t, 2 * FFT_N2, 2 * FFT_N2), lambda k, q: (k, 0, 0))],
        out_specs=pl.BlockSpec(blk, lambda k, q: (q, 0, k, 0, 0)),
        out_shape=jax.ShapeDtypeStruct(a5.shape, F32),
        compiler_params=_cparams(("parallel", "parallel")),
        name="fft_mid_filter",
    )(a5, m2)


def _s3_kernel(b_ref, g_ref, v_ref, x_ref, sc_ref, d_ref, o_ref):
    bm = jnp.concatenate([b_ref[0, 0], b_ref[0, 1]], axis=0)
    y = jnp.dot(g_ref[...], bm, preferred_element_type=F32)
    r = v_ref.shape[1]
    for part in range(2):
        yp = y[part * r:(part + 1) * r]
        v = v_ref[part].astype(F32)
        x = x_ref[part].astype(F32)
        o_ref[part] = (x * (yp * sc_ref[...] + v * d_ref[...])).astype(o_ref.dtype)


def _fft_s3(bm, g1, v, x, inv_norm, skip, n2t=8):
    p, _, n1, wc = bm.shape
    bx, seq, c = v.shape
    r = seq // FFT_N2
    w = n2t * c
    vv = v.reshape(bx, r, FFT_N2 * c)
    xx = x.reshape(bx, r, FFT_N2 * c)
    sc = jnp.tile(inv_norm.astype(F32).reshape(1, c), (1, n2t))
    dd = jnp.tile(skip.astype(F32).reshape(1, c), (1, n2t))
    io = pl.BlockSpec((2, r, w), lambda q, j: (q, 0, j))
    out = pl.pallas_call(
        _s3_kernel,
        grid=(p, FFT_N2 // n2t),
        in_specs=[pl.BlockSpec((1, 2, n1, w), lambda q, j: (q, 0, 0, j)),
                  pl.BlockSpec(g1.shape, lambda q, j: (0, 0)),
                  io, io,
                  pl.BlockSpec((1, w), lambda q, j: (0, 0)),
                  pl.BlockSpec((1, w), lambda q, j: (0, 0))],
        out_specs=io,
        out_shape=jax.ShapeDtypeStruct(vv.shape, BF16),
        compiler_params=_cparams(("parallel", "parallel")),
        name="fft_stage3",
    )(bm, g1.astype(BF16), vv, xx, sc, dd)
    return out.reshape(bx, seq, c)


def _hyena(h3, conv_w, conv_b, filt, skip):
    seq = h3.shape[1]
    f1, f1_real, g1, m2, m2t = _dft_tables(seq)
    taps, asum = _filters(seq, *filt)
    kf = _fft_mid_fwd(_fft_s1(taps, f1_real, pair=False, hi=True), m2)
    inv_norm = 1.0 / asum[:, 0]
    uc = _sconv(h3, conv_w, conv_b)
    z = uc[0]
    for o in range(HYENA_ORDER):
        spec = _fft_s1(z, f1, pair=True)
        spec = _fft_mid(spec, m2, m2t, kf, o)
        z = _fft_s3(spec, g1, z, uc[1 + o], inv_norm[o], skip[o])
    return z


def _mix_kernel(x_ref, a_ref, hb_ref, c_ref, g0_ref, g1_ref, g2_ref, wb_ref, wo_ref,
                lig_ref, lib_ref, l1g_ref, l1b_ref, wr_ref, br_ref, x1_ref, x1b_ref, lg_ref):
    acc = None
    for n, (br, gr) in enumerate(((a_ref, g0_ref), (hb_ref, g1_ref), (c_ref, g2_ref))):
        proj = jnp.dot(br[...], wb_ref[n], preferred_element_type=F32)
        term = jax.nn.sigmoid(gr[...].astype(F32)) * proj
        acc = term if acc is None else acc + term
    mix = jnp.dot(acc.astype(BF16), wo_ref[...], preferred_element_type=F32)
    xn = _ln(x_ref[...], lig_ref[...], lib_ref[...])
    x1 = _ln(DN_ALPHA * xn + mix, l1g_ref[...], l1b_ref[...])
    x1_ref[...] = x1
    x1b_ref[...] = x1.astype(BF16)
    lg_ref[...] = jnp.dot(x1, wr_ref[...], precision=HI, preferred_element_type=F32) + br_ref[...]


def _mix(x2, a, hb, c, h2, wb, wo, lig, lib, l1g, l1b, wr, br, tm=512):
    T, D = x2.shape
    row = lambda wdt, col: pl.BlockSpec((tm, wdt), lambda i: (i, col))
    const = lambda arr: pl.BlockSpec(arr.shape, lambda i: (0,) * arr.ndim)
    return pl.pallas_call(
        _mix_kernel,
        grid=(T // tm,),
        in_specs=[row(D, 0), row(BRANCH_W, 0), row(BRANCH_W, 0), row(BRANCH_W, 0),
                  row(D, H_G), row(D, H_G + 1), row(D, H_G + 2),
                  const(wb), const(wo), const(lig), const(lib), const(l1g), const(l1b), const(wr), const(br)],
        out_specs=[row(D, 0), row(D, 0), row(LANE, 0)],
        out_shape=[jax.ShapeDtypeStruct((T, D), F32), jax.ShapeDtypeStruct((T, D), BF16),
                   jax.ShapeDtypeStruct((T, LANE), F32)],
        compiler_params=_cparams(("parallel",)),
        name="mix",
    )(x2, a, hb, c, h2, h2, h2, wb, wo, lig, lib, l1g, l1b, wr, br)


def _ffn_kernel(ce_ref, x_ref, gt_ref, wg_ref, bg_ref, wu_ref, bu_ref, wd_ref, bd_ref, o_ref):
    del ce_ref
    x = x_ref[...]
    g = jnp.dot(x, wg_ref[0], preferred_element_type=F32) + bg_ref[0]
    u = jnp.dot(x, wu_ref[0], preferred_element_type=F32) + bu_ref[0]
    g = jnp.minimum(g, SWIGLU_LIMIT)
    u = jnp.clip(u, -SWIGLU_LIMIT, SWIGLU_LIMIT)
    hmid = (u + 1.0) * (g * jax.nn.sigmoid(g * SWIGLU_ALPHA))
    y = jnp.dot(hmid.astype(BF16), wd_ref[0], preferred_element_type=F32) + bd_ref[0]
    o_ref[...] = (y * gt_ref[...]).astype(o_ref.dtype)


def _ffn(chunk_e, x_slots, slot_gate, wg, bg, wu, bu, wd, bd):
    P, D = x_slots.shape
    F = wg.shape[2]
    n_chunks = P // MOE_TILE
    wspec = lambda a, b: pl.BlockSpec((1, a, b), lambda i, ce: (ce[i], 0, 0))
    grid_spec = pltpu.PrefetchScalarGridSpec(
        num_scalar_prefetch=1,
        grid=(n_chunks,),
        in_specs=[pl.BlockSpec((MOE_TILE, D), lambda i, ce: (i, 0)),
                  pl.BlockSpec((MOE_TILE, 1), lambda i, ce: (i, 0)),
                  wspec(D, F), wspec(1, F), wspec(D, F), wspec(1, F), wspec(F, D), wspec(1, D)],
        out_specs=pl.BlockSpec((MOE_TILE, D), lambda i, ce: (i, 0)),
    )
    return pl.pallas_call(
        _ffn_kernel,
        grid_spec=grid_spec,
        out_shape=jax.ShapeDtypeStruct((P, D), BF16),
        compiler_params=_cparams(("arbitrary",)),
        name="moe_ffn",
    )(chunk_e, x_slots, slot_gate, wg, bg, wu, bu, wd, bd)


def _final_kernel(x1_ref, y_ref, g_ref, b_ref, o_ref):
    moe = jnp.sum(y_ref[...].astype(F32), axis=1)
    o_ref[...] = _ln(DN_ALPHA * x1_ref[...] + moe, g_ref[...], b_ref[...])


def _final(x1, yg, g, b, tm=512):
    T, D = x1.shape
    return pl.pallas_call(
        _final_kernel,
        grid=(T // tm,),
        in_specs=[pl.BlockSpec((tm, D), lambda i: (i, 0)),
                  pl.BlockSpec((tm, TOP_K, D), lambda i: (i, 0, 0)),
                  pl.BlockSpec((1, D), lambda i: (0, 0)),
                  pl.BlockSpec((1, D), lambda i: (0, 0))],
        out_specs=pl.BlockSpec((tm, D), lambda i: (i, 0)),
        out_shape=jax.ShapeDtypeStruct((T, D), F32),
        compiler_params=_cparams(("parallel",)),
        name="final_ln",
    )(x1, yg, g, b)


def _moe(x1, x1b, logits, ffn_w, ln2_g, ln2_b):
    T, D = x1.shape
    top_v, top_i = lax.top_k(logits[:, :N_EXPERTS], TOP_K)
    gate = jax.nn.softmax(top_v, axis=-1)
    A = T * TOP_K
    flat_e = top_i.reshape(A).astype(jnp.int32)
    idx = jnp.arange(A, dtype=jnp.int32)
    se, order = lax.sort((flat_e, idx), num_keys=2)
    counts = jnp.zeros((N_EXPERTS,), jnp.int32).at[flat_e].add(1)
    starts = jnp.cumsum(counts) - counts
    pcounts = (counts + MOE_TILE - 1) // MOE_TILE * MOE_TILE
    pends = jnp.cumsum(pcounts)
    pstarts = pends - pcounts
    dest = pstarts[se] + idx - starts[se]
    n_chunks = -(-A // MOE_TILE) + N_EXPERTS
    P = n_chunks * MOE_TILE
    slot_tok = jnp.full((P,), T, jnp.int32).at[dest].set(order // TOP_K)
    slot_gate = jnp.zeros((P,), F32).at[dest].set(gate.reshape(A)[order])
    pos = jnp.zeros((A,), jnp.int32).at[order].set(dest)
    chunk_e = jnp.minimum(jnp.searchsorted(pends, jnp.arange(n_chunks, dtype=jnp.int32) * MOE_TILE, side='right'),
                          N_EXPERTS - 1).astype(jnp.int32)
    x_ext = jnp.concatenate([x1b, jnp.zeros((1, D), x1b.dtype)], axis=0)
    x_slots = x_ext[slot_tok]
    y_slots = _ffn(chunk_e, x_slots, slot_gate[:, None], *ffn_w)
    yg = y_slots[pos].reshape(T, TOP_K, D)
    return _final(x1, yg, ln2_g, ln2_b)


def _trunk(x, mem, p):
    B, L, D = x.shape
    x2 = x.reshape(B * L, D)
    h2 = _inproj(x2, p["ln_in_g"], p["ln_in_b"], p["w_in"])
    h3 = h2.reshape(B, L, IN_W)
    a = _wattn(h3, p["sink"], p["bias"])
    hb = _hyena(h3, p["conv_w"], p["conv_b"], p["filt"], p["skip"])
    c = _memattn(h3, mem, p["w_mem_kv"])
    x1, x1b, logits = _mix(x2, a.reshape(B * L, -1), hb.reshape(B * L, -1), c.reshape(B * L, -1), h2,
                           p["w_branch"], p["w_out"], p["ln_in_g"], p["ln_in_b"], p["ln1_g"], p["ln1_b"],
                           p["w_router"], p["b_router"])
    out = _moe(x1, x1b, logits, p["ffn"], p["ln2_g"], p["ln2_b"])
    return out.reshape(B, L, D)


def _prep(ln_in_g, ln_in_b, rel_bias, w_in, attn_sink, conv_w, conv_b, filt_w1, filt_b1, filt_w2, filt_b2,
          filt_w3, filt_b3, filt_freq, hyena_skip, w_mem_kv, w_branch, w_out, ln1_g, ln1_b, w_router, b_router,
          w_gate_up, b_gate_up, w_down, b_down, ln2_g, ln2_b):
    w = w_in
    g_lo = Q_W + 2 * KV_W + 3 * BRANCH_W + BRANCH_W
    hy_lo = Q_W + 2 * KV_W
    w_p = jnp.concatenate([w[:, g_lo:], w[:, :Q_W], w[:, hy_lo + 3 * BRANCH_W:g_lo],
                           w[:, hy_lo:hy_lo + 3 * BRANCH_W], w[:, Q_W:hy_lo]], axis=1).astype(BF16)
    row = lambda v: v.astype(F32)[None]
    wr = jnp.zeros((D_MODEL, LANE), F32).at[:, :N_EXPERTS].set(w_router.astype(F32))
    br = jnp.full((1, LANE), -1e30, F32).at[0, :N_EXPERTS].set(b_router.astype(F32))
    return dict(
        ln_in_g=row(ln_in_g), ln_in_b=row(ln_in_b), w_in=w_p,
        sink=attn_sink.astype(F32), bias=_bias_table(rel_bias),
        conv_w=conv_w.astype(F32), conv_b=row(conv_b),
        filt=(filt_w1, filt_b1, filt_w2, filt_b2, filt_w3, filt_b3, filt_freq), skip=hyena_skip,
        w_mem_kv=w_mem_kv.astype(BF16), w_branch=w_branch.astype(BF16), w_out=w_out.astype(BF16),
        ln1_g=row(ln1_g), ln1_b=row(ln1_b), w_router=wr, b_router=br,
        ffn=(w_gate_up[:, :, 0::2].astype(BF16), b_gate_up[:, None, 0::2].astype(F32),
             w_gate_up[:, :, 1::2].astype(BF16), b_gate_up[:, None, 1::2].astype(F32),
             w_down.astype(BF16), b_down[:, None, :].astype(F32)),
        ln2_g=row(ln2_g), ln2_b=row(ln2_b),
    )


def kernel(x_prompt, x_sample, mem_prompt, mem_sample, ln_in_g, ln_in_b, rel_bias, w_in, attn_sink, conv_w, conv_b, filt_w1, filt_b1, filt_w2, filt_b2, filt_w3, filt_b3, filt_freq, hyena_skip, w_mem_kv, w_branch, w_out, ln1_g, ln1_b, w_router, b_router, w_gate_up, b_gate_up, w_down, b_down, ln2_g, ln2_b):
    p = _prep(ln_in_g, ln_in_b, rel_bias, w_in[0], attn_sink[0], conv_w[0], conv_b[0], filt_w1[0], filt_b1[0],
              filt_w2[0], filt_b2[0], filt_w3[0], filt_b3[0], filt_freq[0], hyena_skip[0], w_mem_kv[0],
              w_branch[0], w_out[0], ln1_g[0], ln1_b[0], w_router[0], b_router[0], w_gate_up[0], b_gate_up[0],
              w_down[0], b_down[0], ln2_g[0], ln2_b[0])
    return (_trunk(x_prompt, mem_prompt, p), _trunk(x_sample, mem_sample, p))
```

```python
import functools
import math

import jax
import jax.numpy as jnp
from jax import lax
from jax.experimental import pallas as pl
from jax.experimental.pallas import tpu as pltpu

F32 = jnp.float32
BF16 = jnp.bfloat16
HI = lax.Precision.HIGHEST

D_MODEL = 1024
BRANCH_W = 512
N_Q_HEADS = 8
N_KV_HEADS = 2
HEAD_DIM = 64
WINDOW = 128
BLOCK = 128
N_BUCKETS = 32
MAX_DISTANCE = 128
HYENA_ORDER = 2
SHORT_CONV = 3
POS_EMB_DIM = 33
FILTER_HIDDEN = 64
DECAY_TARGET = 1e-2
FAST_DECAY_PCT = 0.3
SLOW_DECAY_PCT = 1.5
N_MEM_HEADS = 4
MEM_HEAD_DIM = BRANCH_W // N_MEM_HEADS
N_BRANCHES = 3
N_EXPERTS = 32
TOP_K = 4
D_EXPERT = 1024
SWIGLU_LIMIT = 7.0
SWIGLU_ALPHA = 1.702
LN_EPS = 1e-5
DEPTH = 1
DN_ALPHA = (2 * DEPTH) ** 0.25

Q_W = N_Q_HEADS * HEAD_DIM
KV_W = N_KV_HEADS * HEAD_DIM
IN_W = Q_W + 2 * KV_W + 3 * BRANCH_W + BRANCH_W + N_BRANCHES * D_MODEL

H_G = 0
H_Q = 6
H_QM = 7
H_HY = 16
H_K = 44
H_V = 45

FFT_N2 = 128
MOE_TILE = 512
LANE = 128
VMEM_LIMIT = 52 * 1024 * 1024


def _cparams(sem):
    return pltpu.CompilerParams(dimension_semantics=sem, vmem_limit_bytes=VMEM_LIMIT)


def _ln(x, g, b):
    mu = jnp.mean(x, axis=-1, keepdims=True)
    xc = x - mu
    var = jnp.mean(xc * xc, axis=-1, keepdims=True)
    return xc * lax.rsqrt(var + LN_EPS) * g + b


def _inproj_kernel(x_ref, g_ref, b_ref, w_ref, o_ref):
    xn = _ln(x_ref[...], g_ref[...], b_ref[...])
    o_ref[...] = jnp.dot(xn.astype(BF16), w_ref[...], preferred_element_type=F32).astype(o_ref.dtype)


def _inproj(x2, ln_g, ln_b, w_p, tm=512):
    T, D = x2.shape
    N = w_p.shape[1]
    tn = N // 2
    return pl.pallas_call(
        _inproj_kernel,
        grid=(N // tn, T // tm),
        in_specs=[
            pl.BlockSpec((tm, D), lambda j, i: (i, 0)),
            pl.BlockSpec((1, D), lambda j, i: (0, 0)),
            pl.BlockSpec((1, D), lambda j, i: (0, 0)),
            pl.BlockSpec((D, tn), lambda j, i: (0, j)),
        ],
        out_specs=pl.BlockSpec((tm, tn), lambda j, i: (i, j)),
        out_shape=jax.ShapeDtypeStruct((T, N), BF16),
        compiler_params=_cparams(("parallel", "parallel")),
        name="inproj",
    )(x2, ln_g, ln_b, w_p)


def _wattn_kernel(sink_ref, q_ref, kp_ref, kc_ref, kn_ref, vp_ref, vc_ref, vn_ref, bias_ref, o_ref, *, seq):
    n = pl.program_id(1)
    q = q_ref[0]
    kcat = jnp.concatenate([kp_ref[0], kc_ref[0], kn_ref[0]], axis=0)
    vcat = jnp.concatenate([vp_ref[0], vc_ref[0], vn_ref[0]], axis=0)
    qi = lax.broadcasted_iota(jnp.int32, (BLOCK, 3 * BLOCK), 0)
    kj = lax.broadcasted_iota(jnp.int32, (BLOCK, 3 * BLOCK), 1)
    rel = kj - BLOCK - qi
    kpos = (n - 1) * BLOCK + kj
    valid = (jnp.abs(rel) <= WINDOW) & (kpos >= 0) & (kpos < seq)
    group = N_Q_HEADS // N_KV_HEADS
    outs = []
    for h in range(N_Q_HEADS):
        g = h // group
        qh = q[:, h * HEAD_DIM:(h + 1) * HEAD_DIM]
        kh = kcat[:, g * HEAD_DIM:(g + 1) * HEAD_DIM]
        vh = vcat[:, g * HEAD_DIM:(g + 1) * HEAD_DIM]
        s = lax.dot_general(qh, kh, (((1,), (1,)), ((), ())), preferred_element_type=F32) * (HEAD_DIM ** -0.5)
        s = jnp.where(valid, s + bias_ref[h], F32(-1e30))
        sk = sink_ref[h]
        m = jnp.maximum(jnp.max(s, axis=-1, keepdims=True), sk)
        p = jnp.exp(s - m)
        denom = jnp.sum(p, axis=-1, keepdims=True) + jnp.exp(sk - m)
        o = jnp.dot(p.astype(BF16), vh, preferred_element_type=F32) / denom
        outs.append(o)
    o_ref[0] = jnp.concatenate(outs, axis=1).astype(o_ref.dtype)


def _wattn(h3, sink, bias):
    B, L, _ = h3.shape
    nb = L // BLOCK
    kv_spec = lambda col, shift: pl.BlockSpec(
        (1, BLOCK, KV_W), lambda b, n: (b, jnp.clip(n + shift, 0, nb - 1), col))
    return pl.pallas_call(
        functools.partial(_wattn_kernel, seq=L),
        grid=(B, nb),
        in_specs=[
            pl.BlockSpec(memory_space=pltpu.SMEM),
            pl.BlockSpec((1, BLOCK, Q_W), lambda b, n: (b, n, H_Q)),
            kv_spec(H_K, -1), kv_spec(H_K, 0), kv_spec(H_K, 1),
            kv_spec(H_V, -1), kv_spec(H_V, 0), kv_spec(H_V, 1),
            pl.BlockSpec((N_Q_HEADS, BLOCK, 3 * BLOCK), lambda b, n: (0, 0, 0)),
        ],
        out_specs=pl.BlockSpec((1, BLOCK, Q_W), lambda b, n: (b, n, 0)),
        out_shape=jax.ShapeDtypeStruct((B, L, Q_W), BF16),
        compiler_params=_cparams(("parallel", "parallel")),
        name="wattn",
    )(sink, h3, h3, h3, h3, h3, h3, h3, bias)


def _t5_bucket(rel):
    nb = N_BUCKETS // 2
    max_exact = nb // 2
    ret = jnp.where(rel > 0, nb, 0)
    n = jnp.abs(rel)
    nf = jnp.maximum(n, 1).astype(F32)
    large = max_exact + (jnp.log(nf / max_exact) / math.log(MAX_DISTANCE / max_exact)
                         * (nb - max_exact)).astype(jnp.int32)
    large = jnp.minimum(large, nb - 1)
    return ret + jnp.where(n < max_exact, n, large)


def _bias_table(rel_bias):
    qi = jnp.arange(BLOCK, dtype=jnp.int32)[:, None]
    kj = jnp.arange(3 * BLOCK, dtype=jnp.int32)[None, :]
    bias = rel_bias[_t5_bucket(kj - BLOCK - qi)].astype(F32)
    return jnp.transpose(bias, (2, 0, 1))


def _memattn_kernel(q_ref, mem_ref, w_ref, o_ref, kv_ref):
    @pl.when(pl.program_id(1) == 0)
    def _():
        kv_ref[...] = jnp.dot(mem_ref[0].astype(BF16), w_ref[...],
                              preferred_element_type=F32).astype(kv_ref.dtype)

    q = q_ref[0]
    outs = []
    for h in range(N_MEM_HEADS):
        lo = h * MEM_HEAD_DIM
        qh = q[:, lo:lo + MEM_HEAD_DIM]
        kh = kv_ref[:, lo:lo + MEM_HEAD_DIM]
        vh = kv_ref[:, BRANCH_W + lo:BRANCH_W + lo + MEM_HEAD_DIM]
        s = lax.dot_general(qh, kh, (((1,), (1,)), ((), ())), preferred_element_type=F32) * (MEM_HEAD_DIM ** -0.5)
        m = jnp.max(s, axis=-1, keepdims=True)
        p = jnp.exp(s - m)
        denom = jnp.sum(p, axis=-1, keepdims=True)
        outs.append(jnp.dot(p.astype(BF16), vh, preferred_element_type=F32) / denom)
    o_ref[0] = jnp.concatenate(outs, axis=1).astype(o_ref.dtype)


def _memattn(h3, mem, w_kv, tm=512):
    B, L, _ = h3.shape
    M = mem.shape[1]
    tm = min(tm, L)
    return pl.pallas_call(
        _memattn_kernel,
        grid=(B, L // tm),
        in_specs=[
            pl.BlockSpec((1, tm, BRANCH_W), lambda b, l: (b, l, H_QM)),
            pl.BlockSpec((1, M, D_MODEL), lambda b, l: (b, 0, 0)),
            pl.BlockSpec((D_MODEL, 2 * BRANCH_W), lambda b, l: (0, 0)),
        ],
        out_specs=pl.BlockSpec((1, tm, BRANCH_W), lambda b, l: (b, l, 0)),
        out_shape=jax.ShapeDtypeStruct((B, L, BRANCH_W), BF16),
        scratch_shapes=[pltpu.VMEM((M, 2 * BRANCH_W), BF16)],
        compiler_params=_cparams(("parallel", "arbitrary")),
        name="memattn",
    )(h3, mem, w_kv)


HALO = 16


def _sconv_kernel(cur_ref, prev_ref, next_ref, w_ref, b_ref, o_ref, *, nt):
    t = pl.program_id(2)
    u = cur_ref[0].astype(F32)
    tl = u.shape[0]
    pr = prev_ref[0].astype(F32)[HALO - 1:HALO]
    nx = next_ref[0].astype(F32)[0:1]
    pr = jnp.where(t == 0, 0.0, pr)
    nx = jnp.where(t == nt - 1, 0.0, nx)
    row = lax.broadcasted_iota(jnp.int32, u.shape, 0)
    up = jnp.where(row == 0, pr, pltpu.roll(u, 1, 0))
    un = jnp.where(row == tl - 1, nx, pltpu.roll(u, tl - 1, 0))
    w = w_ref[...]
    o_ref[0] = (b_ref[...] + up * w[0:1] + u * w[1:2] + un * w[2:3]).astype(o_ref.dtype)


def _sconv(h3, conv_w, conv_b, seg, tl=1024, cb=256):
    B, L, _ = h3.shape
    tl = min(tl, L)
    nt = L // tl
    per = BRANCH_W // cb
    col = seg * per
    return pl.pallas_call(
        functools.partial(_sconv_kernel, nt=nt),
        grid=(B, per, nt),
        in_specs=[
            pl.BlockSpec((1, tl, cb), lambda b, j, t: (b, t, H_HY + col + j)),
            pl.BlockSpec((1, HALO, cb), lambda b, j, t: (b, jnp.maximum(t * (tl // HALO) - 1, 0), H_HY + col + j)),
            pl.BlockSpec((1, HALO, cb),
                         lambda b, j, t: (b, jnp.minimum((t + 1) * (tl // HALO), L // HALO - 1), H_HY + col + j)),
            pl.BlockSpec((SHORT_CONV, cb), lambda b, j, t: (0, col + j)),
            pl.BlockSpec((1, cb), lambda b, j, t: (0, col + j)),
        ],
        out_specs=pl.BlockSpec((1, tl, cb), lambda b, j, t: (b, t, j)),
        out_shape=jax.ShapeDtypeStruct((B, L, BRANCH_W), BF16),
        compiler_params=_cparams(("parallel", "parallel", "parallel")),
        name="sconv",
    )(h3, h3, h3, conv_w, conv_b)


def _filt_kernel(w1_ref, b1_ref, w2_ref, b2_ref, w3_ref, b3_ref, fr_ref, fq_ref, dl_ref, k_ref, s_ref, *, seq, tr):
    i = pl.program_id(0)
    m = i * tr + lax.broadcasted_iota(jnp.int32, (tr, LANE), 0)
    lane = lax.broadcasted_iota(jnp.int32, (tr, LANE), 1)
    p = jnp.where(m < seq, m, 2 * seq - m).astype(F32)
    t = p / F32(seq - 1)
    w = F32(2.0 * math.pi) * p / F32(seq)
    arg = fq_ref[...] * w
    bands = (POS_EMB_DIM - 1) // 2
    z = jnp.where(lane == 0, t,
                  jnp.where(lane <= bands, jnp.cos(arg),
                            jnp.where(lane <= 2 * bands, -jnp.sin(arg), 0.0)))
    fr = fr_ref[...]
    h = jnp.sin(fr[0:1] * (jnp.dot(z, w1_ref[...], precision=HI, preferred_element_type=F32) + b1_ref[...]))
    h = jnp.sin(fr[1:2] * (jnp.dot(h, w2_ref[...], precision=HI, preferred_element_type=F32) + b2_ref[...]))
    h = jnp.dot(h, w3_ref[...], precision=HI, preferred_element_type=F32) + b3_ref[...]
    mc = m[:, 0:1]
    tc = t[:, 0:1]
    decay = jnp.exp(-tc * dl_ref[...])
    oc = HYENA_ORDER * BRANCH_W

    @pl.when(i == 0)
    def _():
        s_ref[...] = jnp.zeros_like(s_ref)

    for o in range(HYENA_ORDER):
        fwd = h[:, o * BRANCH_W:(o + 1) * BRANCH_W]
        bwd = h[:, oc + o * BRANCH_W:oc + (o + 1) * BRANCH_W]
        k = jnp.where(mc < seq, fwd, bwd) * decay
        k = jnp.where(mc == seq, 0.0, k)
        k_ref[o] = k
        s_ref[o] += jnp.sum(jnp.abs(k), axis=0, keepdims=True)


def _filters(seq, w1, b1, w2, b2, w3, b3, freq, tr=256):
    n = 2 * seq
    tr = min(tr, n)
    bands = (POS_EMB_DIM - 1) // 2
    hid = LANE
    w1p = jnp.zeros((LANE, hid), F32).at[:POS_EMB_DIM, :FILTER_HIDDEN].set(w1.astype(F32))
    b1p = jnp.zeros((1, hid), F32).at[0, :FILTER_HIDDEN].set(b1.astype(F32))
    w2p = jnp.zeros((hid, hid), F32).at[:FILTER_HIDDEN, :FILTER_HIDDEN].set(w2.astype(F32))
    b2p = jnp.zeros((1, hid), F32).at[0, :FILTER_HIDDEN].set(b2.astype(F32))
    w3p = jnp.zeros((hid, w3.shape[1]), F32).at[:FILTER_HIDDEN].set(w3.astype(F32))
    b3p = b3.astype(F32)[None]
    frp = jnp.zeros((2, hid), F32).at[:, :FILTER_HIDDEN].set(freq.astype(F32))
    f = jnp.linspace(1e-4, bands - 1, bands, dtype=F32)
    fq = jnp.zeros((1, LANE), F32).at[0, 1:1 + bands].set(f).at[0, 1 + bands:1 + 2 * bands].set(f)
    deltas = jnp.abs(jnp.linspace(math.log(DECAY_TARGET) / FAST_DECAY_PCT,
                                  math.log(DECAY_TARGET) / SLOW_DECAY_PCT, BRANCH_W, dtype=F32))[None]
    full = lambda a: pl.BlockSpec(a.shape, lambda i: (0,) * a.ndim)
    args = (w1p, b1p, w2p, b2p, w3p, b3p, frp, fq, deltas)
    return pl.pallas_call(
        functools.partial(_filt_kernel, seq=seq, tr=tr),
        grid=(n // tr,),
        in_specs=[full(a) for a in args],
        out_specs=[pl.BlockSpec((HYENA_ORDER, tr, BRANCH_W), lambda i: (0, i, 0)),
                   pl.BlockSpec((HYENA_ORDER, 1, BRANCH_W), lambda i: (0, 0, 0))],
        out_shape=[jax.ShapeDtypeStruct((HYENA_ORDER, n, BRANCH_W), F32),
                   jax.ShapeDtypeStruct((HYENA_ORDER, 1, BRANCH_W), F32)],
        compiler_params=_cparams(("arbitrary",)),
        name="hyena_filter",
    )(*args)


def _dft_tables(seq):
    n1 = 2 * seq // FFT_N2
    n1h = seq // FFT_N2
    n = 2 * seq
    k1 = jnp.arange(n1, dtype=jnp.int32)
    th = (2.0 * math.pi / n1) * ((k1[:, None] * k1[None, :]) % n1).astype(F32)
    c1, s1 = jnp.cos(th), jnp.sin(th)
    ch, sh = c1[:, :n1h], s1[:, :n1h]
    f1 = jnp.concatenate([jnp.concatenate([ch, sh], 1), jnp.concatenate([-sh, ch], 1)], 0)
    f1_real = jnp.concatenate([c1, -s1], 0)
    cht, sht = ch.T / n, sh.T / n
    g1 = jnp.concatenate([jnp.concatenate([cht, -sht], 1), jnp.concatenate([sht, cht], 1)], 0)
    n2 = jnp.arange(FFT_N2, dtype=jnp.int32)
    kk = k1[:, None, None] + n1 * n2[None, :, None]
    ph = (2.0 * math.pi / n) * ((kk * n2[None, None, :]) % n).astype(F32)
    c2, s2 = jnp.cos(ph), jnp.sin(ph)
    m2 = jnp.concatenate([jnp.concatenate([c2, s2], 2), jnp.concatenate([-s2, c2], 2)], 1)
    return f1, f1_real, g1, m2, jnp.swapaxes(m2, 1, 2)


def _s1_kernel(z_ref, f_ref, o_ref, *, hi):
    nb, r, w = z_ref.shape
    z = z_ref[...].reshape(nb * r, w) if nb > 1 else z_ref[0]
    if hi:
        res = jnp.dot(f_ref[...], z, precision=HI, preferred_element_type=F32)
    else:
        res = jnp.dot(f_ref[...], z.astype(BF16), preferred_element_type=F32)
    half = res.shape[0] // 2
    o_ref[0, 0] = res[:half].astype(o_ref.dtype)
    o_ref[0, 1] = res[half:].astype(o_ref.dtype)


def _fft_s1(z, f, *, pair, hi=False, n2t=8):
    bx, lz, c = z.shape
    r = lz // FFT_N2
    nb = 2 if pair else 1
    n1 = f.shape[0] // 2
    w = n2t * c
    zz = z.reshape(bx, r, FFT_N2 * c)
    out_dtype = F32 if hi else BF16
    return pl.pallas_call(
        functools.partial(_s1_kernel, hi=hi),
        grid=(bx // nb, FFT_N2 // n2t),
        in_specs=[pl.BlockSpec((nb, r, w), lambda p, j: (p, 0, j)),
                  pl.BlockSpec(f.shape, lambda p, j: (0, 0))],
        out_specs=pl.BlockSpec((1, 2, n1, w), lambda p, j: (p, 0, 0, j)),
        out_shape=jax.ShapeDtypeStruct((bx // nb, 2, n1, FFT_N2 * c), out_dtype),
        compiler_params=_cparams(("parallel", "parallel")),
        name="fft_stage1",
    )(zz, f.astype(F32 if hi else BF16))


def _mid_kernel(a_ref, m_ref, mt_ref, kf_ref, o_ref):
    k1t = a_ref.shape[2]
    h = FFT_N2
    for j in range(k1t):
        a = jnp.concatenate([a_ref[0, 0, j], a_ref[0, 1, j]], axis=0)
        x = jnp.dot(m_ref[j], a, preferred_element_type=F32)
        xr, xi = x[:h], x[h:]
        kr, ki = kf_ref[0, 0, j], kf_ref[0, 1, j]
        y = jnp.concatenate([xr * kr - xi * ki, xr * ki + xi * kr], axis=0).astype(BF16)
        b = jnp.dot(mt_ref[j], y, preferred_element_type=F32)
        o_ref[0, 0, j] = b[:h].astype(o_ref.dtype)
        o_ref[0, 1, j] = b[h:].astype(o_ref.dtype)


def _mid_fwd_kernel(a_ref, m_ref, o_ref):
    k1t = a_ref.shape[2]
    h = FFT_N2
    for j in range(k1t):
        a = jnp.concatenate([a_ref[0, 0, j], a_ref[0, 1, j]], axis=0)
        x = jnp.dot(m_ref[j], a, precision=HI, preferred_element_type=F32)
        o_ref[0, 0, j] = x[:h]
        o_ref[0, 1, j] = x[h:]


def _fft_mid(a, m2, m2t, kf, order, k1t=8):
    p, _, n1, wc = a.shape
    c = wc // FFT_N2
    k1t = min(k1t, n1)
    a5 = a.reshape(p, 2, n1, FFT_N2, c)
    blk = (1, 2, k1t, FFT_N2, c)
    out = pl.pallas_call(
        _mid_kernel,
        grid=(n1 // k1t, p),
        in_specs=[pl.BlockSpec(blk, lambda k, q: (q, 0, k, 0, 0)),
                  pl.BlockSpec((k1t, 2 * FFT_N2, 2 * FFT_N2), lambda k, q: (k, 0, 0)),
                  pl.BlockSpec((k1t, 2 * FFT_N2, 2 * FFT_N2), lambda k, q: (k, 0, 0)),
                  pl.BlockSpec(blk, lambda k, q: (order, 0, k, 0, 0))],
        out_specs=pl.BlockSpec(blk, lambda k, q: (q, 0, k, 0, 0)),
        out_shape=jax.ShapeDtypeStruct(a5.shape, BF16),
        compiler_params=_cparams(("parallel", "parallel")),
        name="fft_mid",
    )(a5, m2.astype(BF16), m2t.astype(BF16), kf)
    return out.reshape(a.shape)


def _fft_mid_fwd(a, m2, k1t=8):
    p, _, n1, wc = a.shape
    c = wc // FFT_N2
    k1t = min(k1t, n1)
    a5 = a.reshape(p, 2, n1, FFT_N2, c)
    blk = (1, 2, k1t, FFT_N2, c)
    return pl.pallas_call(
        _mid_fwd_kernel,
        grid=(n1 // k1t, p),
        in_specs=[pl.BlockSpec(blk, lambda k, q: (q, 0, k, 0, 0)),
                  pl.BlockSpec((k1t, 2 * FFT_N2, 2 * FFT_N2), lambda k, q: (k, 0, 0))],
        out_specs=pl.BlockSpec(blk, lambda k, q: (q, 0, k, 0, 0)),
        out_shape=jax.ShapeDtypeStruct(a5.shape, F32),
        compiler_params=_cparams(("parallel", "parallel")),
        name="fft_mid_filter",
    )(a5, m2)


def _s3_kernel(b_ref, g_ref, v_ref, x_ref, sc_ref, d_ref, o_ref):
    bm = jnp.concatenate([b_ref[0, 0], b_ref[0, 1]], axis=0)
    y = jnp.dot(g_ref[...], bm, preferred_element_type=F32)
    r = v_ref.shape[1]
    for part in range(2):
        yp = y[part * r:(part + 1) * r]
        v = v_ref[part].astype(F32)
        x = x_ref[part].astype(F32)
        o_ref[part] = (x * (yp * sc_ref[...] + v * d_ref[...])).astype(o_ref.dtype)


def _fft_s3(bm, g1, v, x, inv_norm, skip, n2t=8):
    p, _, n1, wc = bm.shape
    bx, seq, c = v.shape
    r = seq // FFT_N2
    w = n2t * c
    vv = v.reshape(bx, r, FFT_N2 * c)
    xx = x.reshape(bx, r, FFT_N2 * c)
    sc = jnp.tile(inv_norm.astype(F32).reshape(1, c), (1, n2t))
    dd = jnp.tile(skip.astype(F32).reshape(1, c), (1, n2t))
    io = pl.BlockSpec((2, r, w), lambda q, j: (q, 0, j))
    out = pl.pallas_call(
        _s3_kernel,
        grid=(p, FFT_N2 // n2t),
        in_specs=[pl.BlockSpec((1, 2, n1, w), lambda q, j: (q, 0, 0, j)),
                  pl.BlockSpec(g1.shape, lambda q, j: (0, 0)),
                  io, io,
                  pl.BlockSpec((1, w), lambda q, j: (0, 0)),
                  pl.BlockSpec((1, w), lambda q, j: (0, 0))],
        out_specs=io,
        out_shape=jax.ShapeDtypeStruct(vv.shape, BF16),
        compiler_params=_cparams(("parallel", "parallel")),
        name="fft_stage3",
    )(bm, g1.astype(BF16), vv, xx, sc, dd)
    return out.reshape(bx, seq, c)


def _hyena(h3, conv_w, conv_b, filt, skip):
    seq = h3.shape[1]
    f1, f1_real, g1, m2, m2t = _dft_tables(seq)
    taps, asum = _filters(seq, *filt)
    kf = _fft_mid_fwd(_fft_s1(taps, f1_real, pair=False, hi=True), m2)
    inv_norm = 1.0 / asum[:, 0]
    uc = [_sconv(h3, conv_w, conv_b, seg) for seg in range(3)]
    z = uc[0]
    for o in range(HYENA_ORDER):
        spec = _fft_s1(z, f1, pair=True)
        spec = _fft_mid(spec, m2, m2t, kf, o)
        z = _fft_s3(spec, g1, z, uc[1 + o], inv_norm[o], skip[o])
    return z


def _mix_kernel(x_ref, a_ref, hb_ref, c_ref, g0_ref, g1_ref, g2_ref, wb_ref, wo_ref,
                lig_ref, lib_ref, l1g_ref, l1b_ref, wr_ref, br_ref, x1_ref, x1b_ref, lg_ref):
    acc = None
    for n, (br, gr) in enumerate(((a_ref, g0_ref), (hb_ref, g1_ref), (c_ref, g2_ref))):
        proj = jnp.dot(br[...], wb_ref[n], preferred_element_type=F32)
        term = jax.nn.sigmoid(gr[...].astype(F32)) * proj
        acc = term if acc is None else acc + term
    mix = jnp.dot(acc.astype(BF16), wo_ref[...], preferred_element_type=F32)
    xn = _ln(x_ref[...], lig_ref[...], lib_ref[...])
    x1 = _ln(DN_ALPHA * xn + mix, l1g_ref[...], l1b_ref[...])
    x1_ref[...] = x1
    x1b_ref[...] = x1.astype(BF16)
    lg_ref[...] = jnp.dot(x1, wr_ref[...], precision=HI, preferred_element_type=F32) + br_ref[...]


def _mix(x2, a, hb, c, h2, wb, wo, lig, lib, l1g, l1b, wr, br, tm=512):
    T, D = x2.shape
    row = lambda wdt, col: pl.BlockSpec((tm, wdt), lambda i: (i, col))
    const = lambda arr: pl.BlockSpec(arr.shape, lambda i: (0,) * arr.ndim)
    return pl.pallas_call(
        _mix_kernel,
        grid=(T // tm,),
        in_specs=[row(D, 0), row(BRANCH_W, 0), row(BRANCH_W, 0), row(BRANCH_W, 0),
                  row(D, H_G), row(D, H_G + 1), row(D, H_G + 2),
                  const(wb), const(wo), const(lig), const(lib), const(l1g), const(l1b), const(wr), const(br)],
        out_specs=[row(D, 0), row(D, 0), row(LANE, 0)],
        out_shape=[jax.ShapeDtypeStruct((T, D), F32), jax.ShapeDtypeStruct((T, D), BF16),
                   jax.ShapeDtypeStruct((T, LANE), F32)],
        compiler_params=_cparams(("parallel",)),
        name="mix",
    )(x2, a, hb, c, h2, h2, h2, wb, wo, lig, lib, l1g, l1b, wr, br)


def _ffn_kernel(ce_ref, x_ref, gt_ref, wgu_ref, bgu_ref, wd_ref, bd_ref, o_ref):
    del ce_ref
    gu = jnp.dot(x_ref[...], wgu_ref[0], preferred_element_type=F32) + bgu_ref[0]
    f = gu.shape[1] // 2
    g = jnp.minimum(gu[:, :f], SWIGLU_LIMIT)
    u = jnp.clip(gu[:, f:], -SWIGLU_LIMIT, SWIGLU_LIMIT)
    hmid = (u + 1.0) * (g * jax.nn.sigmoid(g * SWIGLU_ALPHA))
    y = jnp.dot(hmid.astype(BF16), wd_ref[0], preferred_element_type=F32) + bd_ref[0]
    o_ref[...] = (y * gt_ref[...]).astype(o_ref.dtype)


def _ffn(chunk_e, x_slots, slot_gate, wgu, bgu, wd, bd):
    P, D = x_slots.shape
    F = wd.shape[1]
    n_chunks = P // MOE_TILE
    wspec = lambda a, b: pl.BlockSpec((1, a, b), lambda i, ce: (ce[i], 0, 0))
    grid_spec = pltpu.PrefetchScalarGridSpec(
        num_scalar_prefetch=1,
        grid=(n_chunks,),
        in_specs=[pl.BlockSpec((MOE_TILE, D), lambda i, ce: (i, 0)),
                  pl.BlockSpec((MOE_TILE, 1), lambda i, ce: (i, 0)),
                  wspec(D, 2 * F), wspec(1, 2 * F), wspec(F, D), wspec(1, D)],
        out_specs=pl.BlockSpec((MOE_TILE, D), lambda i, ce: (i, 0)),
    )
    return pl.pallas_call(
        _ffn_kernel,
        grid_spec=grid_spec,
        out_shape=jax.ShapeDtypeStruct((P, D), BF16),
        compiler_params=_cparams(("arbitrary",)),
        name="moe_ffn",
    )(chunk_e, x_slots, slot_gate, wgu, bgu, wd, bd)


DEINT_W = 256


def _deint_kernel(w_ref, s_ref, o_ref):
    w = w_ref[0].astype(BF16)
    f = w.shape[1] // 2
    half = DEINT_W // 2
    for blk in range(w.shape[1] // DEINT_W):
        r = jnp.dot(w[:, blk * DEINT_W:(blk + 1) * DEINT_W], s_ref[...], preferred_element_type=F32)
        o_ref[0, :, blk * half:(blk + 1) * half] = r[:, :half].astype(o_ref.dtype)
        o_ref[0, :, f + blk * half:f + (blk + 1) * half] = r[:, half:].astype(o_ref.dtype)


def _deinterleave(w, tr=512):
    E, D, F2 = w.shape
    i = jnp.arange(DEINT_W, dtype=jnp.int32)
    src = jnp.where(i < DEINT_W // 2, 2 * i, 2 * (i - DEINT_W // 2) + 1)
    sel = (i[:, None] == src[None, :]).astype(BF16)
    return pl.pallas_call(
        _deint_kernel,
        grid=(E, D // tr),
        in_specs=[pl.BlockSpec((1, tr, F2), lambda e, r: (e, r, 0)),
                  pl.BlockSpec((DEINT_W, DEINT_W), lambda e, r: (0, 0))],
        out_specs=pl.BlockSpec((1, tr, F2), lambda e, r: (e, r, 0)),
        out_shape=jax.ShapeDtypeStruct((E, D, F2), BF16),
        compiler_params=_cparams(("parallel", "parallel")),
        name="deinterleave_gate_up",
    )(w, sel)


def _final_kernel(x1_ref, y_ref, g_ref, b_ref, o_ref):
    moe = y_ref[0].astype(F32)
    for k in range(1, TOP_K):
        moe = moe + y_ref[k].astype(F32)
    o_ref[...] = _ln(DN_ALPHA * x1_ref[...] + moe, g_ref[...], b_ref[...])


def _final(x1, yg, g, b, tm=512):
    T, D = x1.shape
    return pl.pallas_call(
        _final_kernel,
        grid=(T // tm,),
        in_specs=[pl.BlockSpec((tm, D), lambda i: (i, 0)),
                  pl.BlockSpec((TOP_K, tm, D), lambda i: (0, i, 0)),
                  pl.BlockSpec((1, D), lambda i: (0, 0)),
                  pl.BlockSpec((1, D), lambda i: (0, 0))],
        out_specs=pl.BlockSpec((tm, D), lambda i: (i, 0)),
        out_shape=jax.ShapeDtypeStruct((T, D), F32),
        compiler_params=_cparams(("parallel",)),
        name="final_ln",
    )(x1, yg, g, b)


def _moe(x1, x1b, logits, ffn_w, ln2_g, ln2_b):
    T, D = x1.shape
    top_v, top_i = lax.top_k(logits[:, :N_EXPERTS], TOP_K)
    gate = jax.nn.softmax(top_v, axis=-1)
    A = T * TOP_K
    flat_e = top_i.reshape(A).astype(jnp.int32)
    idx = jnp.arange(A, dtype=jnp.int32)
    se, order = lax.sort((flat_e, idx), num_keys=2)
    experts = jnp.arange(N_EXPERTS, dtype=jnp.int32)
    counts = jnp.sum((flat_e[:, None] == experts[None, :]).astype(jnp.int32), axis=0)
    starts = jnp.cumsum(counts) - counts
    pcounts = (counts + MOE_TILE - 1) // MOE_TILE * MOE_TILE
    pends = jnp.cumsum(pcounts)
    pstarts = pends - pcounts
    shift = pstarts - starts
    dest = idx + jnp.sum(jnp.where(se[:, None] == experts[None, :], shift[None, :], 0), axis=1)
    _, pos = lax.sort((order, dest), num_keys=1)
    n_chunks = -(-A // MOE_TILE) + N_EXPERTS
    P = n_chunks * MOE_TILE
    chunk_lo = jnp.arange(n_chunks, dtype=jnp.int32) * MOE_TILE
    chunk_e = jnp.minimum(jnp.sum((chunk_lo[:, None] >= pends[None, :]).astype(jnp.int32), axis=1), N_EXPERTS - 1)
    per_slot = lambda v: jnp.repeat(v[chunk_e], MOE_TILE)
    off = jnp.arange(P, dtype=jnp.int32) - per_slot(pstarts)
    valid = off < per_slot(counts)
    src = order[jnp.clip(per_slot(starts) + off, 0, A - 1)]
    slot_tok = jnp.where(valid, src // TOP_K, 0)
    slot_gate = jnp.where(valid, gate.reshape(A)[src], 0.0)
    x_slots = x1b[slot_tok]
    y_slots = _ffn(chunk_e, x_slots, slot_gate[:, None], *ffn_w)
    yg = y_slots[pos.reshape(T, TOP_K).T.reshape(A)].reshape(TOP_K, T, D)
    return _final(x1, yg, ln2_g, ln2_b)


def _trunk(x, mem, p):
    B, L, D = x.shape
    x2 = x.reshape(B * L, D)
    h2 = _inproj(x2, p["ln_in_g"], p["ln_in_b"], p["w_in"])
    h3 = h2.reshape(B, L, IN_W)
    a = _wattn(h3, p["sink"], p["bias"])
    hb = _hyena(h3, p["conv_w"], p["conv_b"], p["filt"], p["skip"])
    c = _memattn(h3, mem, p["w_mem_kv"])
    x1, x1b, logits = _mix(x2, a.reshape(B * L, -1), hb.reshape(B * L, -1), c.reshape(B * L, -1), h2,
                           p["w_branch"], p["w_out"], p["ln_in_g"], p["ln_in_b"], p["ln1_g"], p["ln1_b"],
                           p["w_router"], p["b_router"])
    out = _moe(x1, x1b, logits, p["ffn"], p["ln2_g"], p["ln2_b"])
    return out.reshape(B, L, D)


def _prep(ln_in_g, ln_in_b, rel_bias, w_in, attn_sink, conv_w, conv_b, filt_w1, filt_b1, filt_w2, filt_b2,
          filt_w3, filt_b3, filt_freq, hyena_skip, w_mem_kv, w_branch, w_out, ln1_g, ln1_b, w_router, b_router,
          w_gate_up, b_gate_up, w_down, b_down, ln2_g, ln2_b):
    w = w_in
    g_lo = Q_W + 2 * KV_W + 3 * BRANCH_W + BRANCH_W
    hy_lo = Q_W + 2 * KV_W
    w_p = jnp.concatenate([w[:, g_lo:], w[:, :Q_W], w[:, hy_lo + 3 * BRANCH_W:g_lo],
                           w[:, hy_lo:hy_lo + 3 * BRANCH_W], w[:, Q_W:hy_lo]], axis=1).astype(BF16)
    row = lambda v: v.astype(F32)[None]
    wr = jnp.zeros((D_MODEL, LANE), F32).at[:, :N_EXPERTS].set(w_router.astype(F32))
    br = jnp.full((1, LANE), -1e30, F32).at[0, :N_EXPERTS].set(b_router.astype(F32))
    return dict(
        ln_in_g=row(ln_in_g), ln_in_b=row(ln_in_b), w_in=w_p,
        sink=attn_sink.astype(F32), bias=_bias_table(rel_bias),
        conv_w=conv_w.astype(F32), conv_b=row(conv_b),
        filt=(filt_w1, filt_b1, filt_w2, filt_b2, filt_w3, filt_b3, filt_freq), skip=hyena_skip,
        w_mem_kv=w_mem_kv.astype(BF16), w_branch=w_branch.astype(BF16), w_out=w_out.astype(BF16),
        ln1_g=row(ln1_g), ln1_b=row(ln1_b), w_router=wr, b_router=br,
        ffn=(_deinterleave(w_gate_up),
             jnp.concatenate([b_gate_up[:, 0::2], b_gate_up[:, 1::2]], axis=1)[:, None, :].astype(F32),
             w_down.astype(BF16), b_down[:, None, :].astype(F32)),
        ln2_g=row(ln2_g), ln2_b=row(ln2_b),
    )


def kernel(x_prompt, x_sample, mem_prompt, mem_sample, ln_in_g, ln_in_b, rel_bias, w_in, attn_sink, conv_w, conv_b, filt_w1, filt_b1, filt_w2, filt_b2, filt_w3, filt_b3, filt_freq, hyena_skip, w_mem_kv, w_branch, w_out, ln1_g, ln1_b, w_router, b_router, w_gate_up, b_gate_up, w_down, b_down, ln2_g, ln2_b):
    p = _prep(ln_in_g, ln_in_b, rel_bias, w_in[0], attn_sink[0], conv_w[0], conv_b[0], filt_w1[0], filt_b1[0],
              filt_w2[0], filt_b2[0], filt_w3[0], filt_b3[0], filt_freq[0], hyena_skip[0], w_mem_kv[0],
              w_branch[0], w_out[0], ln1_g[0], ln1_b[0], w_router[0], b_router[0], w_gate_up[0], b_gate_up[0],
              w_down[0], b_down[0], ln2_g[0], ln2_b[0])
    return (_trunk(x_prompt, mem_prompt, p), _trunk(x_sample, mem_sample, p))
```

```python
import functools
import math

import jax
import jax.numpy as jnp
from jax import lax
from jax.experimental import pallas as pl
from jax.experimental.pallas import tpu as pltpu

F32 = jnp.float32
BF16 = jnp.bfloat16
HI = lax.Precision.HIGHEST

D_MODEL = 1024
BRANCH_W = 512
N_Q_HEADS = 8
N_KV_HEADS = 2
HEAD_DIM = 64
WINDOW = 128
BLOCK = 128
N_BUCKETS = 32
MAX_DISTANCE = 128
HYENA_ORDER = 2
SHORT_CONV = 3
POS_EMB_DIM = 33
FILTER_HIDDEN = 64
DECAY_TARGET = 1e-2
FAST_DECAY_PCT = 0.3
SLOW_DECAY_PCT = 1.5
N_MEM_HEADS = 4
MEM_HEAD_DIM = BRANCH_W // N_MEM_HEADS
N_BRANCHES = 3
N_EXPERTS = 32
TOP_K = 4
D_EXPERT = 1024
SWIGLU_LIMIT = 7.0
SWIGLU_ALPHA = 1.702
LN_EPS = 1e-5
DEPTH = 1
DN_ALPHA = (2 * DEPTH) ** 0.25

Q_W = N_Q_HEADS * HEAD_DIM
KV_W = N_KV_HEADS * HEAD_DIM
IN_W = Q_W + 2 * KV_W + 3 * BRANCH_W + BRANCH_W + N_BRANCHES * D_MODEL

H_G = 0
H_Q = 6
H_QM = 7
H_HY = 16
H_K = 44
H_V = 45

FFT_N2 = 128
MOE_TILE = 512
LANE = 128
VMEM_LIMIT = 52 * 1024 * 1024


def _cparams(sem):
    return pltpu.CompilerParams(dimension_semantics=sem, vmem_limit_bytes=VMEM_LIMIT)


def _ln(x, g, b):
    mu = jnp.mean(x, axis=-1, keepdims=True)
    xc = x - mu
    var = jnp.mean(xc * xc, axis=-1, keepdims=True)
    return xc * lax.rsqrt(var + LN_EPS) * g + b


def _inproj_kernel(x_ref, g_ref, b_ref, w_ref, o_ref):
    xn = _ln(x_ref[...], g_ref[...], b_ref[...])
    o_ref[...] = jnp.dot(xn.astype(BF16), w_ref[...], preferred_element_type=F32).astype(o_ref.dtype)


def _inproj(x2, ln_g, ln_b, w_p, tm=512):
    T, D = x2.shape
    N = w_p.shape[1]
    tn = N // 2
    return pl.pallas_call(
        _inproj_kernel,
        grid=(N // tn, T // tm),
        in_specs=[
            pl.BlockSpec((tm, D), lambda j, i: (i, 0)),
            pl.BlockSpec((1, D), lambda j, i: (0, 0)),
            pl.BlockSpec((1, D), lambda j, i: (0, 0)),
            pl.BlockSpec((D, tn), lambda j, i: (0, j)),
        ],
        out_specs=pl.BlockSpec((tm, tn), lambda j, i: (i, j)),
        out_shape=jax.ShapeDtypeStruct((T, N), BF16),
        compiler_params=_cparams(("parallel", "parallel")),
        name="inproj",
    )(x2, ln_g, ln_b, w_p)


def _wattn_kernel(sink_ref, q_ref, kp_ref, kc_ref, kn_ref, vp_ref, vc_ref, vn_ref, bias_ref, o_ref, *, seq):
    n = pl.program_id(1)
    q = q_ref[0]
    kcat = jnp.concatenate([kp_ref[0], kc_ref[0], kn_ref[0]], axis=0)
    vcat = jnp.concatenate([vp_ref[0], vc_ref[0], vn_ref[0]], axis=0)
    qi = lax.broadcasted_iota(jnp.int32, (BLOCK, 3 * BLOCK), 0)
    kj = lax.broadcasted_iota(jnp.int32, (BLOCK, 3 * BLOCK), 1)
    rel = kj - BLOCK - qi
    kpos = (n - 1) * BLOCK + kj
    valid = (jnp.abs(rel) <= WINDOW) & (kpos >= 0) & (kpos < seq)
    group = N_Q_HEADS // N_KV_HEADS
    scores = []
    for h in range(N_Q_HEADS):
        g = h // group
        qh = q[:, h * HEAD_DIM:(h + 1) * HEAD_DIM]
        kh = kcat[:, g * HEAD_DIM:(g + 1) * HEAD_DIM]
        scores.append(lax.dot_general(qh, kh, (((1,), (1,)), ((), ())), preferred_element_type=F32))
    probs, denoms = [], []
    for h in range(N_Q_HEADS):
        s = jnp.where(valid, scores[h] + bias_ref[h], F32(-1e30))
        sk = sink_ref[h]
        m = jnp.maximum(jnp.max(s, axis=-1, keepdims=True), sk)
        p = jnp.exp(s - m)
        denoms.append(jnp.sum(p, axis=-1, keepdims=True) + jnp.exp(sk - m))
        probs.append(p.astype(BF16))
    outs = []
    for h in range(N_Q_HEADS):
        g = h // group
        vh = vcat[:, g * HEAD_DIM:(g + 1) * HEAD_DIM]
        outs.append(jnp.dot(probs[h], vh, preferred_element_type=F32) / denoms[h])
    o_ref[0] = jnp.concatenate(outs, axis=1).astype(o_ref.dtype)


def _wattn(h3, sink, bias):
    B, L, _ = h3.shape
    nb = L // BLOCK
    kv_spec = lambda col, shift: pl.BlockSpec(
        (1, BLOCK, KV_W), lambda b, n: (b, jnp.clip(n + shift, 0, nb - 1), col))
    return pl.pallas_call(
        functools.partial(_wattn_kernel, seq=L),
        grid=(B, nb),
        in_specs=[
            pl.BlockSpec(memory_space=pltpu.SMEM),
            pl.BlockSpec((1, BLOCK, Q_W), lambda b, n: (b, n, H_Q)),
            kv_spec(H_K, -1), kv_spec(H_K, 0), kv_spec(H_K, 1),
            kv_spec(H_V, -1), kv_spec(H_V, 0), kv_spec(H_V, 1),
            pl.BlockSpec((N_Q_HEADS, BLOCK, 3 * BLOCK), lambda b, n: (0, 0, 0)),
        ],
        out_specs=pl.BlockSpec((1, BLOCK, Q_W), lambda b, n: (b, n, 0)),
        out_shape=jax.ShapeDtypeStruct((B, L, Q_W), BF16),
        compiler_params=_cparams(("parallel", "parallel")),
        name="wattn",
    )(sink, h3, h3, h3, h3, h3, h3, h3, bias)


def _t5_bucket(rel):
    nb = N_BUCKETS // 2
    max_exact = nb // 2
    ret = jnp.where(rel > 0, nb, 0)
    n = jnp.abs(rel)
    nf = jnp.maximum(n, 1).astype(F32)
    large = max_exact + (jnp.log(nf / max_exact) / math.log(MAX_DISTANCE / max_exact)
                         * (nb - max_exact)).astype(jnp.int32)
    large = jnp.minimum(large, nb - 1)
    return ret + jnp.where(n < max_exact, n, large)


def _bias_table(rel_bias):
    qi = jnp.arange(BLOCK, dtype=jnp.int32)[:, None]
    kj = jnp.arange(3 * BLOCK, dtype=jnp.int32)[None, :]
    bias = rel_bias[_t5_bucket(kj - BLOCK - qi)].astype(F32)
    return jnp.transpose(bias, (2, 0, 1))


def _memattn_kernel(q_ref, mem_ref, w_ref, o_ref, kv_ref):
    @pl.when(pl.program_id(1) == 0)
    def _():
        kv_ref[...] = jnp.dot(mem_ref[0].astype(BF16), w_ref[...],
                              preferred_element_type=F32).astype(kv_ref.dtype)

    q = q_ref[0]
    outs = []
    for h in range(N_MEM_HEADS):
        lo = h * MEM_HEAD_DIM
        qh = q[:, lo:lo + MEM_HEAD_DIM]
        kh = kv_ref[:, lo:lo + MEM_HEAD_DIM]
        vh = kv_ref[:, BRANCH_W + lo:BRANCH_W + lo + MEM_HEAD_DIM]
        s = lax.dot_general(qh, kh, (((1,), (1,)), ((), ())), preferred_element_type=F32) * (MEM_HEAD_DIM ** -0.5)
        m = jnp.max(s, axis=-1, keepdims=True)
        p = jnp.exp(s - m)
        denom = jnp.sum(p, axis=-1, keepdims=True)
        outs.append(jnp.dot(p.astype(BF16), vh, preferred_element_type=F32) / denom)
    o_ref[0] = jnp.concatenate(outs, axis=1).astype(o_ref.dtype)


def _memattn(h3, mem, w_kv, tm=512):
    B, L, _ = h3.shape
    M = mem.shape[1]
    tm = min(tm, L)
    return pl.pallas_call(
        _memattn_kernel,
        grid=(B, L // tm),
        in_specs=[
            pl.BlockSpec((1, tm, BRANCH_W), lambda b, l: (b, l, H_QM)),
            pl.BlockSpec((1, M, D_MODEL), lambda b, l: (b, 0, 0)),
            pl.BlockSpec((D_MODEL, 2 * BRANCH_W), lambda b, l: (0, 0)),
        ],
        out_specs=pl.BlockSpec((1, tm, BRANCH_W), lambda b, l: (b, l, 0)),
        out_shape=jax.ShapeDtypeStruct((B, L, BRANCH_W), BF16),
        scratch_shapes=[pltpu.VMEM((M, 2 * BRANCH_W), BF16)],
        compiler_params=_cparams(("parallel", "arbitrary")),
        name="memattn",
    )(h3, mem, w_kv)


HALO = 16


def _sconv_kernel(cur_ref, prev_ref, next_ref, w_ref, b_ref, o_ref, *, nt):
    t = pl.program_id(2)
    u = cur_ref[0].astype(F32)
    tl = u.shape[0]
    pr = prev_ref[0].astype(F32)[HALO - 1:HALO]
    nx = next_ref[0].astype(F32)[0:1]
    pr = jnp.where(t == 0, 0.0, pr)
    nx = jnp.where(t == nt - 1, 0.0, nx)
    row = lax.broadcasted_iota(jnp.int32, u.shape, 0)
    up = jnp.where(row == 0, pr, pltpu.roll(u, 1, 0))
    un = jnp.where(row == tl - 1, nx, pltpu.roll(u, tl - 1, 0))
    w = w_ref[...]
    o_ref[0] = (b_ref[...] + up * w[0:1] + u * w[1:2] + un * w[2:3]).astype(o_ref.dtype)


def _sconv(h3, conv_w, conv_b, seg, tl=1024, cb=256):
    B, L, _ = h3.shape
    tl = min(tl, L)
    nt = L // tl
    per = BRANCH_W // cb
    col = seg * per
    return pl.pallas_call(
        functools.partial(_sconv_kernel, nt=nt),
        grid=(B, per, nt),
        in_specs=[
            pl.BlockSpec((1, tl, cb), lambda b, j, t: (b, t, H_HY + col + j)),
            pl.BlockSpec((1, HALO, cb), lambda b, j, t: (b, jnp.maximum(t * (tl // HALO) - 1, 0), H_HY + col + j)),
            pl.BlockSpec((1, HALO, cb),
                         lambda b, j, t: (b, jnp.minimum((t + 1) * (tl // HALO), L // HALO - 1), H_HY + col + j)),
            pl.BlockSpec((SHORT_CONV, cb), lambda b, j, t: (0, col + j)),
            pl.BlockSpec((1, cb), lambda b, j, t: (0, col + j)),
        ],
        out_specs=pl.BlockSpec((1, tl, cb), lambda b, j, t: (b, t, j)),
        out_shape=jax.ShapeDtypeStruct((B, L, BRANCH_W), BF16),
        compiler_params=_cparams(("parallel", "parallel", "parallel")),
        name="sconv",
    )(h3, h3, h3, conv_w, conv_b)


def _filt_kernel(w1_ref, b1_ref, w2_ref, b2_ref, w3_ref, b3_ref, fr_ref, fq_ref, dl_ref, k_ref, s_ref, *, seq, tr):
    i = pl.program_id(0)
    m = i * tr + lax.broadcasted_iota(jnp.int32, (tr, LANE), 0)
    lane = lax.broadcasted_iota(jnp.int32, (tr, LANE), 1)
    p = jnp.where(m < seq, m, 2 * seq - m).astype(F32)
    t = p / F32(seq - 1)
    w = F32(2.0 * math.pi) * p / F32(seq)
    arg = fq_ref[...] * w
    bands = (POS_EMB_DIM - 1) // 2
    z = jnp.where(lane == 0, t,
                  jnp.where(lane <= bands, jnp.cos(arg),
                            jnp.where(lane <= 2 * bands, -jnp.sin(arg), 0.0)))
    fr = fr_ref[...]
    h = jnp.sin(fr[0:1] * (jnp.dot(z, w1_ref[...], precision=HI, preferred_element_type=F32) + b1_ref[...]))
    h = jnp.sin(fr[1:2] * (jnp.dot(h, w2_ref[...], precision=HI, preferred_element_type=F32) + b2_ref[...]))
    h = jnp.dot(h, w3_ref[...], precision=HI, preferred_element_type=F32) + b3_ref[...]
    mc = m[:, 0:1]
    tc = t[:, 0:1]
    decay = jnp.exp(-tc * dl_ref[...])
    oc = HYENA_ORDER * BRANCH_W

    @pl.when(i == 0)
    def _():
        s_ref[...] = jnp.zeros_like(s_ref)

    for o in range(HYENA_ORDER):
        fwd = h[:, o * BRANCH_W:(o + 1) * BRANCH_W]
        bwd = h[:, oc + o * BRANCH_W:oc + (o + 1) * BRANCH_W]
        k = jnp.where(mc < seq, fwd, bwd) * decay
        k = jnp.where(mc == seq, 0.0, k)
        k_ref[o] = k
        s_ref[o] += jnp.sum(jnp.abs(k), axis=0, keepdims=True)


def _filters(seq, w1, b1, w2, b2, w3, b3, freq, tr=256):
    n = 2 * seq
    tr = min(tr, n)
    bands = (POS_EMB_DIM - 1) // 2
    hid = LANE
    w1p = jnp.zeros((LANE, hid), F32).at[:POS_EMB_DIM, :FILTER_HIDDEN].set(w1.astype(F32))
    b1p = jnp.zeros((1, hid), F32).at[0, :FILTER_HIDDEN].set(b1.astype(F32))
    w2p = jnp.zeros((hid, hid), F32).at[:FILTER_HIDDEN, :FILTER_HIDDEN].set(w2.astype(F32))
    b2p = jnp.zeros((1, hid), F32).at[0, :FILTER_HIDDEN].set(b2.astype(F32))
    w3p = jnp.zeros((hid, w3.shape[1]), F32).at[:FILTER_HIDDEN].set(w3.astype(F32))
    b3p = b3.astype(F32)[None]
    frp = jnp.zeros((2, hid), F32).at[:, :FILTER_HIDDEN].set(freq.astype(F32))
    f = jnp.linspace(1e-4, bands - 1, bands, dtype=F32)
    fq = jnp.zeros((1, LANE), F32).at[0, 1:1 + bands].set(f).at[0, 1 + bands:1 + 2 * bands].set(f)
    deltas = jnp.abs(jnp.linspace(math.log(DECAY_TARGET) / FAST_DECAY_PCT,
                                  math.log(DECAY_TARGET) / SLOW_DECAY_PCT, BRANCH_W, dtype=F32))[None]
    full = lambda a: pl.BlockSpec(a.shape, lambda i: (0,) * a.ndim)
    args = (w1p, b1p, w2p, b2p, w3p, b3p, frp, fq, deltas)
    return pl.pallas_call(
        functools.partial(_filt_kernel, seq=seq, tr=tr),
        grid=(n // tr,),
        in_specs=[full(a) for a in args],
        out_specs=[pl.BlockSpec((HYENA_ORDER, tr, BRANCH_W), lambda i: (0, i, 0)),
                   pl.BlockSpec((HYENA_ORDER, 1, BRANCH_W), lambda i: (0, 0, 0))],
        out_shape=[jax.ShapeDtypeStruct((HYENA_ORDER, n, BRANCH_W), F32),
                   jax.ShapeDtypeStruct((HYENA_ORDER, 1, BRANCH_W), F32)],
        compiler_params=_cparams(("arbitrary",)),
        name="hyena_filter",
    )(*args)


def _dft_tables(seq):
    n1 = 2 * seq // FFT_N2
    n1h = seq // FFT_N2
    n = 2 * seq
    k1 = jnp.arange(n1, dtype=jnp.int32)
    th = (2.0 * math.pi / n1) * ((k1[:, None] * k1[None, :]) % n1).astype(F32)
    c1, s1 = jnp.cos(th), jnp.sin(th)
    ch, sh = c1[:, :n1h], s1[:, :n1h]
    f1 = jnp.concatenate([jnp.concatenate([ch, sh], 1), jnp.concatenate([-sh, ch], 1)], 0)
    f1_real = jnp.concatenate([c1, -s1], 0)
    cht, sht = ch.T / n, sh.T / n
    g1 = jnp.concatenate([jnp.concatenate([cht, -sht], 1), jnp.concatenate([sht, cht], 1)], 0)
    n2 = jnp.arange(FFT_N2, dtype=jnp.int32)
    kk = k1[:, None, None] + n1 * n2[None, :, None]
    ph = (2.0 * math.pi / n) * ((kk * n2[None, None, :]) % n).astype(F32)
    c2, s2 = jnp.cos(ph), jnp.sin(ph)
    m2 = jnp.concatenate([jnp.concatenate([c2, s2], 2), jnp.concatenate([-s2, c2], 2)], 1)
    return f1, f1_real, g1, m2, jnp.swapaxes(m2, 1, 2)


def _s1_kernel(z_ref, f_ref, o_ref, *, hi):
    nb, r, w = z_ref.shape
    z = z_ref[...].reshape(nb * r, w) if nb > 1 else z_ref[0]
    if hi:
        res = jnp.dot(f_ref[...], z, precision=HI, preferred_element_type=F32)
    else:
        res = jnp.dot(f_ref[...], z.astype(BF16), preferred_element_type=F32)
    half = res.shape[0] // 2
    o_ref[0, 0] = res[:half].astype(o_ref.dtype)
    o_ref[0, 1] = res[half:].astype(o_ref.dtype)


def _fft_s1(z, f, *, pair, hi=False, n2t=8):
    bx, lz, c = z.shape
    r = lz // FFT_N2
    nb = 2 if pair else 1
    n1 = f.shape[0] // 2
    w = n2t * c
    zz = z.reshape(bx, r, FFT_N2 * c)
    out_dtype = F32 if hi else BF16
    return pl.pallas_call(
        functools.partial(_s1_kernel, hi=hi),
        grid=(bx // nb, FFT_N2 // n2t),
        in_specs=[pl.BlockSpec((nb, r, w), lambda p, j: (p, 0, j)),
                  pl.BlockSpec(f.shape, lambda p, j: (0, 0))],
        out_specs=pl.BlockSpec((1, 2, n1, w), lambda p, j: (p, 0, 0, j)),
        out_shape=jax.ShapeDtypeStruct((bx // nb, 2, n1, FFT_N2 * c), out_dtype),
        compiler_params=_cparams(("parallel", "parallel")),
        name="fft_stage1",
    )(zz, f.astype(F32 if hi else BF16))


def _mid_kernel(a_ref, m_ref, mt_ref, kf_ref, o_ref):
    k1t = a_ref.shape[2]
    h = FFT_N2
    for j in range(k1t):
        a = jnp.concatenate([a_ref[0, 0, j], a_ref[0, 1, j]], axis=0)
        x = jnp.dot(m_ref[j], a, preferred_element_type=F32)
        xr, xi = x[:h], x[h:]
        kr, ki = kf_ref[0, 0, j], kf_ref[0, 1, j]
        y = jnp.concatenate([xr * kr - xi * ki, xr * ki + xi * kr], axis=0).astype(BF16)
        b = jnp.dot(mt_ref[j], y, preferred_element_type=F32)
        o_ref[0, 0, j] = b[:h].astype(o_ref.dtype)
        o_ref[0, 1, j] = b[h:].astype(o_ref.dtype)


def _mid_fwd_kernel(a_ref, m_ref, o_ref):
    k1t = a_ref.shape[2]
    h = FFT_N2
    for j in range(k1t):
        a = jnp.concatenate([a_ref[0, 0, j], a_ref[0, 1, j]], axis=0)
        x = jnp.dot(m_ref[j], a, precision=HI, preferred_element_type=F32)
        o_ref[0, 0, j] = x[:h]
        o_ref[0, 1, j] = x[h:]


def _fft_mid(a, m2, m2t, kf, order, k1t=8):
    p, _, n1, wc = a.shape
    c = wc // FFT_N2
    k1t = min(k1t, n1)
    a5 = a.reshape(p, 2, n1, FFT_N2, c)
    blk = (1, 2, k1t, FFT_N2, c)
    out = pl.pallas_call(
        _mid_kernel,
        grid=(n1 // k1t, p),
        in_specs=[pl.BlockSpec(blk, lambda k, q: (q, 0, k, 0, 0)),
                  pl.BlockSpec((k1t, 2 * FFT_N2, 2 * FFT_N2), lambda k, q: (k, 0, 0)),
                  pl.BlockSpec((k1t, 2 * FFT_N2, 2 * FFT_N2), lambda k, q: (k, 0, 0)),
                  pl.BlockSpec(blk, lambda k, q: (order, 0, k, 0, 0))],
        out_specs=pl.BlockSpec(blk, lambda k, q: (q, 0, k, 0, 0)),
        out_shape=jax.ShapeDtypeStruct(a5.shape, BF16),
        compiler_params=_cparams(("parallel", "parallel")),
        name="fft_mid",
    )(a5, m2.astype(BF16), m2t.astype(BF16), kf)
    return out.reshape(a.shape)


def _fft_mid_fwd(a, m2, k1t=8):
    p, _, n1, wc = a.shape
    c = wc // FFT_N2
    k1t = min(k1t, n1)
    a5 = a.reshape(p, 2, n1, FFT_N2, c)
    blk = (1, 2, k1t, FFT_N2, c)
    return pl.pallas_call(
        _mid_fwd_kernel,
        grid=(n1 // k1t, p),
        in_specs=[pl.BlockSpec(blk, lambda k, q: (q, 0, k, 0, 0)),
                  pl.BlockSpec((k1t, 2 * FFT_N2, 2 * FFT_N2), lambda k, q: (k, 0, 0))],
        out_specs=pl.BlockSpec(blk, lambda k, q: (q, 0, k, 0, 0)),
        out_shape=jax.ShapeDtypeStruct(a5.shape, F32),
        compiler_params=_cparams(("parallel", "parallel")),
        name="fft_mid_filter",
    )(a5, m2)


def _s3_kernel(b_ref, g_ref, v_ref, x_ref, sc_ref, d_ref, o_ref):
    bm = jnp.concatenate([b_ref[0, 0], b_ref[0, 1]], axis=0)
    y = jnp.dot(g_ref[...], bm, preferred_element_type=F32)
    r = v_ref.shape[1]
    for part in range(2):
        yp = y[part * r:(part + 1) * r]
        v = v_ref[part].astype(F32)
        x = x_ref[part].astype(F32)
        o_ref[part] = (x * (yp * sc_ref[...] + v * d_ref[...])).astype(o_ref.dtype)


def _fft_s3(bm, g1, v, x, inv_norm, skip, n2t=8):
    p, _, n1, wc = bm.shape
    bx, seq, c = v.shape
    r = seq // FFT_N2
    w = n2t * c
    vv = v.reshape(bx, r, FFT_N2 * c)
    xx = x.reshape(bx, r, FFT_N2 * c)
    sc = jnp.tile(inv_norm.astype(F32).reshape(1, c), (1, n2t))
    dd = jnp.tile(skip.astype(F32).reshape(1, c), (1, n2t))
    io = pl.BlockSpec((2, r, w), lambda q, j: (q, 0, j))
    out = pl.pallas_call(
        _s3_kernel,
        grid=(p, FFT_N2 // n2t),
        in_specs=[pl.BlockSpec((1, 2, n1, w), lambda q, j: (q, 0, 0, j)),
                  pl.BlockSpec(g1.shape, lambda q, j: (0, 0)),
                  io, io,
                  pl.BlockSpec((1, w), lambda q, j: (0, 0)),
                  pl.BlockSpec((1, w), lambda q, j: (0, 0))],
        out_specs=io,
        out_shape=jax.ShapeDtypeStruct(vv.shape, BF16),
        compiler_params=_cparams(("parallel", "parallel")),
        name="fft_stage3",
    )(bm, g1.astype(BF16), vv, xx, sc, dd)
    return out.reshape(bx, seq, c)


def _hyena(h3, conv_w, conv_b, filt, skip):
    seq = h3.shape[1]
    f1, f1_real, g1, m2, m2t = _dft_tables(seq)
    taps, asum = _filters(seq, *filt)
    kf = _fft_mid_fwd(_fft_s1(taps, f1_real, pair=False, hi=True), m2)
    inv_norm = 1.0 / asum[:, 0]
    uc = [_sconv(h3, conv_w, conv_b, seg) for seg in range(3)]
    z = uc[0]
    for o in range(HYENA_ORDER):
        spec = _fft_s1(z, f1, pair=True)
        spec = _fft_mid(spec, m2, m2t, kf, o)
        z = _fft_s3(spec, g1, z, uc[1 + o], inv_norm[o], skip[o])
    return z


def _mix_kernel(x_ref, a_ref, hb_ref, c_ref, g0_ref, g1_ref, g2_ref, wb_ref, wo_ref,
                lig_ref, lib_ref, l1g_ref, l1b_ref, wrh_ref, wrl_ref, br_ref, x1_ref, x1b_ref, rt_ref, cnt_ref):
    acc = None
    for n, (br, gr) in enumerate(((a_ref, g0_ref), (hb_ref, g1_ref), (c_ref, g2_ref))):
        proj = jnp.dot(br[...], wb_ref[n], preferred_element_type=F32)
        term = jax.nn.sigmoid(gr[...].astype(F32)) * proj
        acc = term if acc is None else acc + term
    mix = jnp.dot(acc.astype(BF16), wo_ref[...], preferred_element_type=F32)
    xn = _ln(x_ref[...], lig_ref[...], lib_ref[...])
    x1 = _ln(DN_ALPHA * xn + mix, l1g_ref[...], l1b_ref[...])
    x1_ref[...] = x1
    xh = x1.astype(BF16)
    x1b_ref[...] = xh
    xl = (x1 - xh.astype(F32)).astype(BF16)
    nt = (((1,), (1,)), ((), ()))
    work = lax.dot_general(wrh_ref[...], xh, nt, preferred_element_type=F32)
    work = work + lax.dot_general(wrh_ref[...], xl, nt, preferred_element_type=F32)
    work = work + lax.dot_general(wrl_ref[...], xh, nt, preferred_element_type=F32)
    work = work + br_ref[...]

    tm = work.shape[1]
    erow = lax.broadcasted_iota(jnp.int32, work.shape, 0)
    ids, vals, hots = [], [], []
    for _ in range(TOP_K):
        mx = jnp.max(work, axis=0, keepdims=True)
        idx = jnp.min(jnp.where(work == mx, erow, N_EXPERTS), axis=0, keepdims=True)
        hot = erow == idx
        ids.append(idx)
        vals.append(mx)
        hots.append(hot)
        work = jnp.where(hot, F32(-3e38), work)
    exps = [jnp.exp(v - vals[0]) for v in vals]
    den = exps[0]
    for e in exps[1:]:
        den = den + e
    chosen = hots[0]
    for hot in hots[1:]:
        chosen = chosen | hot
    chosen = chosen.astype(F32)

    @pl.when(pl.program_id(0) == 0)
    def _():
        cnt_ref[...] = jnp.zeros_like(cnt_ref)

    r_i = lax.broadcasted_iota(jnp.int32, (tm, tm), 0)
    c_i = lax.broadcasted_iota(jnp.int32, (tm, tm), 1)
    earlier = (r_i < c_i).astype(BF16)
    before = jnp.dot(chosen.astype(BF16), earlier, preferred_element_type=F32)
    before = before + jnp.tile(cnt_ref[...], (1, tm // LANE))
    cnt_ref[...] += jnp.sum(chosen, axis=1, keepdims=True)
    rows = [i.astype(F32) for i in ids] + [e / den for e in exps]
    rows += [jnp.sum(jnp.where(hot, before, 0.0), axis=0, keepdims=True) for hot in hots]
    rows.append(jnp.zeros((rt_ref.shape[0] - len(rows), tm), F32))
    rt_ref[...] = jnp.concatenate(rows, axis=0)


ROUTE_ROWS = 16


def _mix(x2, a, hb, c, h2, wb, wo, lig, lib, l1g, l1b, wr, br, tm=512):
    T, D = x2.shape
    wrt = wr.T
    wrh = wrt.astype(BF16)
    wrl = (wrt - wrh.astype(F32)).astype(BF16)
    br = br.reshape(N_EXPERTS, 1)
    row = lambda wdt, col: pl.BlockSpec((tm, wdt), lambda i: (i, col))
    const = lambda arr: pl.BlockSpec(arr.shape, lambda i: (0,) * arr.ndim)
    return pl.pallas_call(
        _mix_kernel,
        grid=(T // tm,),
        in_specs=[row(D, 0), row(BRANCH_W, 0), row(BRANCH_W, 0), row(BRANCH_W, 0),
                  row(D, H_G), row(D, H_G + 1), row(D, H_G + 2),
                  const(wb), const(wo), const(lig), const(lib), const(l1g), const(l1b), const(wrh), const(wrl), const(br)],
        out_specs=[row(D, 0), row(D, 0), pl.BlockSpec((ROUTE_ROWS, tm), lambda i: (0, i)),
                   pl.BlockSpec((N_EXPERTS, LANE), lambda i: (0, 0))],
        out_shape=[jax.ShapeDtypeStruct((T, D), F32), jax.ShapeDtypeStruct((T, D), BF16),
                   jax.ShapeDtypeStruct((ROUTE_ROWS, T), F32), jax.ShapeDtypeStruct((N_EXPERTS, LANE), F32)],
        compiler_params=_cparams(("arbitrary",)),
        name="mix",
    )(x2, a, hb, c, h2, h2, h2, wb, wo, lig, lib, l1g, l1b, wrh, wrl, br)


def _ffn_kernel(ce_ref, x_ref, gt_ref, wgu_ref, bgu_ref, wd_ref, bd_ref, o_ref):
    del ce_ref
    gu = jnp.dot(x_ref[...], wgu_ref[0], preferred_element_type=F32) + bgu_ref[0]
    f = gu.shape[1] // 2
    g = jnp.minimum(gu[:, :f], SWIGLU_LIMIT)
    u = jnp.clip(gu[:, f:], -SWIGLU_LIMIT, SWIGLU_LIMIT)
    hmid = (u + 1.0) * (g * jax.nn.sigmoid(g * SWIGLU_ALPHA))
    y = jnp.dot(hmid.astype(BF16), wd_ref[0], preferred_element_type=F32) + bd_ref[0]
    o_ref[...] = (y * gt_ref[...]).astype(o_ref.dtype)


def _ffn(chunk_e, x_slots, slot_gate, wgu, bgu, wd, bd):
    P, D = x_slots.shape
    F = wd.shape[1]
    n_chunks = P // MOE_TILE
    wspec = lambda a, b: pl.BlockSpec((1, a, b), lambda i, ce: (ce[i], 0, 0))
    grid_spec = pltpu.PrefetchScalarGridSpec(
        num_scalar_prefetch=1,
        grid=(n_chunks,),
        in_specs=[pl.BlockSpec((MOE_TILE, D), lambda i, ce: (i, 0)),
                  pl.BlockSpec((MOE_TILE, 1), lambda i, ce: (i, 0)),
                  wspec(D, 2 * F), wspec(1, 2 * F), wspec(F, D), wspec(1, D)],
        out_specs=pl.BlockSpec((MOE_TILE, D), lambda i, ce: (i, 0)),
    )
    return pl.pallas_call(
        _ffn_kernel,
        grid_spec=grid_spec,
        out_shape=jax.ShapeDtypeStruct((P, D), BF16),
        compiler_params=_cparams(("arbitrary",)),
        name="moe_ffn",
    )(chunk_e, x_slots, slot_gate, wgu, bgu, wd, bd)


DEINT_W = 256


def _deint_kernel(w_ref, s_ref, o_ref):
    w = w_ref[0].astype(BF16)
    f = w.shape[1] // 2
    half = DEINT_W // 2
    for blk in range(w.shape[1] // DEINT_W):
        r = jnp.dot(w[:, blk * DEINT_W:(blk + 1) * DEINT_W], s_ref[...], preferred_element_type=F32)
        o_ref[0, :, blk * half:(blk + 1) * half] = r[:, :half].astype(o_ref.dtype)
        o_ref[0, :, f + blk * half:f + (blk + 1) * half] = r[:, half:].astype(o_ref.dtype)


def _deinterleave(w, tr=512):
    E, D, F2 = w.shape
    i = jnp.arange(DEINT_W, dtype=jnp.int32)
    src = jnp.where(i < DEINT_W // 2, 2 * i, 2 * (i - DEINT_W // 2) + 1)
    sel = (i[:, None] == src[None, :]).astype(BF16)
    return pl.pallas_call(
        _deint_kernel,
        grid=(E, D // tr),
        in_specs=[pl.BlockSpec((1, tr, F2), lambda e, r: (e, r, 0)),
                  pl.BlockSpec((DEINT_W, DEINT_W), lambda e, r: (0, 0))],
        out_specs=pl.BlockSpec((1, tr, F2), lambda e, r: (e, r, 0)),
        out_shape=jax.ShapeDtypeStruct((E, D, F2), BF16),
        compiler_params=_cparams(("parallel", "parallel")),
        name="deinterleave_gate_up",
    )(w, sel)


def _final_kernel(x1_ref, y_ref, g_ref, b_ref, o_ref):
    moe = y_ref[0].astype(F32)
    for k in range(1, TOP_K):
        moe = moe + y_ref[k].astype(F32)
    o_ref[...] = _ln(DN_ALPHA * x1_ref[...] + moe, g_ref[...], b_ref[...])


def _final(x1, yg, g, b, tm=512):
    T, D = x1.shape
    return pl.pallas_call(
        _final_kernel,
        grid=(T // tm,),
        in_specs=[pl.BlockSpec((tm, D), lambda i: (i, 0)),
                  pl.BlockSpec((TOP_K, tm, D), lambda i: (0, i, 0)),
                  pl.BlockSpec((1, D), lambda i: (0, 0)),
                  pl.BlockSpec((1, D), lambda i: (0, 0))],
        out_specs=pl.BlockSpec((tm, D), lambda i: (i, 0)),
        out_shape=jax.ShapeDtypeStruct((T, D), F32),
        compiler_params=_cparams(("parallel",)),
        name="final_ln",
    )(x1, yg, g, b)


def _moe(x1, x1b, route, cnt, ffn_w, ln2_g, ln2_b):
    T, D = x1.shape
    A = T * TOP_K
    ids = route[:TOP_K].astype(jnp.int32)
    gate = route[TOP_K:2 * TOP_K]
    rank = route[2 * TOP_K:3 * TOP_K].astype(jnp.int32)
    counts = cnt[:, 0].astype(jnp.int32)
    experts = jnp.arange(N_EXPERTS, dtype=jnp.int32)
    starts = jnp.cumsum(counts) - counts
    pcounts = (counts + MOE_TILE - 1) // MOE_TILE * MOE_TILE
    pends = jnp.cumsum(pcounts)
    pstarts = pends - pcounts
    pos = rank + jnp.sum(jnp.where(ids[..., None] == experts, pstarts, 0), axis=-1)
    _, order = lax.sort((pos.reshape(A), jnp.arange(A, dtype=jnp.int32)), num_keys=1)
    n_chunks = -(-A // MOE_TILE) + N_EXPERTS
    P = n_chunks * MOE_TILE
    chunk_lo = jnp.arange(n_chunks, dtype=jnp.int32) * MOE_TILE
    chunk_e = jnp.minimum(jnp.sum((chunk_lo[:, None] >= pends[None, :]).astype(jnp.int32), axis=1), N_EXPERTS - 1)
    per_slot = lambda v: jnp.repeat(v[chunk_e], MOE_TILE)
    off = jnp.arange(P, dtype=jnp.int32) - per_slot(pstarts)
    valid = off < per_slot(counts)
    src = order[jnp.clip(per_slot(starts) + off, 0, A - 1)]
    slot_tok = jnp.where(valid, src % T, 0)
    slot_gate = jnp.where(valid, gate.reshape(A)[src], 0.0)
    x_slots = x1b[slot_tok]
    y_slots = _ffn(chunk_e, x_slots, slot_gate[:, None], *ffn_w)
    yg = y_slots[pos.reshape(A)].reshape(TOP_K, T, D)
    return _final(x1, yg, ln2_g, ln2_b)


def _trunk(x, mem, p):
    B, L, D = x.shape
    x2 = x.reshape(B * L, D)
    h2 = _inproj(x2, p["ln_in_g"], p["ln_in_b"], p["w_in"])
    h3 = h2.reshape(B, L, IN_W)
    a = _wattn(h3, p["sink"], p["bias"])
    hb = _hyena(h3, p["conv_w"], p["conv_b"], p["filt"], p["skip"])
    c = _memattn(h3, mem, p["w_mem_kv"])
    x1, x1b, route, cnt = _mix(x2, a.reshape(B * L, -1), hb.reshape(B * L, -1), c.reshape(B * L, -1), h2,
                               p["w_branch"], p["w_out"], p["ln_in_g"], p["ln_in_b"], p["ln1_g"], p["ln1_b"],
                               p["w_router"], p["b_router"])
    out = _moe(x1, x1b, route, cnt, p["ffn"], p["ln2_g"], p["ln2_b"])
    return out.reshape(B, L, D)


def _prep(ln_in_g, ln_in_b, rel_bias, w_in, attn_sink, conv_w, conv_b, filt_w1, filt_b1, filt_w2, filt_b2,
          filt_w3, filt_b3, filt_freq, hyena_skip, w_mem_kv, w_branch, w_out, ln1_g, ln1_b, w_router, b_router,
          w_gate_up, b_gate_up, w_down, b_down, ln2_g, ln2_b):
    w = w_in
    g_lo = Q_W + 2 * KV_W + 3 * BRANCH_W + BRANCH_W
    hy_lo = Q_W + 2 * KV_W
    assert HEAD_DIM ** -0.5 == 2.0 ** -3
    w_p = jnp.concatenate([w[:, g_lo:], w[:, :Q_W] * (HEAD_DIM ** -0.5), w[:, hy_lo + 3 * BRANCH_W:g_lo],
                           w[:, hy_lo:hy_lo + 3 * BRANCH_W], w[:, Q_W:hy_lo]], axis=1).astype(BF16)
    row = lambda v: v.astype(F32)[None]
    wr = w_router.astype(F32)
    br = b_router.astype(F32)
    return dict(
        ln_in_g=row(ln_in_g), ln_in_b=row(ln_in_b), w_in=w_p,
        sink=attn_sink.astype(F32), bias=_bias_table(rel_bias),
        conv_w=conv_w.astype(F32), conv_b=row(conv_b),
        filt=(filt_w1, filt_b1, filt_w2, filt_b2, filt_w3, filt_b3, filt_freq), skip=hyena_skip,
        w_mem_kv=w_mem_kv.astype(BF16), w_branch=w_branch.astype(BF16), w_out=w_out.astype(BF16),
        ln1_g=row(ln1_g), ln1_b=row(ln1_b), w_router=wr, b_router=br,
        ffn=(_deinterleave(w_gate_up),
             jnp.concatenate([b_gate_up[:, 0::2], b_gate_up[:, 1::2]], axis=1)[:, None, :].astype(F32),
             w_down.astype(BF16), b_down[:, None, :].astype(F32)),
        ln2_g=row(ln2_g), ln2_b=row(ln2_b),
    )


def kernel(x_prompt, x_sample, mem_prompt, mem_sample, ln_in_g, ln_in_b, rel_bias, w_in, attn_sink, conv_w, conv_b, filt_w1, filt_b1, filt_w2, filt_b2, filt_w3, filt_b3, filt_freq, hyena_skip, w_mem_kv, w_branch, w_out, ln1_g, ln1_b, w_router, b_router, w_gate_up, b_gate_up, w_down, b_down, ln2_g, ln2_b):
    p = _prep(ln_in_g, ln_in_b, rel_bias, w_in[0], attn_sink[0], conv_w[0], conv_b[0], filt_w1[0], filt_b1[0],
              filt_w2[0], filt_b2[0], filt_w3[0], filt_b3[0], filt_freq[0], hyena_skip[0], w_mem_kv[0],
              w_branch[0], w_out[0], ln1_g[0], ln1_b[0], w_router[0], b_router[0], w_gate_up[0], b_gate_up[0],
              w_down[0], b_down[0], ln2_g[0], ln2_b[0])
    return (_trunk(x_prompt, mem_prompt, p), _trunk(x_sample, mem_sample, p))
```

```python
import functools
import math

import jax
import jax.numpy as jnp
from jax import lax
from jax.experimental import pallas as pl
from jax.experimental.pallas import tpu as pltpu

F32 = jnp.float32
BF16 = jnp.bfloat16
HI = lax.Precision.HIGHEST

D_MODEL = 1024
BRANCH_W = 512
N_Q_HEADS = 8
N_KV_HEADS = 2
HEAD_DIM = 64
WINDOW = 128
BLOCK = 128
N_BUCKETS = 32
MAX_DISTANCE = 128
HYENA_ORDER = 2
SHORT_CONV = 3
POS_EMB_DIM = 33
FILTER_HIDDEN = 64
DECAY_TARGET = 1e-2
FAST_DECAY_PCT = 0.3
SLOW_DECAY_PCT = 1.5
N_MEM_HEADS = 4
MEM_HEAD_DIM = BRANCH_W // N_MEM_HEADS
N_BRANCHES = 3
N_EXPERTS = 32
TOP_K = 4
D_EXPERT = 1024
SWIGLU_LIMIT = 7.0
SWIGLU_ALPHA = 1.702
LN_EPS = 1e-5
DEPTH = 1
DN_ALPHA = (2 * DEPTH) ** 0.25

Q_W = N_Q_HEADS * HEAD_DIM
KV_W = N_KV_HEADS * HEAD_DIM
IN_W = Q_W + 2 * KV_W + 3 * BRANCH_W + BRANCH_W + N_BRANCHES * D_MODEL

H_G = 0
H_Q = 6
H_QM = 7
H_HY = 16
H_K = 44
H_V = 45

FFT_N2 = 128
MOE_TILE = 512
LANE = 128
VMEM_LIMIT = 52 * 1024 * 1024


def _cparams(sem):
    return pltpu.CompilerParams(dimension_semantics=sem, vmem_limit_bytes=VMEM_LIMIT)


def _ln(x, g, b):
    mu = jnp.mean(x, axis=-1, keepdims=True)
    xc = x - mu
    var = jnp.mean(xc * xc, axis=-1, keepdims=True)
    return xc * lax.rsqrt(var + LN_EPS) * g + b


def _inproj_kernel(x_ref, g_ref, b_ref, w_ref, o_ref):
    xn = _ln(x_ref[...], g_ref[...], b_ref[...])
    o_ref[...] = jnp.dot(xn.astype(BF16), w_ref[...], preferred_element_type=F32).astype(o_ref.dtype)


def _inproj(x2, ln_g, ln_b, w_p, tm=512):
    T, D = x2.shape
    N = w_p.shape[1]
    tn = N // 2
    return pl.pallas_call(
        _inproj_kernel,
        grid=(N // tn, T // tm),
        in_specs=[
            pl.BlockSpec((tm, D), lambda j, i: (i, 0)),
            pl.BlockSpec((1, D), lambda j, i: (0, 0)),
            pl.BlockSpec((1, D), lambda j, i: (0, 0)),
            pl.BlockSpec((D, tn), lambda j, i: (0, j)),
        ],
        out_specs=pl.BlockSpec((tm, tn), lambda j, i: (i, j)),
        out_shape=jax.ShapeDtypeStruct((T, N), BF16),
        compiler_params=_cparams(("parallel", "parallel")),
        name="inproj",
    )(x2, ln_g, ln_b, w_p)


def _wattn_kernel(sink_ref, q_ref, kp_ref, kc_ref, kn_ref, vp_ref, vc_ref, vn_ref, bias_ref, o_ref, *, seq):
    n = pl.program_id(1)
    q = q_ref[0]
    kcat = jnp.concatenate([kp_ref[0], kc_ref[0], kn_ref[0]], axis=0)
    vcat = jnp.concatenate([vp_ref[0], vc_ref[0], vn_ref[0]], axis=0)
    qi = lax.broadcasted_iota(jnp.int32, (BLOCK, 3 * BLOCK), 0)
    kj = lax.broadcasted_iota(jnp.int32, (BLOCK, 3 * BLOCK), 1)
    rel = kj - BLOCK - qi
    kpos = (n - 1) * BLOCK + kj
    valid = (jnp.abs(rel) <= WINDOW) & (kpos >= 0) & (kpos < seq)
    group = N_Q_HEADS // N_KV_HEADS
    scores = []
    for h in range(N_Q_HEADS):
        g = h // group
        qh = q[:, h * HEAD_DIM:(h + 1) * HEAD_DIM]
        kh = kcat[:, g * HEAD_DIM:(g + 1) * HEAD_DIM]
        scores.append(lax.dot_general(qh, kh, (((1,), (1,)), ((), ())), preferred_element_type=F32))
    probs, denoms = [], []
    for h in range(N_Q_HEADS):
        s = jnp.where(valid, scores[h] + bias_ref[h], F32(-1e30))
        sk = sink_ref[h]
        m = jnp.maximum(jnp.max(s, axis=-1, keepdims=True), sk)
        p = jnp.exp(s - m)
        denoms.append(jnp.sum(p, axis=-1, keepdims=True) + jnp.exp(sk - m))
        probs.append(p.astype(BF16))
    outs = []
    for h in range(N_Q_HEADS):
        g = h // group
        vh = vcat[:, g * HEAD_DIM:(g + 1) * HEAD_DIM]
        outs.append(jnp.dot(probs[h], vh, preferred_element_type=F32) / denoms[h])
    o_ref[0] = jnp.concatenate(outs, axis=1).astype(o_ref.dtype)


def _wattn(h3, sink, bias):
    B, L, _ = h3.shape
    nb = L // BLOCK
    kv_spec = lambda col, shift: pl.BlockSpec(
        (1, BLOCK, KV_W), lambda b, n: (b, jnp.clip(n + shift, 0, nb - 1), col))
    return pl.pallas_call(
        functools.partial(_wattn_kernel, seq=L),
        grid=(B, nb),
        in_specs=[
            pl.BlockSpec(memory_space=pltpu.SMEM),
            pl.BlockSpec((1, BLOCK, Q_W), lambda b, n: (b, n, H_Q)),
            kv_spec(H_K, -1), kv_spec(H_K, 0), kv_spec(H_K, 1),
            kv_spec(H_V, -1), kv_spec(H_V, 0), kv_spec(H_V, 1),
            pl.BlockSpec((N_Q_HEADS, BLOCK, 3 * BLOCK), lambda b, n: (0, 0, 0)),
        ],
        out_specs=pl.BlockSpec((1, BLOCK, Q_W), lambda b, n: (b, n, 0)),
        out_shape=jax.ShapeDtypeStruct((B, L, Q_W), BF16),
        compiler_params=_cparams(("parallel", "parallel")),
        name="wattn",
    )(sink, h3, h3, h3, h3, h3, h3, h3, bias)


def _t5_bucket(rel):
    nb = N_BUCKETS // 2
    max_exact = nb // 2
    ret = jnp.where(rel > 0, nb, 0)
    n = jnp.abs(rel)
    nf = jnp.maximum(n, 1).astype(F32)
    large = max_exact + (jnp.log(nf / max_exact) / math.log(MAX_DISTANCE / max_exact)
                         * (nb - max_exact)).astype(jnp.int32)
    large = jnp.minimum(large, nb - 1)
    return ret + jnp.where(n < max_exact, n, large)


def _bias_table(rel_bias):
    qi = jnp.arange(BLOCK, dtype=jnp.int32)[:, None]
    kj = jnp.arange(3 * BLOCK, dtype=jnp.int32)[None, :]
    bias = rel_bias[_t5_bucket(kj - BLOCK - qi)].astype(F32)
    return jnp.transpose(bias, (2, 0, 1))


def _memattn_kernel(q_ref, mem_ref, w_ref, o_ref, kv_ref):
    @pl.when(pl.program_id(1) == 0)
    def _():
        kv_ref[...] = jnp.dot(mem_ref[0].astype(BF16), w_ref[...],
                              preferred_element_type=F32).astype(kv_ref.dtype)

    q = q_ref[0]
    outs = []
    for h in range(N_MEM_HEADS):
        lo = h * MEM_HEAD_DIM
        qh = q[:, lo:lo + MEM_HEAD_DIM]
        kh = kv_ref[:, lo:lo + MEM_HEAD_DIM]
        vh = kv_ref[:, BRANCH_W + lo:BRANCH_W + lo + MEM_HEAD_DIM]
        s = lax.dot_general(qh, kh, (((1,), (1,)), ((), ())), preferred_element_type=F32) * (MEM_HEAD_DIM ** -0.5)
        m = jnp.max(s, axis=-1, keepdims=True)
        p = jnp.exp(s - m)
        denom = jnp.sum(p, axis=-1, keepdims=True)
        outs.append(jnp.dot(p.astype(BF16), vh, preferred_element_type=F32) / denom)
    o_ref[0] = jnp.concatenate(outs, axis=1).astype(o_ref.dtype)


def _memattn(h3, mem, w_kv, tm=512):
    B, L, _ = h3.shape
    M = mem.shape[1]
    tm = min(tm, L)
    return pl.pallas_call(
        _memattn_kernel,
        grid=(B, L // tm),
        in_specs=[
            pl.BlockSpec((1, tm, BRANCH_W), lambda b, l: (b, l, H_QM)),
            pl.BlockSpec((1, M, D_MODEL), lambda b, l: (b, 0, 0)),
            pl.BlockSpec((D_MODEL, 2 * BRANCH_W), lambda b, l: (0, 0)),
        ],
        out_specs=pl.BlockSpec((1, tm, BRANCH_W), lambda b, l: (b, l, 0)),
        out_shape=jax.ShapeDtypeStruct((B, L, BRANCH_W), BF16),
        scratch_shapes=[pltpu.VMEM((M, 2 * BRANCH_W), BF16)],
        compiler_params=_cparams(("parallel", "arbitrary")),
        name="memattn",
    )(h3, mem, w_kv)


HALO = 16


def _sconv_kernel(cur_ref, prev_ref, next_ref, w_ref, b_ref, o_ref, *, nt):
    t = pl.program_id(2)
    u = cur_ref[0].astype(F32)
    tl = u.shape[0]
    pr = prev_ref[0].astype(F32)[HALO - 1:HALO]
    nx = next_ref[0].astype(F32)[0:1]
    pr = jnp.where(t == 0, 0.0, pr)
    nx = jnp.where(t == nt - 1, 0.0, nx)
    row = lax.broadcasted_iota(jnp.int32, u.shape, 0)
    up = jnp.where(row == 0, pr, pltpu.roll(u, 1, 0))
    un = jnp.where(row == tl - 1, nx, pltpu.roll(u, tl - 1, 0))
    w = w_ref[...]
    o_ref[0] = (b_ref[...] + up * w[0:1] + u * w[1:2] + un * w[2:3]).astype(o_ref.dtype)


def _sconv(h3, conv_w, conv_b, seg, tl=1024, cb=256):
    B, L, _ = h3.shape
    tl = min(tl, L)
    nt = L // tl
    per = BRANCH_W // cb
    col = seg * per
    return pl.pallas_call(
        functools.partial(_sconv_kernel, nt=nt),
        grid=(B, per, nt),
        in_specs=[
            pl.BlockSpec((1, tl, cb), lambda b, j, t: (b, t, H_HY + col + j)),
            pl.BlockSpec((1, HALO, cb), lambda b, j, t: (b, jnp.maximum(t * (tl // HALO) - 1, 0), H_HY + col + j)),
            pl.BlockSpec((1, HALO, cb),
                         lambda b, j, t: (b, jnp.minimum((t + 1) * (tl // HALO), L // HALO - 1), H_HY + col + j)),
            pl.BlockSpec((SHORT_CONV, cb), lambda b, j, t: (0, col + j)),
            pl.BlockSpec((1, cb), lambda b, j, t: (0, col + j)),
        ],
        out_specs=pl.BlockSpec((1, tl, cb), lambda b, j, t: (b, t, j)),
        out_shape=jax.ShapeDtypeStruct((B, L, BRANCH_W), BF16),
        compiler_params=_cparams(("parallel", "parallel", "parallel")),
        name="sconv",
    )(h3, h3, h3, conv_w, conv_b)


def _filt_kernel(w1_ref, b1_ref, w2_ref, b2_ref, w3_ref, b3_ref, fr_ref, fq_ref, dl_ref, k_ref, s_ref, *, seq, tr):
    i = pl.program_id(0)
    m = i * tr + lax.broadcasted_iota(jnp.int32, (tr, LANE), 0)
    lane = lax.broadcasted_iota(jnp.int32, (tr, LANE), 1)
    p = jnp.where(m < seq, m, 2 * seq - m).astype(F32)
    t = p / F32(seq - 1)
    w = F32(2.0 * math.pi) * p / F32(seq)
    arg = fq_ref[...] * w
    bands = (POS_EMB_DIM - 1) // 2
    z = jnp.where(lane == 0, t,
                  jnp.where(lane <= bands, jnp.cos(arg),
                            jnp.where(lane <= 2 * bands, -jnp.sin(arg), 0.0)))
    fr = fr_ref[...]
    h = jnp.sin(fr[0:1] * (jnp.dot(z, w1_ref[...], precision=HI, preferred_element_type=F32) + b1_ref[...]))
    h = jnp.sin(fr[1:2] * (jnp.dot(h, w2_ref[...], precision=HI, preferred_element_type=F32) + b2_ref[...]))
    h = jnp.dot(h, w3_ref[...], precision=HI, preferred_element_type=F32) + b3_ref[...]
    mc = m[:, 0:1]
    tc = t[:, 0:1]
    decay = jnp.exp(-tc * dl_ref[...])
    oc = HYENA_ORDER * BRANCH_W

    @pl.when(i == 0)
    def _():
        s_ref[...] = jnp.zeros_like(s_ref)

    for o in range(HYENA_ORDER):
        fwd = h[:, o * BRANCH_W:(o + 1) * BRANCH_W]
        bwd = h[:, oc + o * BRANCH_W:oc + (o + 1) * BRANCH_W]
        k = jnp.where(mc < seq, fwd, bwd) * decay
        k = jnp.where(mc == seq, 0.0, k)
        k_ref[o] = k
        s_ref[o] += jnp.sum(jnp.abs(k), axis=0, keepdims=True)


def _filters(seq, w1, b1, w2, b2, w3, b3, freq, tr=256):
    n = 2 * seq
    tr = min(tr, n)
    bands = (POS_EMB_DIM - 1) // 2
    hid = LANE
    w1p = jnp.zeros((LANE, hid), F32).at[:POS_EMB_DIM, :FILTER_HIDDEN].set(w1.astype(F32))
    b1p = jnp.zeros((1, hid), F32).at[0, :FILTER_HIDDEN].set(b1.astype(F32))
    w2p = jnp.zeros((hid, hid), F32).at[:FILTER_HIDDEN, :FILTER_HIDDEN].set(w2.astype(F32))
    b2p = jnp.zeros((1, hid), F32).at[0, :FILTER_HIDDEN].set(b2.astype(F32))
    w3p = jnp.zeros((hid, w3.shape[1]), F32).at[:FILTER_HIDDEN].set(w3.astype(F32))
    b3p = b3.astype(F32)[None]
    frp = jnp.zeros((2, hid), F32).at[:, :FILTER_HIDDEN].set(freq.astype(F32))
    f = jnp.linspace(1e-4, bands - 1, bands, dtype=F32)
    fq = jnp.zeros((1, LANE), F32).at[0, 1:1 + bands].set(f).at[0, 1 + bands:1 + 2 * bands].set(f)
    deltas = jnp.abs(jnp.linspace(math.log(DECAY_TARGET) / FAST_DECAY_PCT,
                                  math.log(DECAY_TARGET) / SLOW_DECAY_PCT, BRANCH_W, dtype=F32))[None]
    full = lambda a: pl.BlockSpec(a.shape, lambda i: (0,) * a.ndim)
    args = (w1p, b1p, w2p, b2p, w3p, b3p, frp, fq, deltas)
    return pl.pallas_call(
        functools.partial(_filt_kernel, seq=seq, tr=tr),
        grid=(n // tr,),
        in_specs=[full(a) for a in args],
        out_specs=[pl.BlockSpec((HYENA_ORDER, tr, BRANCH_W), lambda i: (0, i, 0)),
                   pl.BlockSpec((HYENA_ORDER, 1, BRANCH_W), lambda i: (0, 0, 0))],
        out_shape=[jax.ShapeDtypeStruct((HYENA_ORDER, n, BRANCH_W), F32),
                   jax.ShapeDtypeStruct((HYENA_ORDER, 1, BRANCH_W), F32)],
        compiler_params=_cparams(("arbitrary",)),
        name="hyena_filter",
    )(*args)


def _dft_tables(seq):
    n1 = 2 * seq // FFT_N2
    n1h = seq // FFT_N2
    n = 2 * seq
    k1 = jnp.arange(n1, dtype=jnp.int32)
    th = (2.0 * math.pi / n1) * ((k1[:, None] * k1[None, :]) % n1).astype(F32)
    c1, s1 = jnp.cos(th), jnp.sin(th)
    ch, sh = c1[:, :n1h], s1[:, :n1h]
    f1 = jnp.concatenate([jnp.concatenate([ch, sh], 1), jnp.concatenate([-sh, ch], 1)], 0)
    f1_real = jnp.concatenate([c1, -s1], 0)
    cht, sht = ch.T / n, sh.T / n
    g1 = jnp.concatenate([jnp.concatenate([cht, -sht], 1), jnp.concatenate([sht, cht], 1)], 0)
    n2 = jnp.arange(FFT_N2, dtype=jnp.int32)
    kk = k1[:, None, None] + n1 * n2[None, :, None]
    ph = (2.0 * math.pi / n) * ((kk * n2[None, None, :]) % n).astype(F32)
    c2, s2 = jnp.cos(ph), jnp.sin(ph)
    m2 = jnp.concatenate([jnp.concatenate([c2, s2], 2), jnp.concatenate([-s2, c2], 2)], 1)
    return f1, f1_real, g1, m2, jnp.swapaxes(m2, 1, 2)


def _s1_kernel(z_ref, f_ref, o_ref, *, hi):
    nb, r, w = z_ref.shape
    z = z_ref[...].reshape(nb * r, w) if nb > 1 else z_ref[0]
    if hi:
        res = jnp.dot(f_ref[...], z, precision=HI, preferred_element_type=F32)
    else:
        res = jnp.dot(f_ref[...], z.astype(BF16), preferred_element_type=F32)
    half = res.shape[0] // 2
    o_ref[0, 0] = res[:half].astype(o_ref.dtype)
    o_ref[0, 1] = res[half:].astype(o_ref.dtype)


def _fft_s1(z, f, *, pair, hi=False, n2t=8):
    bx, lz, c = z.shape
    r = lz // FFT_N2
    nb = 2 if pair else 1
    n1 = f.shape[0] // 2
    w = n2t * c
    zz = z.reshape(bx, r, FFT_N2 * c)
    out_dtype = F32 if hi else BF16
    return pl.pallas_call(
        functools.partial(_s1_kernel, hi=hi),
        grid=(bx // nb, FFT_N2 // n2t),
        in_specs=[pl.BlockSpec((nb, r, w), lambda p, j: (p, 0, j)),
                  pl.BlockSpec(f.shape, lambda p, j: (0, 0))],
        out_specs=pl.BlockSpec((1, 2, n1, w), lambda p, j: (p, 0, 0, j)),
        out_shape=jax.ShapeDtypeStruct((bx // nb, 2, n1, FFT_N2 * c), out_dtype),
        compiler_params=_cparams(("parallel", "parallel")),
        name="fft_stage1",
    )(zz, f.astype(F32 if hi else BF16))


def _mid_kernel(a_ref, m_ref, mt_ref, kf_ref, o_ref):
    k1t = a_ref.shape[2]
    h = FFT_N2
    for j in range(k1t):
        a = jnp.concatenate([a_ref[0, 0, j], a_ref[0, 1, j]], axis=0)
        x = jnp.dot(m_ref[j], a, preferred_element_type=F32)
        xr, xi = x[:h], x[h:]
        kr, ki = kf_ref[0, 0, j], kf_ref[0, 1, j]
        y = jnp.concatenate([xr * kr - xi * ki, xr * ki + xi * kr], axis=0).astype(BF16)
        b = jnp.dot(mt_ref[j], y, preferred_element_type=F32)
        o_ref[0, 0, j] = b[:h].astype(o_ref.dtype)
        o_ref[0, 1, j] = b[h:].astype(o_ref.dtype)


def _mid_fwd_kernel(a_ref, m_ref, o_ref):
    k1t = a_ref.shape[2]
    h = FFT_N2
    for j in range(k1t):
        a = jnp.concatenate([a_ref[0, 0, j], a_ref[0, 1, j]], axis=0)
        x = jnp.dot(m_ref[j], a, precision=HI, preferred_element_type=F32)
        o_ref[0, 0, j] = x[:h]
        o_ref[0, 1, j] = x[h:]


def _fft_mid(a, m2, m2t, kf, order, k1t=8):
    p, _, n1, wc = a.shape
    c = wc // FFT_N2
    k1t = min(k1t, n1)
    a5 = a.reshape(p, 2, n1, FFT_N2, c)
    blk = (1, 2, k1t, FFT_N2, c)
    out = pl.pallas_call(
        _mid_kernel,
        grid=(n1 // k1t, p),
        in_specs=[pl.BlockSpec(blk, lambda k, q: (q, 0, k, 0, 0)),
                  pl.BlockSpec((k1t, 2 * FFT_N2, 2 * FFT_N2), lambda k, q: (k, 0, 0)),
                  pl.BlockSpec((k1t, 2 * FFT_N2, 2 * FFT_N2), lambda k, q: (k, 0, 0)),
                  pl.BlockSpec(blk, lambda k, q: (order, 0, k, 0, 0))],
        out_specs=pl.BlockSpec(blk, lambda k, q: (q, 0, k, 0, 0)),
        out_shape=jax.ShapeDtypeStruct(a5.shape, BF16),
        compiler_params=_cparams(("parallel", "parallel")),
        name="fft_mid",
    )(a5, m2.astype(BF16), m2t.astype(BF16), kf)
    return out.reshape(a.shape)


def _fft_mid_fwd(a, m2, k1t=8):
    p, _, n1, wc = a.shape
    c = wc // FFT_N2
    k1t = min(k1t, n1)
    a5 = a.reshape(p, 2, n1, FFT_N2, c)
    blk = (1, 2, k1t, FFT_N2, c)
    return pl.pallas_call(
        _mid_fwd_kernel,
        grid=(n1 // k1t, p),
        in_specs=[pl.BlockSpec(blk, lambda k, q: (q, 0, k, 0, 0)),
                  pl.BlockSpec((k1t, 2 * FFT_N2, 2 * FFT_N2), lambda k, q: (k, 0, 0))],
        out_specs=pl.BlockSpec(blk, lambda k, q: (q, 0, k, 0, 0)),
        out_shape=jax.ShapeDtypeStruct(a5.shape, F32),
        compiler_params=_cparams(("parallel", "parallel")),
        name="fft_mid_filter",
    )(a5, m2)


def _s3_kernel(b_ref, g_ref, v_ref, x_ref, sc_ref, d_ref, o_ref):
    bm = jnp.concatenate([b_ref[0, 0], b_ref[0, 1]], axis=0)
    y = jnp.dot(g_ref[...], bm, preferred_element_type=F32)
    r = v_ref.shape[1]
    for part in range(2):
        yp = y[part * r:(part + 1) * r]
        v = v_ref[part].astype(F32)
        x = x_ref[part].astype(F32)
        o_ref[part] = (x * (yp * sc_ref[...] + v * d_ref[...])).astype(o_ref.dtype)


def _fft_s3(bm, g1, v, x, inv_norm, skip, n2t=8):
    p, _, n1, wc = bm.shape
    bx, seq, c = v.shape
    r = seq // FFT_N2
    w = n2t * c
    vv = v.reshape(bx, r, FFT_N2 * c)
    xx = x.reshape(bx, r, FFT_N2 * c)
    sc = jnp.tile(inv_norm.astype(F32).reshape(1, c), (1, n2t))
    dd = jnp.tile(skip.astype(F32).reshape(1, c), (1, n2t))
    io = pl.BlockSpec((2, r, w), lambda q, j: (q, 0, j))
    out = pl.pallas_call(
        _s3_kernel,
        grid=(p, FFT_N2 // n2t),
        in_specs=[pl.BlockSpec((1, 2, n1, w), lambda q, j: (q, 0, 0, j)),
                  pl.BlockSpec(g1.shape, lambda q, j: (0, 0)),
                  io, io,
                  pl.BlockSpec((1, w), lambda q, j: (0, 0)),
                  pl.BlockSpec((1, w), lambda q, j: (0, 0))],
        out_specs=io,
        out_shape=jax.ShapeDtypeStruct(vv.shape, BF16),
        compiler_params=_cparams(("parallel", "parallel")),
        name="fft_stage3",
    )(bm, g1.astype(BF16), vv, xx, sc, dd)
    return out.reshape(bx, seq, c)


def _hyena(h3, conv_w, conv_b, filt, skip):
    seq = h3.shape[1]
    f1, f1_real, g1, m2, m2t = _dft_tables(seq)
    taps, asum = _filters(seq, *filt)
    kf = _fft_mid_fwd(_fft_s1(taps, f1_real, pair=False, hi=True), m2)
    inv_norm = 1.0 / asum[:, 0]
    uc = [_sconv(h3, conv_w, conv_b, seg) for seg in range(3)]
    z = uc[0]
    for o in range(HYENA_ORDER):
        spec = _fft_s1(z, f1, pair=True)
        spec = _fft_mid(spec, m2, m2t, kf, o)
        z = _fft_s3(spec, g1, z, uc[1 + o], inv_norm[o], skip[o])
    return z


def _mix_kernel(x_ref, a_ref, hb_ref, c_ref, g0_ref, g1_ref, g2_ref, wb_ref, wo_ref,
                lig_ref, lib_ref, l1g_ref, l1b_ref, wrh_ref, wrl_ref, br_ref, x1_ref, x1b_ref, rt_ref, cnt_ref):
    acc = None
    for n, (br, gr) in enumerate(((a_ref, g0_ref), (hb_ref, g1_ref), (c_ref, g2_ref))):
        proj = jnp.dot(br[...], wb_ref[n], preferred_element_type=F32)
        term = jax.nn.sigmoid(gr[...].astype(F32)) * proj
        acc = term if acc is None else acc + term
    mix = jnp.dot(acc.astype(BF16), wo_ref[...], preferred_element_type=F32)
    xn = _ln(x_ref[...], lig_ref[...], lib_ref[...])
    x1 = _ln(DN_ALPHA * xn + mix, l1g_ref[...], l1b_ref[...])
    x1_ref[...] = x1
    xh = x1.astype(BF16)
    x1b_ref[...] = xh
    xl = (x1 - xh.astype(F32)).astype(BF16)
    nt = (((1,), (1,)), ((), ()))
    work = lax.dot_general(wrh_ref[...], xh, nt, preferred_element_type=F32)
    work = work + lax.dot_general(wrh_ref[...], xl, nt, preferred_element_type=F32)
    work = work + lax.dot_general(wrl_ref[...], xh, nt, preferred_element_type=F32)
    work = work + br_ref[...]

    tm = work.shape[1]
    erow = lax.broadcasted_iota(jnp.int32, work.shape, 0)
    ids, vals, hots = [], [], []
    for _ in range(TOP_K):
        mx = jnp.max(work, axis=0, keepdims=True)
        idx = jnp.min(jnp.where(work == mx, erow, N_EXPERTS), axis=0, keepdims=True)
        hot = erow == idx
        ids.append(idx)
        vals.append(mx)
        hots.append(hot)
        work = jnp.where(hot, F32(-3e38), work)
    exps = [jnp.exp(v - vals[0]) for v in vals]
    den = exps[0]
    for e in exps[1:]:
        den = den + e
    chosen = hots[0]
    for hot in hots[1:]:
        chosen = chosen | hot
    chosen = chosen.astype(F32)

    @pl.when(pl.program_id(0) == 0)
    def _():
        cnt_ref[...] = jnp.zeros_like(cnt_ref)

    r_i = lax.broadcasted_iota(jnp.int32, (tm, tm), 0)
    c_i = lax.broadcasted_iota(jnp.int32, (tm, tm), 1)
    earlier = (r_i < c_i).astype(BF16)
    before = jnp.dot(chosen.astype(BF16), earlier, preferred_element_type=F32)
    before = before + jnp.tile(cnt_ref[...], (1, tm // LANE))
    cnt_ref[...] += jnp.sum(chosen, axis=1, keepdims=True)
    rows = [i.astype(F32) for i in ids] + [e / den for e in exps]
    rows += [jnp.sum(jnp.where(hot, before, 0.0), axis=0, keepdims=True) for hot in hots]
    rows.append(jnp.zeros((rt_ref.shape[0] - len(rows), tm), F32))
    rt_ref[...] = jnp.concatenate(rows, axis=0)


ROUTE_ROWS = 16


def _mix(x2, a, hb, c, h2, wb, wo, lig, lib, l1g, l1b, wr, br, tm=512):
    T, D = x2.shape
    wrt = wr.T
    wrh = wrt.astype(BF16)
    wrl = (wrt - wrh.astype(F32)).astype(BF16)
    br = br.reshape(N_EXPERTS, 1)
    row = lambda wdt, col: pl.BlockSpec((tm, wdt), lambda i: (i, col))
    const = lambda arr: pl.BlockSpec(arr.shape, lambda i: (0,) * arr.ndim)
    return pl.pallas_call(
        _mix_kernel,
        grid=(T // tm,),
        in_specs=[row(D, 0), row(BRANCH_W, 0), row(BRANCH_W, 0), row(BRANCH_W, 0),
                  row(D, H_G), row(D, H_G + 1), row(D, H_G + 2),
                  const(wb), const(wo), const(lig), const(lib), const(l1g), const(l1b), const(wrh), const(wrl), const(br)],
        out_specs=[row(D, 0), row(D, 0), pl.BlockSpec((ROUTE_ROWS, tm), lambda i: (0, i)),
                   pl.BlockSpec((N_EXPERTS, LANE), lambda i: (0, 0))],
        out_shape=[jax.ShapeDtypeStruct((T, D), F32), jax.ShapeDtypeStruct((T, D), BF16),
                   jax.ShapeDtypeStruct((ROUTE_ROWS, T), F32), jax.ShapeDtypeStruct((N_EXPERTS, LANE), F32)],
        compiler_params=_cparams(("arbitrary",)),
        name="mix",
    )(x2, a, hb, c, h2, h2, h2, wb, wo, lig, lib, l1g, l1b, wrh, wrl, br)


def _ffn_kernel(ce_ref, x_ref, wgu_ref, bgu_ref, wd_ref, bd_ref, o_ref):
    del ce_ref
    gu = jnp.dot(x_ref[...], wgu_ref[0], preferred_element_type=F32) + bgu_ref[0]
    f = gu.shape[1] // 2
    g = jnp.minimum(gu[:, :f], SWIGLU_LIMIT)
    u = jnp.clip(gu[:, f:], -SWIGLU_LIMIT, SWIGLU_LIMIT)
    hmid = (u + 1.0) * (g * jax.nn.sigmoid(g * SWIGLU_ALPHA))
    y = jnp.dot(hmid.astype(BF16), wd_ref[0], preferred_element_type=F32) + bd_ref[0]
    o_ref[...] = y.astype(o_ref.dtype)


def _ffn(chunk_e, x_slots, wgu, bgu, wd, bd):
    P, D = x_slots.shape
    F = wd.shape[1]
    n_chunks = P // MOE_TILE
    wspec = lambda a, b: pl.BlockSpec((1, a, b), lambda i, ce: (ce[i], 0, 0))
    grid_spec = pltpu.PrefetchScalarGridSpec(
        num_scalar_prefetch=1,
        grid=(n_chunks,),
        in_specs=[pl.BlockSpec((MOE_TILE, D), lambda i, ce: (i, 0)),
                  wspec(D, 2 * F), wspec(1, 2 * F), wspec(F, D), wspec(1, D)],
        out_specs=pl.BlockSpec((MOE_TILE, D), lambda i, ce: (i, 0)),
    )
    return pl.pallas_call(
        _ffn_kernel,
        grid_spec=grid_spec,
        out_shape=jax.ShapeDtypeStruct((P, D), BF16),
        compiler_params=_cparams(("arbitrary",)),
        name="moe_ffn",
    )(chunk_e, x_slots, wgu, bgu, wd, bd)


DEINT_W = 256


def _deint_kernel(w_ref, s_ref, o_ref):
    w = w_ref[0].astype(BF16)
    f = w.shape[1] // 2
    half = DEINT_W // 2
    for blk in range(w.shape[1] // DEINT_W):
        r = jnp.dot(w[:, blk * DEINT_W:(blk + 1) * DEINT_W], s_ref[...], preferred_element_type=F32)
        o_ref[0, :, blk * half:(blk + 1) * half] = r[:, :half].astype(o_ref.dtype)
        o_ref[0, :, f + blk * half:f + (blk + 1) * half] = r[:, half:].astype(o_ref.dtype)


def _deinterleave(w, tr=512):
    E, D, F2 = w.shape
    i = jnp.arange(DEINT_W, dtype=jnp.int32)
    src = jnp.where(i < DEINT_W // 2, 2 * i, 2 * (i - DEINT_W // 2) + 1)
    sel = (i[:, None] == src[None, :]).astype(BF16)
    return pl.pallas_call(
        _deint_kernel,
        grid=(E, D // tr),
        in_specs=[pl.BlockSpec((1, tr, F2), lambda e, r: (e, r, 0)),
                  pl.BlockSpec((DEINT_W, DEINT_W), lambda e, r: (0, 0))],
        out_specs=pl.BlockSpec((1, tr, F2), lambda e, r: (e, r, 0)),
        out_shape=jax.ShapeDtypeStruct((E, D, F2), BF16),
        compiler_params=_cparams(("parallel", "parallel")),
        name="deinterleave_gate_up",
    )(w, sel)


def _final_kernel(x1_ref, y_ref, gt_ref, g_ref, b_ref, o_ref):
    gt = gt_ref[...]
    moe = y_ref[0].astype(F32) * gt[:, 0:1]
    for k in range(1, TOP_K):
        moe = moe + y_ref[k].astype(F32) * gt[:, k:k + 1]
    o_ref[...] = _ln(DN_ALPHA * x1_ref[...] + moe, g_ref[...], b_ref[...])


def _final(x1, yg, gates, g, b, tm=512):
    T, D = x1.shape
    return pl.pallas_call(
        _final_kernel,
        grid=(T // tm,),
        in_specs=[pl.BlockSpec((tm, D), lambda i: (i, 0)),
                  pl.BlockSpec((TOP_K, tm, D), lambda i: (0, i, 0)),
                  pl.BlockSpec((tm, TOP_K), lambda i: (i, 0)),
                  pl.BlockSpec((1, D), lambda i: (0, 0)),
                  pl.BlockSpec((1, D), lambda i: (0, 0))],
        out_specs=pl.BlockSpec((tm, D), lambda i: (i, 0)),
        out_shape=jax.ShapeDtypeStruct((T, D), F32),
        compiler_params=_cparams(("parallel",)),
        name="final_ln",
    )(x1, yg, gates, g, b)


def _moe(x1, x1b, route, cnt, ffn_w, ln2_g, ln2_b):
    T, D = x1.shape
    A = T * TOP_K
    i32 = jnp.int32
    ids = route[:TOP_K].astype(i32)
    gates = route[TOP_K:2 * TOP_K]
    rank = route[2 * TOP_K:3 * TOP_K].astype(i32)
    counts = cnt[:, 0].astype(i32)
    pcounts = (counts + MOE_TILE - 1) // MOE_TILE * MOE_TILE
    pends = jnp.cumsum(pcounts)
    pstarts = pends - pcounts
    n_chunks = -(-A // MOE_TILE) + N_EXPERTS
    P = n_chunks * MOE_TILE
    e3 = jnp.arange(N_EXPERTS, dtype=i32)[:, None, None]
    pos = rank + jnp.sum(jnp.where(ids[None] == e3, pstarts[:, None, None], 0), axis=0)
    padc = jnp.concatenate([pcounts - counts, (P - pends[-1])[None]])
    pad_lo = jnp.concatenate([pstarts + counts, pends[-1:]])
    cum = jnp.cumsum(padc)
    j = jnp.arange(P - A, dtype=i32)
    run = jnp.sum((j[None, :] >= cum[:, None]).astype(i32), axis=0)
    hot = run[None, :] == jnp.arange(N_EXPERTS + 1, dtype=i32)[:, None]
    pad_key = j + jnp.sum(jnp.where(hot, (pad_lo - (cum - padc))[:, None], 0), axis=0)
    keys = jnp.concatenate([pos.reshape(A), pad_key])
    toks = jnp.concatenate([jnp.tile(jnp.arange(T, dtype=i32), TOP_K), jnp.zeros((P - A,), i32)])
    _, slot_tok = lax.sort((keys, toks), num_keys=1)
    chunk_lo = jnp.arange(n_chunks, dtype=i32) * MOE_TILE
    chunk_e = jnp.minimum(jnp.sum((chunk_lo[None, :] >= pends[:, None]).astype(i32), axis=0), N_EXPERTS - 1)
    x_slots = x1b[slot_tok]
    y_slots = _ffn(chunk_e, x_slots, *ffn_w)
    yg = y_slots[pos.reshape(A)].reshape(TOP_K, T, D)
    return _final(x1, yg, gates.T, ln2_g, ln2_b)


def _mixer_half(x, mem, p):
    B, L, D = x.shape
    x2 = x.reshape(B * L, D)
    h2 = _inproj(x2, p["ln_in_g"], p["ln_in_b"], p["w_in"])
    h3 = h2.reshape(B, L, IN_W)
    a = _wattn(h3, p["sink"], p["bias"])
    hb = _hyena(h3, p["conv_w"], p["conv_b"], p["filt"], p["skip"])
    c = _memattn(h3, mem, p["w_mem_kv"])
    return _mix(x2, a.reshape(B * L, -1), hb.reshape(B * L, -1), c.reshape(B * L, -1), h2,
                p["w_branch"], p["w_out"], p["ln_in_g"], p["ln_in_b"], p["ln1_g"], p["ln1_b"],
                p["w_router"], p["b_router"])


def _trunk(x, mem, p):
    return _moe(*_mixer_half(x, mem, p), p["ffn"], p["ln2_g"], p["ln2_b"]).reshape(x.shape)


def _prep(ln_in_g, ln_in_b, rel_bias, w_in, attn_sink, conv_w, conv_b, filt_w1, filt_b1, filt_w2, filt_b2,
          filt_w3, filt_b3, filt_freq, hyena_skip, w_mem_kv, w_branch, w_out, ln1_g, ln1_b, w_router, b_router,
          w_gate_up, b_gate_up, w_down, b_down, ln2_g, ln2_b):
    w = w_in
    g_lo = Q_W + 2 * KV_W + 3 * BRANCH_W + BRANCH_W
    hy_lo = Q_W + 2 * KV_W
    assert HEAD_DIM ** -0.5 == 2.0 ** -3
    w_p = jnp.concatenate([w[:, g_lo:], w[:, :Q_W] * (HEAD_DIM ** -0.5), w[:, hy_lo + 3 * BRANCH_W:g_lo],
                           w[:, hy_lo:hy_lo + 3 * BRANCH_W], w[:, Q_W:hy_lo]], axis=1).astype(BF16)
    row = lambda v: v.astype(F32)[None]
    wr = w_router.astype(F32)
    br = b_router.astype(F32)
    return dict(
        ln_in_g=row(ln_in_g), ln_in_b=row(ln_in_b), w_in=w_p,
        sink=attn_sink.astype(F32), bias=_bias_table(rel_bias),
        conv_w=conv_w.astype(F32), conv_b=row(conv_b),
        filt=(filt_w1, filt_b1, filt_w2, filt_b2, filt_w3, filt_b3, filt_freq), skip=hyena_skip,
        w_mem_kv=w_mem_kv.astype(BF16), w_branch=w_branch.astype(BF16), w_out=w_out.astype(BF16),
        ln1_g=row(ln1_g), ln1_b=row(ln1_b), w_router=wr, b_router=br,
        ffn=(_deinterleave(w_gate_up),
             jnp.concatenate([b_gate_up[:, 0::2], b_gate_up[:, 1::2]], axis=1)[:, None, :].astype(F32),
             w_down.astype(BF16), b_down[:, None, :].astype(F32)),
        ln2_g=row(ln2_g), ln2_b=row(ln2_b),
    )


def kernel(x_prompt, x_sample, mem_prompt, mem_sample, ln_in_g, ln_in_b, rel_bias, w_in, attn_sink, conv_w, conv_b, filt_w1, filt_b1, filt_w2, filt_b2, filt_w3, filt_b3, filt_freq, hyena_skip, w_mem_kv, w_branch, w_out, ln1_g, ln1_b, w_router, b_router, w_gate_up, b_gate_up, w_down, b_down, ln2_g, ln2_b):
    p = _prep(ln_in_g, ln_in_b, rel_bias, w_in[0], attn_sink[0], conv_w[0], conv_b[0], filt_w1[0], filt_b1[0],
              filt_w2[0], filt_b2[0], filt_w3[0], filt_b3[0], filt_freq[0], hyena_skip[0], w_mem_kv[0],
              w_branch[0], w_out[0], ln1_g[0], ln1_b[0], w_router[0], b_router[0], w_gate_up[0], b_gate_up[0],
              w_down[0], b_down[0], ln2_g[0], ln2_b[0])
    halves = [_mixer_half(x, mem, p) for x, mem in ((x_prompt, mem_prompt), (x_sample, mem_sample))]
    outs = [_moe(*half, p["ffn"], p["ln2_g"], p["ln2_b"]) for half in halves]
    return (outs[0].reshape(x_prompt.shape), outs[1].reshape(x_sample.shape))
```

```python
import functools
import math

import jax
import jax.numpy as jnp
from jax import lax
from jax.experimental import pallas as pl
from jax.experimental.pallas import tpu as pltpu

F32 = jnp.float32
BF16 = jnp.bfloat16
HI = lax.Precision.HIGHEST

D_MODEL = 1024
BRANCH_W = 512
N_Q_HEADS = 8
N_KV_HEADS = 2
HEAD_DIM = 64
WINDOW = 128
BLOCK = 128
N_BUCKETS = 32
MAX_DISTANCE = 128
HYENA_ORDER = 2
SHORT_CONV = 3
POS_EMB_DIM = 33
FILTER_HIDDEN = 64
DECAY_TARGET = 1e-2
FAST_DECAY_PCT = 0.3
SLOW_DECAY_PCT = 1.5
N_MEM_HEADS = 4
MEM_HEAD_DIM = BRANCH_W // N_MEM_HEADS
N_BRANCHES = 3
N_EXPERTS = 32
TOP_K = 4
D_EXPERT = 1024
SWIGLU_LIMIT = 7.0
SWIGLU_ALPHA = 1.702
LN_EPS = 1e-5
DEPTH = 1
DN_ALPHA = (2 * DEPTH) ** 0.25

Q_W = N_Q_HEADS * HEAD_DIM
KV_W = N_KV_HEADS * HEAD_DIM
IN_W = Q_W + 2 * KV_W + 3 * BRANCH_W + BRANCH_W + N_BRANCHES * D_MODEL

H_G = 0
H_Q = 6
H_QM = 7
H_HY = 16
H_K = 44
H_V = 45

FFT_N2 = 128
MOE_TILE = 512
LANE = 128
VMEM_LIMIT = 52 * 1024 * 1024


def _cparams(sem):
    return pltpu.CompilerParams(dimension_semantics=sem, vmem_limit_bytes=VMEM_LIMIT)


def _ln(x, g, b):
    mu = jnp.mean(x, axis=-1, keepdims=True)
    xc = x - mu
    var = jnp.mean(xc * xc, axis=-1, keepdims=True)
    return xc * lax.rsqrt(var + LN_EPS) * g + b


def _inproj_kernel(x_ref, g_ref, b_ref, w_ref, o_ref):
    xn = _ln(x_ref[...], g_ref[...], b_ref[...])
    o_ref[...] = jnp.dot(xn.astype(BF16), w_ref[...], preferred_element_type=F32).astype(o_ref.dtype)


def _inproj(x2, ln_g, ln_b, w_p, tm=512):
    T, D = x2.shape
    N = w_p.shape[1]
    tn = N // 2
    return pl.pallas_call(
        _inproj_kernel,
        grid=(N // tn, T // tm),
        in_specs=[
            pl.BlockSpec((tm, D), lambda j, i: (i, 0)),
            pl.BlockSpec((1, D), lambda j, i: (0, 0)),
            pl.BlockSpec((1, D), lambda j, i: (0, 0)),
            pl.BlockSpec((D, tn), lambda j, i: (0, j)),
        ],
        out_specs=pl.BlockSpec((tm, tn), lambda j, i: (i, j)),
        out_shape=jax.ShapeDtypeStruct((T, N), BF16),
        compiler_params=_cparams(("parallel", "parallel")),
        name="inproj",
    )(x2, ln_g, ln_b, w_p)


def _wattn_kernel(sink_ref, q_ref, kp_ref, kc_ref, kn_ref, vp_ref, vc_ref, vn_ref, bias_ref, o_ref, *, seq):
    n = pl.program_id(1)
    q = q_ref[0]
    kcat = jnp.concatenate([kp_ref[0], kc_ref[0], kn_ref[0]], axis=0)
    vcat = jnp.concatenate([vp_ref[0], vc_ref[0], vn_ref[0]], axis=0)
    qi = lax.broadcasted_iota(jnp.int32, (BLOCK, 3 * BLOCK), 0)
    kj = lax.broadcasted_iota(jnp.int32, (BLOCK, 3 * BLOCK), 1)
    rel = kj - BLOCK - qi
    kpos = (n - 1) * BLOCK + kj
    valid = (jnp.abs(rel) <= WINDOW) & (kpos >= 0) & (kpos < seq)
    group = N_Q_HEADS // N_KV_HEADS
    scores = []
    for h in range(N_Q_HEADS):
        g = h // group
        qh = q[:, h * HEAD_DIM:(h + 1) * HEAD_DIM]
        kh = kcat[:, g * HEAD_DIM:(g + 1) * HEAD_DIM]
        scores.append(lax.dot_general(qh, kh, (((1,), (1,)), ((), ())), preferred_element_type=F32))
    probs, denoms = [], []
    for h in range(N_Q_HEADS):
        s = jnp.where(valid, scores[h] + bias_ref[h], F32(-1e30))
        sk = sink_ref[h]
        m = jnp.maximum(jnp.max(s, axis=-1, keepdims=True), sk)
        p = jnp.exp(s - m)
        denoms.append(jnp.sum(p, axis=-1, keepdims=True) + jnp.exp(sk - m))
        probs.append(p.astype(BF16))
    outs = []
    for h in range(N_Q_HEADS):
        g = h // group
        vh = vcat[:, g * HEAD_DIM:(g + 1) * HEAD_DIM]
        outs.append(jnp.dot(probs[h], vh, preferred_element_type=F32) / denoms[h])
    o_ref[0] = jnp.concatenate(outs, axis=1).astype(o_ref.dtype)


def _wattn(h3, sink, bias):
    B, L, _ = h3.shape
    nb = L // BLOCK
    kv_spec = lambda col, shift: pl.BlockSpec(
        (1, BLOCK, KV_W), lambda b, n: (b, jnp.clip(n + shift, 0, nb - 1), col))
    return pl.pallas_call(
        functools.partial(_wattn_kernel, seq=L),
        grid=(B, nb),
        in_specs=[
            pl.BlockSpec(memory_space=pltpu.SMEM),
            pl.BlockSpec((1, BLOCK, Q_W), lambda b, n: (b, n, H_Q)),
            kv_spec(H_K, -1), kv_spec(H_K, 0), kv_spec(H_K, 1),
            kv_spec(H_V, -1), kv_spec(H_V, 0), kv_spec(H_V, 1),
            pl.BlockSpec((N_Q_HEADS, BLOCK, 3 * BLOCK), lambda b, n: (0, 0, 0)),
        ],
        out_specs=pl.BlockSpec((1, BLOCK, Q_W), lambda b, n: (b, n, 0)),
        out_shape=jax.ShapeDtypeStruct((B, L, Q_W), BF16),
        compiler_params=_cparams(("parallel", "parallel")),
        name="wattn",
    )(sink, h3, h3, h3, h3, h3, h3, h3, bias)


def _t5_bucket(rel):
    nb = N_BUCKETS // 2
    max_exact = nb // 2
    ret = jnp.where(rel > 0, nb, 0)
    n = jnp.abs(rel)
    nf = jnp.maximum(n, 1).astype(F32)
    large = max_exact + (jnp.log(nf / max_exact) / math.log(MAX_DISTANCE / max_exact)
                         * (nb - max_exact)).astype(jnp.int32)
    large = jnp.minimum(large, nb - 1)
    return ret + jnp.where(n < max_exact, n, large)


def _bias_table(rel_bias):
    qi = jnp.arange(BLOCK, dtype=jnp.int32)[:, None]
    kj = jnp.arange(3 * BLOCK, dtype=jnp.int32)[None, :]
    bias = rel_bias[_t5_bucket(kj - BLOCK - qi)].astype(F32)
    return jnp.transpose(bias, (2, 0, 1))


def _memattn_kernel(q_ref, mem_ref, w_ref, o_ref, kv_ref):
    @pl.when(pl.program_id(1) == 0)
    def _():
        kv_ref[...] = jnp.dot(mem_ref[0].astype(BF16), w_ref[...],
                              preferred_element_type=F32).astype(kv_ref.dtype)

    q = q_ref[0]
    outs = []
    for h in range(N_MEM_HEADS):
        lo = h * MEM_HEAD_DIM
        qh = q[:, lo:lo + MEM_HEAD_DIM]
        kh = kv_ref[:, lo:lo + MEM_HEAD_DIM]
        vh = kv_ref[:, BRANCH_W + lo:BRANCH_W + lo + MEM_HEAD_DIM]
        s = lax.dot_general(qh, kh, (((1,), (1,)), ((), ())), preferred_element_type=F32) * (MEM_HEAD_DIM ** -0.5)
        m = jnp.max(s, axis=-1, keepdims=True)
        p = jnp.exp(s - m)
        denom = jnp.sum(p, axis=-1, keepdims=True)
        outs.append(jnp.dot(p.astype(BF16), vh, preferred_element_type=F32) / denom)
    o_ref[0] = jnp.concatenate(outs, axis=1).astype(o_ref.dtype)


def _memattn(h3, mem, w_kv, tm=512):
    B, L, _ = h3.shape
    M = mem.shape[1]
    tm = min(tm, L)
    return pl.pallas_call(
        _memattn_kernel,
        grid=(B, L // tm),
        in_specs=[
            pl.BlockSpec((1, tm, BRANCH_W), lambda b, l: (b, l, H_QM)),
            pl.BlockSpec((1, M, D_MODEL), lambda b, l: (b, 0, 0)),
            pl.BlockSpec((D_MODEL, 2 * BRANCH_W), lambda b, l: (0, 0)),
        ],
        out_specs=pl.BlockSpec((1, tm, BRANCH_W), lambda b, l: (b, l, 0)),
        out_shape=jax.ShapeDtypeStruct((B, L, BRANCH_W), BF16),
        scratch_shapes=[pltpu.VMEM((M, 2 * BRANCH_W), BF16)],
        compiler_params=_cparams(("parallel", "arbitrary")),
        name="memattn",
    )(h3, mem, w_kv)


HALO = 16


def _sconv_kernel(cur_ref, prev_ref, next_ref, w_ref, b_ref, o_ref, *, nt):
    t = pl.program_id(2)
    u = cur_ref[0].astype(F32)
    tl = u.shape[0]
    pr = prev_ref[0].astype(F32)[HALO - 1:HALO]
    nx = next_ref[0].astype(F32)[0:1]
    pr = jnp.where(t == 0, 0.0, pr)
    nx = jnp.where(t == nt - 1, 0.0, nx)
    row = lax.broadcasted_iota(jnp.int32, u.shape, 0)
    up = jnp.where(row == 0, pr, pltpu.roll(u, 1, 0))
    un = jnp.where(row == tl - 1, nx, pltpu.roll(u, tl - 1, 0))
    w = w_ref[...]
    o_ref[0] = (b_ref[...] + up * w[0:1] + u * w[1:2] + un * w[2:3]).astype(o_ref.dtype)


def _sconv(h3, conv_w, conv_b, seg, tl=1024, cb=256):
    B, L, _ = h3.shape
    tl = min(tl, L)
    nt = L // tl
    per = BRANCH_W // cb
    col = seg * per
    return pl.pallas_call(
        functools.partial(_sconv_kernel, nt=nt),
        grid=(B, per, nt),
        in_specs=[
            pl.BlockSpec((1, tl, cb), lambda b, j, t: (b, t, H_HY + col + j)),
            pl.BlockSpec((1, HALO, cb), lambda b, j, t: (b, jnp.maximum(t * (tl // HALO) - 1, 0), H_HY + col + j)),
            pl.BlockSpec((1, HALO, cb),
                         lambda b, j, t: (b, jnp.minimum((t + 1) * (tl // HALO), L // HALO - 1), H_HY + col + j)),
            pl.BlockSpec((SHORT_CONV, cb), lambda b, j, t: (0, col + j)),
            pl.BlockSpec((1, cb), lambda b, j, t: (0, col + j)),
        ],
        out_specs=pl.BlockSpec((1, tl, cb), lambda b, j, t: (b, t, j)),
        out_shape=jax.ShapeDtypeStruct((B, L, BRANCH_W), BF16),
        compiler_params=_cparams(("parallel", "parallel", "parallel")),
        name="sconv",
    )(h3, h3, h3, conv_w, conv_b)


def _filt_kernel(w1_ref, b1_ref, w2_ref, b2_ref, w3_ref, b3_ref, fr_ref, fq_ref, dl_ref, k_ref, s_ref, *, seq, tr):
    i = pl.program_id(0)
    m = i * tr + lax.broadcasted_iota(jnp.int32, (tr, LANE), 0)
    lane = lax.broadcasted_iota(jnp.int32, (tr, LANE), 1)
    p = jnp.where(m < seq, m, 2 * seq - m).astype(F32)
    t = p / F32(seq - 1)
    w = F32(2.0 * math.pi) * p / F32(seq)
    arg = fq_ref[...] * w
    bands = (POS_EMB_DIM - 1) // 2
    z = jnp.where(lane == 0, t,
                  jnp.where(lane <= bands, jnp.cos(arg),
                            jnp.where(lane <= 2 * bands, -jnp.sin(arg), 0.0)))
    fr = fr_ref[...]
    h = jnp.sin(fr[0:1] * (jnp.dot(z, w1_ref[...], precision=HI, preferred_element_type=F32) + b1_ref[...]))
    h = jnp.sin(fr[1:2] * (jnp.dot(h, w2_ref[...], precision=HI, preferred_element_type=F32) + b2_ref[...]))
    h = jnp.dot(h, w3_ref[...], precision=HI, preferred_element_type=F32) + b3_ref[...]
    mc = m[:, 0:1]
    tc = t[:, 0:1]
    decay = jnp.exp(-tc * dl_ref[...])
    oc = HYENA_ORDER * BRANCH_W

    @pl.when(i == 0)
    def _():
        s_ref[...] = jnp.zeros_like(s_ref)

    for o in range(HYENA_ORDER):
        fwd = h[:, o * BRANCH_W:(o + 1) * BRANCH_W]
        bwd = h[:, oc + o * BRANCH_W:oc + (o + 1) * BRANCH_W]
        k = jnp.where(mc < seq, fwd, bwd) * decay
        k = jnp.where(mc == seq, 0.0, k)
        k_ref[o] = k
        s_ref[o] += jnp.sum(jnp.abs(k), axis=0, keepdims=True)


def _filters(seq, w1, b1, w2, b2, w3, b3, freq, tr=256):
    n = 2 * seq
    tr = min(tr, n)
    bands = (POS_EMB_DIM - 1) // 2
    hid = LANE
    w1p = jnp.zeros((LANE, hid), F32).at[:POS_EMB_DIM, :FILTER_HIDDEN].set(w1.astype(F32))
    b1p = jnp.zeros((1, hid), F32).at[0, :FILTER_HIDDEN].set(b1.astype(F32))
    w2p = jnp.zeros((hid, hid), F32).at[:FILTER_HIDDEN, :FILTER_HIDDEN].set(w2.astype(F32))
    b2p = jnp.zeros((1, hid), F32).at[0, :FILTER_HIDDEN].set(b2.astype(F32))
    w3p = jnp.zeros((hid, w3.shape[1]), F32).at[:FILTER_HIDDEN].set(w3.astype(F32))
    b3p = b3.astype(F32)[None]
    frp = jnp.zeros((2, hid), F32).at[:, :FILTER_HIDDEN].set(freq.astype(F32))
    f = jnp.linspace(1e-4, bands - 1, bands, dtype=F32)
    fq = jnp.zeros((1, LANE), F32).at[0, 1:1 + bands].set(f).at[0, 1 + bands:1 + 2 * bands].set(f)
    deltas = jnp.abs(jnp.linspace(math.log(DECAY_TARGET) / FAST_DECAY_PCT,
                                  math.log(DECAY_TARGET) / SLOW_DECAY_PCT, BRANCH_W, dtype=F32))[None]
    full = lambda a: pl.BlockSpec(a.shape, lambda i: (0,) * a.ndim)
    args = (w1p, b1p, w2p, b2p, w3p, b3p, frp, fq, deltas)
    return pl.pallas_call(
        functools.partial(_filt_kernel, seq=seq, tr=tr),
        grid=(n // tr,),
        in_specs=[full(a) for a in args],
        out_specs=[pl.BlockSpec((HYENA_ORDER, tr, BRANCH_W), lambda i: (0, i, 0)),
                   pl.BlockSpec((HYENA_ORDER, 1, BRANCH_W), lambda i: (0, 0, 0))],
        out_shape=[jax.ShapeDtypeStruct((HYENA_ORDER, n, BRANCH_W), F32),
                   jax.ShapeDtypeStruct((HYENA_ORDER, 1, BRANCH_W), F32)],
        compiler_params=_cparams(("arbitrary",)),
        name="hyena_filter",
    )(*args)


def _dft_tables(seq):
    n1 = 2 * seq // FFT_N2
    n1h = seq // FFT_N2
    n = 2 * seq
    k1 = jnp.arange(n1, dtype=jnp.int32)
    th = (2.0 * math.pi / n1) * ((k1[:, None] * k1[None, :]) % n1).astype(F32)
    c1, s1 = jnp.cos(th), jnp.sin(th)
    ch, sh = c1[:, :n1h], s1[:, :n1h]
    f1 = jnp.concatenate([jnp.concatenate([ch, sh], 1), jnp.concatenate([-sh, ch], 1)], 0)
    f1_real = jnp.concatenate([c1, -s1], 0)
    cht, sht = ch.T / n, sh.T / n
    g1 = jnp.concatenate([jnp.concatenate([cht, -sht], 1), jnp.concatenate([sht, cht], 1)], 0)
    n2 = jnp.arange(FFT_N2, dtype=jnp.int32)
    kk = k1[:, None, None] + n1 * n2[None, :, None]
    ph = (2.0 * math.pi / n) * ((kk * n2[None, None, :]) % n).astype(F32)
    c2, s2 = jnp.cos(ph), jnp.sin(ph)
    m2 = jnp.concatenate([jnp.concatenate([c2, s2], 2), jnp.concatenate([-s2, c2], 2)], 1)
    return f1, f1_real, g1, m2, jnp.swapaxes(m2, 1, 2)


def _s1_kernel(z_ref, f_ref, o_ref, *, hi):
    nb, r, w = z_ref.shape
    z = z_ref[...].reshape(nb * r, w) if nb > 1 else z_ref[0]
    if hi:
        res = jnp.dot(f_ref[...], z, precision=HI, preferred_element_type=F32)
    else:
        res = jnp.dot(f_ref[...], z.astype(BF16), preferred_element_type=F32)
    half = res.shape[0] // 2
    o_ref[0, 0] = res[:half].astype(o_ref.dtype)
    o_ref[0, 1] = res[half:].astype(o_ref.dtype)


def _fft_s1(z, f, *, pair, hi=False, n2t=8):
    bx, lz, c = z.shape
    r = lz // FFT_N2
    nb = 2 if pair else 1
    n1 = f.shape[0] // 2
    w = n2t * c
    zz = z.reshape(bx, r, FFT_N2 * c)
    out_dtype = F32 if hi else BF16
    return pl.pallas_call(
        functools.partial(_s1_kernel, hi=hi),
        grid=(bx // nb, FFT_N2 // n2t),
        in_specs=[pl.BlockSpec((nb, r, w), lambda p, j: (p, 0, j)),
                  pl.BlockSpec(f.shape, lambda p, j: (0, 0))],
        out_specs=pl.BlockSpec((1, 2, n1, w), lambda p, j: (p, 0, 0, j)),
        out_shape=jax.ShapeDtypeStruct((bx // nb, 2, n1, FFT_N2 * c), out_dtype),
        compiler_params=_cparams(("parallel", "parallel")),
        name="fft_stage1",
    )(zz, f.astype(F32 if hi else BF16))


def _mid_kernel(a_ref, m_ref, mt_ref, kf_ref, o_ref):
    k1t = a_ref.shape[2]
    h = FFT_N2
    for j in range(k1t):
        a = jnp.concatenate([a_ref[0, 0, j], a_ref[0, 1, j]], axis=0)
        x = jnp.dot(m_ref[j], a, preferred_element_type=F32)
        xr, xi = x[:h], x[h:]
        kr, ki = kf_ref[0, 0, j], kf_ref[0, 1, j]
        y = jnp.concatenate([xr * kr - xi * ki, xr * ki + xi * kr], axis=0).astype(BF16)
        b = jnp.dot(mt_ref[j], y, preferred_element_type=F32)
        o_ref[0, 0, j] = b[:h].astype(o_ref.dtype)
        o_ref[0, 1, j] = b[h:].astype(o_ref.dtype)


def _mid_fwd_kernel(a_ref, m_ref, o_ref):
    k1t = a_ref.shape[2]
    h = FFT_N2
    for j in range(k1t):
        a = jnp.concatenate([a_ref[0, 0, j], a_ref[0, 1, j]], axis=0)
        x = jnp.dot(m_ref[j], a, precision=HI, preferred_element_type=F32)
        o_ref[0, 0, j] = x[:h]
        o_ref[0, 1, j] = x[h:]


def _fft_mid(a, m2, m2t, kf, order, k1t=8):
    p, _, n1, wc = a.shape
    c = wc // FFT_N2
    k1t = min(k1t, n1)
    a5 = a.reshape(p, 2, n1, FFT_N2, c)
    blk = (1, 2, k1t, FFT_N2, c)
    out = pl.pallas_call(
        _mid_kernel,
        grid=(n1 // k1t, p),
        in_specs=[pl.BlockSpec(blk, lambda k, q: (q, 0, k, 0, 0)),
                  pl.BlockSpec((k1t, 2 * FFT_N2, 2 * FFT_N2), lambda k, q: (k, 0, 0)),
                  pl.BlockSpec((k1t, 2 * FFT_N2, 2 * FFT_N2), lambda k, q: (k, 0, 0)),
                  pl.BlockSpec(blk, lambda k, q: (order, 0, k, 0, 0))],
        out_specs=pl.BlockSpec(blk, lambda k, q: (q, 0, k, 0, 0)),
        out_shape=jax.ShapeDtypeStruct(a5.shape, BF16),
        compiler_params=_cparams(("parallel", "parallel")),
        name="fft_mid",
    )(a5, m2.astype(BF16), m2t.astype(BF16), kf)
    return out.reshape(a.shape)


def _fft_mid_fwd(a, m2, k1t=8):
    p, _, n1, wc = a.shape
    c = wc // FFT_N2
    k1t = min(k1t, n1)
    a5 = a.reshape(p, 2, n1, FFT_N2, c)
    blk = (1, 2, k1t, FFT_N2, c)
    return pl.pallas_call(
        _mid_fwd_kernel,
        grid=(n1 // k1t, p),
        in_specs=[pl.BlockSpec(blk, lambda k, q: (q, 0, k, 0, 0)),
                  pl.BlockSpec((k1t, 2 * FFT_N2, 2 * FFT_N2), lambda k, q: (k, 0, 0))],
        out_specs=pl.BlockSpec(blk, lambda k, q: (q, 0, k, 0, 0)),
        out_shape=jax.ShapeDtypeStruct(a5.shape, F32),
        compiler_params=_cparams(("parallel", "parallel")),
        name="fft_mid_filter",
    )(a5, m2)


def _s3_kernel(b_ref, g_ref, v_ref, x_ref, sc_ref, d_ref, o_ref):
    bm = jnp.concatenate([b_ref[0, 0], b_ref[0, 1]], axis=0)
    y = jnp.dot(g_ref[...], bm, preferred_element_type=F32)
    r = v_ref.shape[1]
    for part in range(2):
        yp = y[part * r:(part + 1) * r]
        v = v_ref[part].astype(F32)
        x = x_ref[part].astype(F32)
        o_ref[part] = (x * (yp * sc_ref[...] + v * d_ref[...])).astype(o_ref.dtype)


def _fft_s3(bm, g1, v, x, inv_norm, skip, n2t=8):
    p, _, n1, wc = bm.shape
    bx, seq, c = v.shape
    r = seq // FFT_N2
    w = n2t * c
    vv = v.reshape(bx, r, FFT_N2 * c)
    xx = x.reshape(bx, r, FFT_N2 * c)
    sc = jnp.tile(inv_norm.astype(F32).reshape(1, c), (1, n2t))
    dd = jnp.tile(skip.astype(F32).reshape(1, c), (1, n2t))
    io = pl.BlockSpec((2, r, w), lambda q, j: (q, 0, j))
    out = pl.pallas_call(
        _s3_kernel,
        grid=(p, FFT_N2 // n2t),
        in_specs=[pl.BlockSpec((1, 2, n1, w), lambda q, j: (q, 0, 0, j)),
                  pl.BlockSpec(g1.shape, lambda q, j: (0, 0)),
                  io, io,
                  pl.BlockSpec((1, w), lambda q, j: (0, 0)),
                  pl.BlockSpec((1, w), lambda q, j: (0, 0))],
        out_specs=io,
        out_shape=jax.ShapeDtypeStruct(vv.shape, BF16),
        compiler_params=_cparams(("parallel", "parallel")),
        name="fft_stage3",
    )(bm, g1.astype(BF16), vv, xx, sc, dd)
    return out.reshape(bx, seq, c)


ROW_PAD = 8
LC_GROUP = 4


def _pack2(hi, lo):
    hb = lax.bitcast_convert_type(hi.astype(BF16).astype(F32), jnp.uint32)
    lb = lax.bitcast_convert_type(lo.astype(BF16).astype(F32), jnp.uint32)
    return hb | (lb >> 16)


def _unpack2(w):
    return (lax.bitcast_convert_type(w & jnp.uint32(0xFFFF0000), F32),
            lax.bitcast_convert_type(w << 16, F32))


def _lc_kernel(v_ref, x_ref, f1_ref, g1_ref, m2_ref, m2t_ref, kf_ref, sc_ref, d_ref, o_ref, zst, a3,
               *, n1, n1h, k1t, nk):
    k = pl.program_id(1)
    cb = v_ref.shape[2]
    zp = FFT_N2 + ROW_PAD
    ap = n1 + ROW_PAD
    h = FFT_N2

    @pl.when(k == 0)
    def _():
        def stage(i, c):
            r0 = pl.multiple_of(i * h, h)
            w = _pack2(v_ref[0, pl.ds(r0, h), :].astype(F32), v_ref[1, pl.ds(r0, h), :].astype(F32))
            zst[pl.ds(pl.multiple_of(i * zp, 8), h), :] = w
            return c
        lax.fori_loop(0, n1h, stage, 0)

        def level1(t, c):
            zs = []
            for u in range(LC_GROUP):
                zr, zi = _unpack2(zst[pl.ds(t * LC_GROUP + u, n1h, stride=zp), :])
                zs.append(jnp.concatenate([zr, zi], axis=0).astype(BF16))
            res = jnp.dot(f1_ref[...], jnp.concatenate(zs, axis=1), preferred_element_type=F32)
            for u in range(LC_GROUP):
                r0 = pl.multiple_of((t * LC_GROUP + u) * ap, 8)
                a3[pl.ds(r0, n1), :] = _pack2(res[:n1, u * cb:(u + 1) * cb], res[n1:, u * cb:(u + 1) * cb])
            return c
        lax.fori_loop(0, h // LC_GROUP, level1, 0, unroll=2)

    for jj in range(k1t):
        k1 = k * k1t + jj
        ar, ai = _unpack2(a3[pl.ds(k1, h, stride=ap), :])
        a = jnp.concatenate([ar, ai], axis=0).astype(BF16)
        x = jnp.dot(m2_ref[jj], a, preferred_element_type=F32)
        xr, xi = x[:h], x[h:]
        kr, ki = kf_ref[0, 0, jj], kf_ref[0, 1, jj]
        y = jnp.concatenate([xr * kr - xi * ki, xr * ki + xi * kr], axis=0).astype(BF16)
        b = jnp.dot(m2t_ref[jj], y, preferred_element_type=F32)
        a3[pl.ds(k1, h, stride=ap), :] = _pack2(b[:h], b[h:])

    @pl.when(k == nk - 1)
    def _():
        def level3(t, c):
            bms = []
            for u in range(LC_GROUP):
                r0 = pl.multiple_of((t * LC_GROUP + u) * ap, 8)
                br, bi = _unpack2(a3[pl.ds(r0, n1), :])
                bms.append(jnp.concatenate([br, bi], axis=0).astype(BF16))
            y = jnp.dot(g1_ref[...], jnp.concatenate(bms, axis=1), preferred_element_type=F32)
            for u in range(LC_GROUP):
                r0 = pl.multiple_of((t * LC_GROUP + u) * ap, 8)
                a3[pl.ds(r0, n1), :] = lax.bitcast_convert_type(y[:, u * cb:(u + 1) * cb], jnp.uint32)
            return c
        lax.fori_loop(0, h // LC_GROUP, level3, 0, unroll=2)

        def gate(i, c):
            r0 = pl.multiple_of(i * h, h)
            vs = _unpack2(zst[pl.ds(pl.multiple_of(i * zp, 8), h), :])
            for part in range(2):
                yv = lax.bitcast_convert_type(a3[pl.ds(part * n1h + i, h, stride=ap), :], F32)
                xx = x_ref[part, pl.ds(r0, h), :].astype(F32)
                o_ref[part, pl.ds(r0, h), :] = (xx * (yv * sc_ref[...] + vs[part] * d_ref[...])).astype(o_ref.dtype)
            return c
        lax.fori_loop(0, n1h, gate, 0)


def _longconv(v, x, kf, order, inv_norm, skip, tables, k1t=8, cb=LANE):
    f1, g1, m2, m2t = tables
    bx, seq, c = v.shape
    n1 = 2 * seq // FFT_N2
    n1h = seq // FFT_N2
    k1t = min(k1t, n1)
    nk = n1 // k1t
    ncb = c // cb
    io = pl.BlockSpec((2, seq, cb), lambda g, k: (g // ncb, 0, g % ncb))
    vec = pl.BlockSpec((1, cb), lambda g, k: (0, g % ncb))
    mat = pl.BlockSpec((k1t, 2 * FFT_N2, 2 * FFT_N2), lambda g, k: (k, 0, 0))
    return pl.pallas_call(
        functools.partial(_lc_kernel, n1=n1, n1h=n1h, k1t=k1t, nk=nk),
        grid=((bx // 2) * ncb, nk),
        in_specs=[io, io,
                  pl.BlockSpec(f1.shape, lambda g, k: (0, 0)),
                  pl.BlockSpec(g1.shape, lambda g, k: (0, 0)),
                  mat, mat,
                  pl.BlockSpec((1, 2, k1t, FFT_N2, cb), lambda g, k: (order, 0, k, 0, g % ncb)),
                  vec, vec],
        out_specs=io,
        out_shape=jax.ShapeDtypeStruct(v.shape, BF16),
        scratch_shapes=[pltpu.VMEM((n1h * (FFT_N2 + ROW_PAD), cb), jnp.uint32),
                        pltpu.VMEM((FFT_N2 * (n1 + ROW_PAD), cb), jnp.uint32)],
        compiler_params=_cparams(("arbitrary", "arbitrary")),
        name="hyena_longconv",
    )(v, x, f1, g1, m2, m2t, kf, inv_norm.astype(F32).reshape(1, c), skip.astype(F32).reshape(1, c))


def _hyena(h3, conv_w, conv_b, filt, skip):
    seq = h3.shape[1]
    f1, f1_real, g1, m2, m2t = _dft_tables(seq)
    taps, asum = _filters(seq, *filt)
    kf = _fft_mid_fwd(_fft_s1(taps, f1_real, pair=False, hi=True), m2)
    inv_norm = 1.0 / asum[:, 0]
    uc = [_sconv(h3, conv_w, conv_b, seg) for seg in range(3)]
    tables = (f1.astype(BF16), g1.astype(BF16), m2.astype(BF16), m2t.astype(BF16))
    z = uc[0]
    for o in range(HYENA_ORDER):
        z = _longconv(z, uc[1 + o], kf, o, inv_norm[o], skip[o], tables)
    return z


def _mix_kernel(x_ref, a_ref, hb_ref, c_ref, g0_ref, g1_ref, g2_ref, wb_ref, wo_ref,
                lig_ref, lib_ref, l1g_ref, l1b_ref, wrh_ref, wrl_ref, br_ref, x1_ref, x1b_ref, rt_ref, cnt_ref):
    acc = None
    for n, (br, gr) in enumerate(((a_ref, g0_ref), (hb_ref, g1_ref), (c_ref, g2_ref))):
        proj = jnp.dot(br[...], wb_ref[n], preferred_element_type=F32)
        term = jax.nn.sigmoid(gr[...].astype(F32)) * proj
        acc = term if acc is None else acc + term
    mix = jnp.dot(acc.astype(BF16), wo_ref[...], preferred_element_type=F32)
    xn = _ln(x_ref[...], lig_ref[...], lib_ref[...])
    x1 = _ln(DN_ALPHA * xn + mix, l1g_ref[...], l1b_ref[...])
    x1_ref[...] = x1
    xh = x1.astype(BF16)
    x1b_ref[...] = xh
    xl = (x1 - xh.astype(F32)).astype(BF16)
    nt = (((1,), (1,)), ((), ()))
    work = lax.dot_general(wrh_ref[...], xh, nt, preferred_element_type=F32)
    work = work + lax.dot_general(wrh_ref[...], xl, nt, preferred_element_type=F32)
    work = work + lax.dot_general(wrl_ref[...], xh, nt, preferred_element_type=F32)
    work = work + br_ref[...]

    tm = work.shape[1]
    erow = lax.broadcasted_iota(jnp.int32, work.shape, 0)
    ids, vals, hots = [], [], []
    for _ in range(TOP_K):
        mx = jnp.max(work, axis=0, keepdims=True)
        idx = jnp.min(jnp.where(work == mx, erow, N_EXPERTS), axis=0, keepdims=True)
        hot = erow == idx
        ids.append(idx)
        vals.append(mx)
        hots.append(hot)
        work = jnp.where(hot, F32(-3e38), work)
    exps = [jnp.exp(v - vals[0]) for v in vals]
    den = exps[0]
    for e in exps[1:]:
        den = den + e
    chosen = hots[0]
    for hot in hots[1:]:
        chosen = chosen | hot
    chosen = chosen.astype(F32)

    @pl.when(pl.program_id(0) == 0)
    def _():
        cnt_ref[...] = jnp.zeros_like(cnt_ref)

    r_i = lax.broadcasted_iota(jnp.int32, (tm, tm), 0)
    c_i = lax.broadcasted_iota(jnp.int32, (tm, tm), 1)
    earlier = (r_i < c_i).astype(BF16)
    before = jnp.dot(chosen.astype(BF16), earlier, preferred_element_type=F32)
    before = before + jnp.tile(cnt_ref[...], (1, tm // LANE))
    cnt_ref[...] += jnp.sum(chosen, axis=1, keepdims=True)
    rows = [i.astype(F32) for i in ids] + [e / den for e in exps]
    rows += [jnp.sum(jnp.where(hot, before, 0.0), axis=0, keepdims=True) for hot in hots]
    rows.append(jnp.zeros((rt_ref.shape[0] - len(rows), tm), F32))
    rt_ref[...] = jnp.concatenate(rows, axis=0)


ROUTE_ROWS = 16


def _mix(x2, a, hb, c, h2, wb, wo, lig, lib, l1g, l1b, wr, br, tm=512):
    T, D = x2.shape
    wrt = wr.T
    wrh = wrt.astype(BF16)
    wrl = (wrt - wrh.astype(F32)).astype(BF16)
    br = br.reshape(N_EXPERTS, 1)
    row = lambda wdt, col: pl.BlockSpec((tm, wdt), lambda i: (i, col))
    const = lambda arr: pl.BlockSpec(arr.shape, lambda i: (0,) * arr.ndim)
    return pl.pallas_call(
        _mix_kernel,
        grid=(T // tm,),
        in_specs=[row(D, 0), row(BRANCH_W, 0), row(BRANCH_W, 0), row(BRANCH_W, 0),
                  row(D, H_G), row(D, H_G + 1), row(D, H_G + 2),
                  const(wb), const(wo), const(lig), const(lib), const(l1g), const(l1b), const(wrh), const(wrl), const(br)],
        out_specs=[row(D, 0), row(D, 0), pl.BlockSpec((ROUTE_ROWS, tm), lambda i: (0, i)),
                   pl.BlockSpec((N_EXPERTS, LANE), lambda i: (0, 0))],
        out_shape=[jax.ShapeDtypeStruct((T, D), F32), jax.ShapeDtypeStruct((T, D), BF16),
                   jax.ShapeDtypeStruct((ROUTE_ROWS, T), F32), jax.ShapeDtypeStruct((N_EXPERTS, LANE), F32)],
        compiler_params=_cparams(("arbitrary",)),
        name="mix",
    )(x2, a, hb, c, h2, h2, h2, wb, wo, lig, lib, l1g, l1b, wrh, wrl, br)


def _ffn_kernel(ce_ref, x_ref, wgu_ref, bgu_ref, wd_ref, bd_ref, o_ref):
    del ce_ref
    gu = jnp.dot(x_ref[...], wgu_ref[0], preferred_element_type=F32) + bgu_ref[0]
    f = gu.shape[1] // 2
    g = jnp.minimum(gu[:, :f], SWIGLU_LIMIT)
    u = jnp.clip(gu[:, f:], -SWIGLU_LIMIT, SWIGLU_LIMIT)
    hmid = (u + 1.0) * (g * jax.nn.sigmoid(g * SWIGLU_ALPHA))
    y = jnp.dot(hmid.astype(BF16), wd_ref[0], preferred_element_type=F32) + bd_ref[0]
    o_ref[...] = y.astype(o_ref.dtype)


def _ffn(chunk_e, x_slots, wgu, bgu, wd, bd):
    P, D = x_slots.shape
    F = wd.shape[1]
    n_chunks = P // MOE_TILE
    wspec = lambda a, b: pl.BlockSpec((1, a, b), lambda i, ce: (ce[i], 0, 0))
    grid_spec = pltpu.PrefetchScalarGridSpec(
        num_scalar_prefetch=1,
        grid=(n_chunks,),
        in_specs=[pl.BlockSpec((MOE_TILE, D), lambda i, ce: (i, 0)),
                  wspec(D, 2 * F), wspec(1, 2 * F), wspec(F, D), wspec(1, D)],
        out_specs=pl.BlockSpec((MOE_TILE, D), lambda i, ce: (i, 0)),
    )
    return pl.pallas_call(
        _ffn_kernel,
        grid_spec=grid_spec,
        out_shape=jax.ShapeDtypeStruct((P, D), BF16),
        compiler_params=_cparams(("arbitrary",)),
        name="moe_ffn",
    )(chunk_e, x_slots, wgu, bgu, wd, bd)


DEINT_W = 256


def _deint_kernel(w_ref, s_ref, o_ref):
    w = w_ref[0].astype(BF16)
    f = w.shape[1] // 2
    half = DEINT_W // 2
    for blk in range(w.shape[1] // DEINT_W):
        r = jnp.dot(w[:, blk * DEINT_W:(blk + 1) * DEINT_W], s_ref[...], preferred_element_type=F32)
        o_ref[0, :, blk * half:(blk + 1) * half] = r[:, :half].astype(o_ref.dtype)
        o_ref[0, :, f + blk * half:f + (blk + 1) * half] = r[:, half:].astype(o_ref.dtype)


def _deinterleave(w, tr=512):
    E, D, F2 = w.shape
    i = jnp.arange(DEINT_W, dtype=jnp.int32)
    src = jnp.where(i < DEINT_W // 2, 2 * i, 2 * (i - DEINT_W // 2) + 1)
    sel = (i[:, None] == src[None, :]).astype(BF16)
    return pl.pallas_call(
        _deint_kernel,
        grid=(E, D // tr),
        in_specs=[pl.BlockSpec((1, tr, F2), lambda e, r: (e, r, 0)),
                  pl.BlockSpec((DEINT_W, DEINT_W), lambda e, r: (0, 0))],
        out_specs=pl.BlockSpec((1, tr, F2), lambda e, r: (e, r, 0)),
        out_shape=jax.ShapeDtypeStruct((E, D, F2), BF16),
        compiler_params=_cparams(("parallel", "parallel")),
        name="deinterleave_gate_up",
    )(w, sel)


def _final_kernel(x1_ref, y_ref, gt_ref, g_ref, b_ref, o_ref):
    gt = gt_ref[...]
    moe = y_ref[0].astype(F32) * gt[:, 0:1]
    for k in range(1, TOP_K):
        moe = moe + y_ref[k].astype(F32) * gt[:, k:k + 1]
    o_ref[...] = _ln(DN_ALPHA * x1_ref[...] + moe, g_ref[...], b_ref[...])


def _final(x1, yg, gates, g, b, tm=512):
    T, D = x1.shape
    return pl.pallas_call(
        _final_kernel,
        grid=(T // tm,),
        in_specs=[pl.BlockSpec((tm, D), lambda i: (i, 0)),
                  pl.BlockSpec((TOP_K, tm, D), lambda i: (0, i, 0)),
                  pl.BlockSpec((tm, TOP_K), lambda i: (i, 0)),
                  pl.BlockSpec((1, D), lambda i: (0, 0)),
                  pl.BlockSpec((1, D), lambda i: (0, 0))],
        out_specs=pl.BlockSpec((tm, D), lambda i: (i, 0)),
        out_shape=jax.ShapeDtypeStruct((T, D), F32),
        compiler_params=_cparams(("parallel",)),
        name="final_ln",
    )(x1, yg, gates, g, b)


def _moe(x1, x1b, route, cnt, ffn_w, ln2_g, ln2_b):
    T, D = x1.shape
    A = T * TOP_K
    i32 = jnp.int32
    ids = route[:TOP_K].astype(i32)
    gates = route[TOP_K:2 * TOP_K]
    rank = route[2 * TOP_K:3 * TOP_K].astype(i32)
    counts = cnt[:, 0].astype(i32)
    pcounts = (counts + MOE_TILE - 1) // MOE_TILE * MOE_TILE
    pends = jnp.cumsum(pcounts)
    pstarts = pends - pcounts
    n_chunks = -(-A // MOE_TILE) + N_EXPERTS
    P = n_chunks * MOE_TILE
    e3 = jnp.arange(N_EXPERTS, dtype=i32)[:, None, None]
    pos = rank + jnp.sum(jnp.where(ids[None] == e3, pstarts[:, None, None], 0), axis=0)
    padc = jnp.concatenate([pcounts - counts, (P - pends[-1])[None]])
    pad_lo = jnp.concatenate([pstarts + counts, pends[-1:]])
    cum = jnp.cumsum(padc)
    j = jnp.arange(P - A, dtype=i32)
    run = jnp.sum((j[None, :] >= cum[:, None]).astype(i32), axis=0)
    hot = run[None, :] == jnp.arange(N_EXPERTS + 1, dtype=i32)[:, None]
    pad_key = j + jnp.sum(jnp.where(hot, (pad_lo - (cum - padc))[:, None], 0), axis=0)
    keys = jnp.concatenate([pos.reshape(A), pad_key])
    toks = jnp.concatenate([jnp.tile(jnp.arange(T, dtype=i32), TOP_K), jnp.zeros((P - A,), i32)])
    _, slot_tok = lax.sort((keys, toks), num_keys=1)
    chunk_lo = jnp.arange(n_chunks, dtype=i32) * MOE_TILE
    chunk_e = jnp.minimum(jnp.sum((chunk_lo[None, :] >= pends[:, None]).astype(i32), axis=0), N_EXPERTS - 1)
    x_slots = x1b[slot_tok]
    y_slots = _ffn(chunk_e, x_slots, *ffn_w)
    yg = y_slots[pos.reshape(A)].reshape(TOP_K, T, D)
    return _final(x1, yg, gates.T, ln2_g, ln2_b)


def _mixer_half(x, mem, p):
    B, L, D = x.shape
    x2 = x.reshape(B * L, D)
    h2 = _inproj(x2, p["ln_in_g"], p["ln_in_b"], p["w_in"])
    h3 = h2.reshape(B, L, IN_W)
    a = _wattn(h3, p["sink"], p["bias"])
    hb = _hyena(h3, p["conv_w"], p["conv_b"], p["filt"], p["skip"])
    c = _memattn(h3, mem, p["w_mem_kv"])
    return _mix(x2, a.reshape(B * L, -1), hb.reshape(B * L, -1), c.reshape(B * L, -1), h2,
                p["w_branch"], p["w_out"], p["ln_in_g"], p["ln_in_b"], p["ln1_g"], p["ln1_b"],
                p["w_router"], p["b_router"])


def _trunk(x, mem, p):
    return _moe(*_mixer_half(x, mem, p), p["ffn"], p["ln2_g"], p["ln2_b"]).reshape(x.shape)


def _prep(ln_in_g, ln_in_b, rel_bias, w_in, attn_sink, conv_w, conv_b, filt_w1, filt_b1, filt_w2, filt_b2,
          filt_w3, filt_b3, filt_freq, hyena_skip, w_mem_kv, w_branch, w_out, ln1_g, ln1_b, w_router, b_router,
          w_gate_up, b_gate_up, w_down, b_down, ln2_g, ln2_b):
    w = w_in
    g_lo = Q_W + 2 * KV_W + 3 * BRANCH_W + BRANCH_W
    hy_lo = Q_W + 2 * KV_W
    assert HEAD_DIM ** -0.5 == 2.0 ** -3
    w_p = jnp.concatenate([w[:, g_lo:], w[:, :Q_W] * (HEAD_DIM ** -0.5), w[:, hy_lo + 3 * BRANCH_W:g_lo],
                           w[:, hy_lo:hy_lo + 3 * BRANCH_W], w[:, Q_W:hy_lo]], axis=1).astype(BF16)
    row = lambda v: v.astype(F32)[None]
    wr = w_router.astype(F32)
    br = b_router.astype(F32)
    return dict(
        ln_in_g=row(ln_in_g), ln_in_b=row(ln_in_b), w_in=w_p,
        sink=attn_sink.astype(F32), bias=_bias_table(rel_bias),
        conv_w=conv_w.astype(F32), conv_b=row(conv_b),
        filt=(filt_w1, filt_b1, filt_w2, filt_b2, filt_w3, filt_b3, filt_freq), skip=hyena_skip,
        w_mem_kv=w_mem_kv.astype(BF16), w_branch=w_branch.astype(BF16), w_out=w_out.astype(BF16),
        ln1_g=row(ln1_g), ln1_b=row(ln1_b), w_router=wr, b_router=br,
        ffn=(_deinterleave(w_gate_up),
             jnp.concatenate([b_gate_up[:, 0::2], b_gate_up[:, 1::2]], axis=1)[:, None, :].astype(F32),
             w_down.astype(BF16), b_down[:, None, :].astype(F32)),
        ln2_g=row(ln2_g), ln2_b=row(ln2_b),
    )


def kernel(x_prompt, x_sample, mem_prompt, mem_sample, ln_in_g, ln_in_b, rel_bias, w_in, attn_sink, conv_w, conv_b, filt_w1, filt_b1, filt_w2, filt_b2, filt_w3, filt_b3, filt_freq, hyena_skip, w_mem_kv, w_branch, w_out, ln1_g, ln1_b, w_router, b_router, w_gate_up, b_gate_up, w_down, b_down, ln2_g, ln2_b):
    p = _prep(ln_in_g, ln_in_b, rel_bias, w_in[0], attn_sink[0], conv_w[0], conv_b[0], filt_w1[0], filt_b1[0],
              filt_w2[0], filt_b2[0], filt_w3[0], filt_b3[0], filt_freq[0], hyena_skip[0], w_mem_kv[0],
              w_branch[0], w_out[0], ln1_g[0], ln1_b[0], w_router[0], b_router[0], w_gate_up[0], b_gate_up[0],
              w_down[0], b_down[0], ln2_g[0], ln2_b[0])
    halves = [_mixer_half(x, mem, p) for x, mem in ((x_prompt, mem_prompt), (x_sample, mem_sample))]
    outs = [_moe(*half, p["ffn"], p["ln2_g"], p["ln2_b"]) for half in halves]
    return (outs[0].reshape(x_prompt.shape), outs[1].reshape(x_sample.shape))
```

```python
import functools
import math

import jax
import jax.numpy as jnp
from jax import lax
from jax.experimental import pallas as pl
from jax.experimental.pallas import tpu as pltpu

F32 = jnp.float32
BF16 = jnp.bfloat16
HI = lax.Precision.HIGHEST

D_MODEL = 1024
BRANCH_W = 512
N_Q_HEADS = 8
N_KV_HEADS = 2
HEAD_DIM = 64
WINDOW = 128
BLOCK = 128
N_BUCKETS = 32
MAX_DISTANCE = 128
HYENA_ORDER = 2
SHORT_CONV = 3
POS_EMB_DIM = 33
FILTER_HIDDEN = 64
DECAY_TARGET = 1e-2
FAST_DECAY_PCT = 0.3
SLOW_DECAY_PCT = 1.5
N_MEM_HEADS = 4
MEM_HEAD_DIM = BRANCH_W // N_MEM_HEADS
N_BRANCHES = 3
N_EXPERTS = 32
TOP_K = 4
D_EXPERT = 1024
SWIGLU_LIMIT = 7.0
SWIGLU_ALPHA = 1.702
LN_EPS = 1e-5
DEPTH = 1
DN_ALPHA = (2 * DEPTH) ** 0.25

Q_W = N_Q_HEADS * HEAD_DIM
KV_W = N_KV_HEADS * HEAD_DIM
IN_W = Q_W + 2 * KV_W + 3 * BRANCH_W + BRANCH_W + N_BRANCHES * D_MODEL

H_G = 0
H_Q = 6
H_QM = 7
H_HY = 16
H_K = 44
H_V = 45

FFT_N2 = 128
MOE_TILE = 512
LANE = 128
VMEM_LIMIT = 52 * 1024 * 1024


def _cparams(sem):
    return pltpu.CompilerParams(dimension_semantics=sem, vmem_limit_bytes=VMEM_LIMIT)


def _ln(x, g, b):
    mu = jnp.mean(x, axis=-1, keepdims=True)
    xc = x - mu
    var = jnp.mean(xc * xc, axis=-1, keepdims=True)
    return xc * lax.rsqrt(var + LN_EPS) * g + b


def _inproj_kernel(x_ref, g_ref, b_ref, w_ref, o_ref):
    xn = _ln(x_ref[...], g_ref[...], b_ref[...])
    o_ref[...] = jnp.dot(xn.astype(BF16), w_ref[...], preferred_element_type=F32).astype(o_ref.dtype)


def _inproj(x2, ln_g, ln_b, w_p, tm=512):
    T, D = x2.shape
    N = w_p.shape[1]
    tn = N // 2
    return pl.pallas_call(
        _inproj_kernel,
        grid=(N // tn, T // tm),
        in_specs=[
            pl.BlockSpec((tm, D), lambda j, i: (i, 0)),
            pl.BlockSpec((1, D), lambda j, i: (0, 0)),
            pl.BlockSpec((1, D), lambda j, i: (0, 0)),
            pl.BlockSpec((D, tn), lambda j, i: (0, j)),
        ],
        out_specs=pl.BlockSpec((tm, tn), lambda j, i: (i, j)),
        out_shape=jax.ShapeDtypeStruct((T, N), BF16),
        compiler_params=_cparams(("parallel", "parallel")),
        name="inproj",
    )(x2, ln_g, ln_b, w_p)


def _wattn_kernel(sink_ref, q_ref, kp_ref, kc_ref, kn_ref, vp_ref, vc_ref, vn_ref, bias_ref, o_ref, *, seq):
    n = pl.program_id(1)
    q = q_ref[0]
    kcat = jnp.concatenate([kp_ref[0], kc_ref[0], kn_ref[0]], axis=0)
    vcat = jnp.concatenate([vp_ref[0], vc_ref[0], vn_ref[0]], axis=0)
    qi = lax.broadcasted_iota(jnp.int32, (BLOCK, 3 * BLOCK), 0)
    kj = lax.broadcasted_iota(jnp.int32, (BLOCK, 3 * BLOCK), 1)
    rel = kj - BLOCK - qi
    kpos = (n - 1) * BLOCK + kj
    valid = (jnp.abs(rel) <= WINDOW) & (kpos >= 0) & (kpos < seq)
    group = N_Q_HEADS // N_KV_HEADS
    scores = []
    for h in range(N_Q_HEADS):
        g = h // group
        qh = q[:, h * HEAD_DIM:(h + 1) * HEAD_DIM]
        kh = kcat[:, g * HEAD_DIM:(g + 1) * HEAD_DIM]
        scores.append(lax.dot_general(qh, kh, (((1,), (1,)), ((), ())), preferred_element_type=F32))
    probs, denoms = [], []
    for h in range(N_Q_HEADS):
        s = jnp.where(valid, scores[h] + bias_ref[h], F32(-1e30))
        sk = sink_ref[h]
        m = jnp.maximum(jnp.max(s, axis=-1, keepdims=True), sk)
        p = jnp.exp(s - m)
        denoms.append(jnp.sum(p, axis=-1, keepdims=True) + jnp.exp(sk - m))
        probs.append(p.astype(BF16))
    outs = []
    for h in range(N_Q_HEADS):
        g = h // group
        vh = vcat[:, g * HEAD_DIM:(g + 1) * HEAD_DIM]
        outs.append(jnp.dot(probs[h], vh, preferred_element_type=F32) / denoms[h])
    o_ref[0] = jnp.concatenate(outs, axis=1).astype(o_ref.dtype)


def _wattn(h3, sink, bias):
    B, L, _ = h3.shape
    nb = L // BLOCK
    kv_spec = lambda col, shift: pl.BlockSpec(
        (1, BLOCK, KV_W), lambda b, n: (b, jnp.clip(n + shift, 0, nb - 1), col))
    return pl.pallas_call(
        functools.partial(_wattn_kernel, seq=L),
        grid=(B, nb),
        in_specs=[
            pl.BlockSpec(memory_space=pltpu.SMEM),
            pl.BlockSpec((1, BLOCK, Q_W), lambda b, n: (b, n, H_Q)),
            kv_spec(H_K, -1), kv_spec(H_K, 0), kv_spec(H_K, 1),
            kv_spec(H_V, -1), kv_spec(H_V, 0), kv_spec(H_V, 1),
            pl.BlockSpec((N_Q_HEADS, BLOCK, 3 * BLOCK), lambda b, n: (0, 0, 0)),
        ],
        out_specs=pl.BlockSpec((1, BLOCK, Q_W), lambda b, n: (b, n, 0)),
        out_shape=jax.ShapeDtypeStruct((B, L, Q_W), BF16),
        compiler_params=_cparams(("parallel", "parallel")),
        name="wattn",
    )(sink, h3, h3, h3, h3, h3, h3, h3, bias)


def _t5_bucket(rel):
    nb = N_BUCKETS // 2
    max_exact = nb // 2
    ret = jnp.where(rel > 0, nb, 0)
    n = jnp.abs(rel)
    nf = jnp.maximum(n, 1).astype(F32)
    large = max_exact + (jnp.log(nf / max_exact) / math.log(MAX_DISTANCE / max_exact)
                         * (nb - max_exact)).astype(jnp.int32)
    large = jnp.minimum(large, nb - 1)
    return ret + jnp.where(n < max_exact, n, large)


def _bias_table(rel_bias):
    qi = jnp.arange(BLOCK, dtype=jnp.int32)[:, None]
    kj = jnp.arange(3 * BLOCK, dtype=jnp.int32)[None, :]
    bias = rel_bias[_t5_bucket(kj - BLOCK - qi)].astype(F32)
    return jnp.transpose(bias, (2, 0, 1))


def _memattn_kernel(q_ref, mem_ref, w_ref, o_ref, kv_ref):
    @pl.when(pl.program_id(1) == 0)
    def _():
        kv_ref[...] = jnp.dot(mem_ref[0].astype(BF16), w_ref[...],
                              preferred_element_type=F32).astype(kv_ref.dtype)

    q = q_ref[0]
    outs = []
    for h in range(N_MEM_HEADS):
        lo = h * MEM_HEAD_DIM
        qh = q[:, lo:lo + MEM_HEAD_DIM]
        kh = kv_ref[:, lo:lo + MEM_HEAD_DIM]
        vh = kv_ref[:, BRANCH_W + lo:BRANCH_W + lo + MEM_HEAD_DIM]
        s = lax.dot_general(qh, kh, (((1,), (1,)), ((), ())), preferred_element_type=F32) * (MEM_HEAD_DIM ** -0.5)
        m = jnp.max(s, axis=-1, keepdims=True)
        p = jnp.exp(s - m)
        denom = jnp.sum(p, axis=-1, keepdims=True)
        outs.append(jnp.dot(p.astype(BF16), vh, preferred_element_type=F32) / denom)
    o_ref[0] = jnp.concatenate(outs, axis=1).astype(o_ref.dtype)


def _memattn(h3, mem, w_kv, tm=512):
    B, L, _ = h3.shape
    M = mem.shape[1]
    tm = min(tm, L)
    return pl.pallas_call(
        _memattn_kernel,
        grid=(B, L // tm),
        in_specs=[
            pl.BlockSpec((1, tm, BRANCH_W), lambda b, l: (b, l, H_QM)),
            pl.BlockSpec((1, M, D_MODEL), lambda b, l: (b, 0, 0)),
            pl.BlockSpec((D_MODEL, 2 * BRANCH_W), lambda b, l: (0, 0)),
        ],
        out_specs=pl.BlockSpec((1, tm, BRANCH_W), lambda b, l: (b, l, 0)),
        out_shape=jax.ShapeDtypeStruct((B, L, BRANCH_W), BF16),
        scratch_shapes=[pltpu.VMEM((M, 2 * BRANCH_W), BF16)],
        compiler_params=_cparams(("parallel", "arbitrary")),
        name="memattn",
    )(h3, mem, w_kv)


HALO = 16


def _sconv_kernel(cur_ref, prev_ref, next_ref, w_ref, b_ref, o_ref, *, nt):
    t = pl.program_id(2)
    u = cur_ref[0].astype(F32)
    tl = u.shape[0]
    pr = prev_ref[0].astype(F32)[HALO - 1:HALO]
    nx = next_ref[0].astype(F32)[0:1]
    pr = jnp.where(t == 0, 0.0, pr)
    nx = jnp.where(t == nt - 1, 0.0, nx)
    row = lax.broadcasted_iota(jnp.int32, u.shape, 0)
    up = jnp.where(row == 0, pr, pltpu.roll(u, 1, 0))
    un = jnp.where(row == tl - 1, nx, pltpu.roll(u, tl - 1, 0))
    w = w_ref[...]
    o_ref[0] = (b_ref[...] + up * w[0:1] + u * w[1:2] + un * w[2:3]).astype(o_ref.dtype)


def _sconv(h3, conv_w, conv_b, seg, tl=1024, cb=256):
    B, L, _ = h3.shape
    tl = min(tl, L)
    nt = L // tl
    per = BRANCH_W // cb
    col = seg * per
    return pl.pallas_call(
        functools.partial(_sconv_kernel, nt=nt),
        grid=(B, per, nt),
        in_specs=[
            pl.BlockSpec((1, tl, cb), lambda b, j, t: (b, t, H_HY + col + j)),
            pl.BlockSpec((1, HALO, cb), lambda b, j, t: (b, jnp.maximum(t * (tl // HALO) - 1, 0), H_HY + col + j)),
            pl.BlockSpec((1, HALO, cb),
                         lambda b, j, t: (b, jnp.minimum((t + 1) * (tl // HALO), L // HALO - 1), H_HY + col + j)),
            pl.BlockSpec((SHORT_CONV, cb), lambda b, j, t: (0, col + j)),
            pl.BlockSpec((1, cb), lambda b, j, t: (0, col + j)),
        ],
        out_specs=pl.BlockSpec((1, tl, cb), lambda b, j, t: (b, t, j)),
        out_shape=jax.ShapeDtypeStruct((B, L, BRANCH_W), BF16),
        compiler_params=_cparams(("parallel", "parallel", "parallel")),
        name="sconv",
    )(h3, h3, h3, conv_w, conv_b)


def _filt_kernel(w1_ref, b1_ref, w2_ref, b2_ref, w3_ref, b3_ref, fr_ref, fq_ref, dl_ref, k_ref, s_ref, *, seq, tr):
    i = pl.program_id(0)
    m = i * tr + lax.broadcasted_iota(jnp.int32, (tr, LANE), 0)
    lane = lax.broadcasted_iota(jnp.int32, (tr, LANE), 1)
    p = jnp.where(m < seq, m, 2 * seq - m).astype(F32)
    t = p / F32(seq - 1)
    w = F32(2.0 * math.pi) * p / F32(seq)
    arg = fq_ref[...] * w
    bands = (POS_EMB_DIM - 1) // 2
    z = jnp.where(lane == 0, t,
                  jnp.where(lane <= bands, jnp.cos(arg),
                            jnp.where(lane <= 2 * bands, -jnp.sin(arg), 0.0)))
    fr = fr_ref[...]
    h = jnp.sin(fr[0:1] * (jnp.dot(z, w1_ref[...], precision=HI, preferred_element_type=F32) + b1_ref[...]))
    h = jnp.sin(fr[1:2] * (jnp.dot(h, w2_ref[...], precision=HI, preferred_element_type=F32) + b2_ref[...]))
    h = jnp.dot(h, w3_ref[...], precision=HI, preferred_element_type=F32) + b3_ref[...]
    mc = m[:, 0:1]
    tc = t[:, 0:1]
    decay = jnp.exp(-tc * dl_ref[...])
    oc = HYENA_ORDER * BRANCH_W

    @pl.when(i == 0)
    def _():
        s_ref[...] = jnp.zeros_like(s_ref)

    for o in range(HYENA_ORDER):
        fwd = h[:, o * BRANCH_W:(o + 1) * BRANCH_W]
        bwd = h[:, oc + o * BRANCH_W:oc + (o + 1) * BRANCH_W]
        k = jnp.where(mc < seq, fwd, bwd) * decay
        k = jnp.where(mc == seq, 0.0, k)
        k_ref[o] = k
        s_ref[o] += jnp.sum(jnp.abs(k), axis=0, keepdims=True)


def _filters(seq, w1, b1, w2, b2, w3, b3, freq, tr=256):
    n = 2 * seq
    tr = min(tr, n)
    bands = (POS_EMB_DIM - 1) // 2
    hid = LANE
    w1p = jnp.zeros((LANE, hid), F32).at[:POS_EMB_DIM, :FILTER_HIDDEN].set(w1.astype(F32))
    b1p = jnp.zeros((1, hid), F32).at[0, :FILTER_HIDDEN].set(b1.astype(F32))
    w2p = jnp.zeros((hid, hid), F32).at[:FILTER_HIDDEN, :FILTER_HIDDEN].set(w2.astype(F32))
    b2p = jnp.zeros((1, hid), F32).at[0, :FILTER_HIDDEN].set(b2.astype(F32))
    w3p = jnp.zeros((hid, w3.shape[1]), F32).at[:FILTER_HIDDEN].set(w3.astype(F32))
    b3p = b3.astype(F32)[None]
    frp = jnp.zeros((2, hid), F32).at[:, :FILTER_HIDDEN].set(freq.astype(F32))
    f = jnp.linspace(1e-4, bands - 1, bands, dtype=F32)
    fq = jnp.zeros((1, LANE), F32).at[0, 1:1 + bands].set(f).at[0, 1 + bands:1 + 2 * bands].set(f)
    deltas = jnp.abs(jnp.linspace(math.log(DECAY_TARGET) / FAST_DECAY_PCT,
                                  math.log(DECAY_TARGET) / SLOW_DECAY_PCT, BRANCH_W, dtype=F32))[None]
    full = lambda a: pl.BlockSpec(a.shape, lambda i: (0,) * a.ndim)
    args = (w1p, b1p, w2p, b2p, w3p, b3p, frp, fq, deltas)
    return pl.pallas_call(
        functools.partial(_filt_kernel, seq=seq, tr=tr),
        grid=(n // tr,),
        in_specs=[full(a) for a in args],
        out_specs=[pl.BlockSpec((HYENA_ORDER, tr, BRANCH_W), lambda i: (0, i, 0)),
                   pl.BlockSpec((HYENA_ORDER, 1, BRANCH_W), lambda i: (0, 0, 0))],
        out_shape=[jax.ShapeDtypeStruct((HYENA_ORDER, n, BRANCH_W), F32),
                   jax.ShapeDtypeStruct((HYENA_ORDER, 1, BRANCH_W), F32)],
        compiler_params=_cparams(("arbitrary",)),
        name="hyena_filter",
    )(*args)


def _dft_tables(seq):
    n1 = 2 * seq // FFT_N2
    n1h = seq // FFT_N2
    n = 2 * seq
    k1 = jnp.arange(n1, dtype=jnp.int32)
    th = (2.0 * math.pi / n1) * ((k1[:, None] * k1[None, :]) % n1).astype(F32)
    c1, s1 = jnp.cos(th), jnp.sin(th)
    ch, sh = c1[:, :n1h], s1[:, :n1h]
    f1 = jnp.concatenate([jnp.concatenate([ch, sh], 1), jnp.concatenate([-sh, ch], 1)], 0)
    f1_real = jnp.concatenate([c1, -s1], 0)
    cht, sht = ch.T / n, sh.T / n
    g1 = jnp.concatenate([jnp.concatenate([cht, -sht], 1), jnp.concatenate([sht, cht], 1)], 0)
    n2 = jnp.arange(FFT_N2, dtype=jnp.int32)
    kk = k1[:, None, None] + n1 * n2[None, :, None]
    ph = (2.0 * math.pi / n) * ((kk * n2[None, None, :]) % n).astype(F32)
    c2, s2 = jnp.cos(ph), jnp.sin(ph)
    m2 = jnp.concatenate([jnp.concatenate([c2, s2], 2), jnp.concatenate([-s2, c2], 2)], 1)
    return f1, f1_real, g1, m2, jnp.swapaxes(m2, 1, 2)


def _s1_kernel(z_ref, f_ref, o_ref, *, hi):
    nb, r, w = z_ref.shape
    z = z_ref[...].reshape(nb * r, w) if nb > 1 else z_ref[0]
    if hi:
        res = jnp.dot(f_ref[...], z, precision=HI, preferred_element_type=F32)
    else:
        res = jnp.dot(f_ref[...], z.astype(BF16), preferred_element_type=F32)
    half = res.shape[0] // 2
    o_ref[0, 0] = res[:half].astype(o_ref.dtype)
    o_ref[0, 1] = res[half:].astype(o_ref.dtype)


def _fft_s1(z, f, *, pair, hi=False, n2t=8):
    bx, lz, c = z.shape
    r = lz // FFT_N2
    nb = 2 if pair else 1
    n1 = f.shape[0] // 2
    w = n2t * c
    zz = z.reshape(bx, r, FFT_N2 * c)
    out_dtype = F32 if hi else BF16
    return pl.pallas_call(
        functools.partial(_s1_kernel, hi=hi),
        grid=(bx // nb, FFT_N2 // n2t),
        in_specs=[pl.BlockSpec((nb, r, w), lambda p, j: (p, 0, j)),
                  pl.BlockSpec(f.shape, lambda p, j: (0, 0))],
        out_specs=pl.BlockSpec((1, 2, n1, w), lambda p, j: (p, 0, 0, j)),
        out_shape=jax.ShapeDtypeStruct((bx // nb, 2, n1, FFT_N2 * c), out_dtype),
        compiler_params=_cparams(("parallel", "parallel")),
        name="fft_stage1",
    )(zz, f.astype(F32 if hi else BF16))


def _mid_kernel(a_ref, m_ref, mt_ref, kf_ref, o_ref):
    k1t = a_ref.shape[2]
    h = FFT_N2
    for j in range(k1t):
        a = jnp.concatenate([a_ref[0, 0, j], a_ref[0, 1, j]], axis=0)
        x = jnp.dot(m_ref[j], a, preferred_element_type=F32)
        xr, xi = x[:h], x[h:]
        kr, ki = kf_ref[0, 0, j], kf_ref[0, 1, j]
        y = jnp.concatenate([xr * kr - xi * ki, xr * ki + xi * kr], axis=0).astype(BF16)
        b = jnp.dot(mt_ref[j], y, preferred_element_type=F32)
        o_ref[0, 0, j] = b[:h].astype(o_ref.dtype)
        o_ref[0, 1, j] = b[h:].astype(o_ref.dtype)


def _mid_fwd_kernel(a_ref, m_ref, o_ref):
    k1t = a_ref.shape[2]
    h = FFT_N2
    for j in range(k1t):
        a = jnp.concatenate([a_ref[0, 0, j], a_ref[0, 1, j]], axis=0)
        x = jnp.dot(m_ref[j], a, precision=HI, preferred_element_type=F32)
        o_ref[0, 0, j] = x[:h]
        o_ref[0, 1, j] = x[h:]


def _fft_mid(a, m2, m2t, kf, order, k1t=8):
    p, _, n1, wc = a.shape
    c = wc // FFT_N2
    k1t = min(k1t, n1)
    a5 = a.reshape(p, 2, n1, FFT_N2, c)
    blk = (1, 2, k1t, FFT_N2, c)
    out = pl.pallas_call(
        _mid_kernel,
        grid=(n1 // k1t, p),
        in_specs=[pl.BlockSpec(blk, lambda k, q: (q, 0, k, 0, 0)),
                  pl.BlockSpec((k1t, 2 * FFT_N2, 2 * FFT_N2), lambda k, q: (k, 0, 0)),
                  pl.BlockSpec((k1t, 2 * FFT_N2, 2 * FFT_N2), lambda k, q: (k, 0, 0)),
                  pl.BlockSpec(blk, lambda k, q: (order, 0, k, 0, 0))],
        out_specs=pl.BlockSpec(blk, lambda k, q: (q, 0, k, 0, 0)),
        out_shape=jax.ShapeDtypeStruct(a5.shape, BF16),
        compiler_params=_cparams(("parallel", "parallel")),
        name="fft_mid",
    )(a5, m2.astype(BF16), m2t.astype(BF16), kf)
    return out.reshape(a.shape)


def _fft_mid_fwd(a, m2, k1t=8):
    p, _, n1, wc = a.shape
    c = wc // FFT_N2
    k1t = min(k1t, n1)
    a5 = a.reshape(p, 2, n1, FFT_N2, c)
    blk = (1, 2, k1t, FFT_N2, c)
    return pl.pallas_call(
        _mid_fwd_kernel,
        grid=(n1 // k1t, p),
        in_specs=[pl.BlockSpec(blk, lambda k, q: (q, 0, k, 0, 0)),
                  pl.BlockSpec((k1t, 2 * FFT_N2, 2 * FFT_N2), lambda k, q: (k, 0, 0))],
        out_specs=pl.BlockSpec(blk, lambda k, q: (q, 0, k, 0, 0)),
        out_shape=jax.ShapeDtypeStruct(a5.shape, F32),
        compiler_params=_cparams(("parallel", "parallel")),
        name="fft_mid_filter",
    )(a5, m2)


def _s3_kernel(b_ref, g_ref, v_ref, x_ref, sc_ref, d_ref, o_ref):
    bm = jnp.concatenate([b_ref[0, 0], b_ref[0, 1]], axis=0)
    y = jnp.dot(g_ref[...], bm, preferred_element_type=F32)
    r = v_ref.shape[1]
    for part in range(2):
        yp = y[part * r:(part + 1) * r]
        v = v_ref[part].astype(F32)
        x = x_ref[part].astype(F32)
        o_ref[part] = (x * (yp * sc_ref[...] + v * d_ref[...])).astype(o_ref.dtype)


def _fft_s3(bm, g1, v, x, inv_norm, skip, n2t=8):
    p, _, n1, wc = bm.shape
    bx, seq, c = v.shape
    r = seq // FFT_N2
    w = n2t * c
    vv = v.reshape(bx, r, FFT_N2 * c)
    xx = x.reshape(bx, r, FFT_N2 * c)
    sc = jnp.tile(inv_norm.astype(F32).reshape(1, c), (1, n2t))
    dd = jnp.tile(skip.astype(F32).reshape(1, c), (1, n2t))
    io = pl.BlockSpec((2, r, w), lambda q, j: (q, 0, j))
    out = pl.pallas_call(
        _s3_kernel,
        grid=(p, FFT_N2 // n2t),
        in_specs=[pl.BlockSpec((1, 2, n1, w), lambda q, j: (q, 0, 0, j)),
                  pl.BlockSpec(g1.shape, lambda q, j: (0, 0)),
                  io, io,
                  pl.BlockSpec((1, w), lambda q, j: (0, 0)),
                  pl.BlockSpec((1, w), lambda q, j: (0, 0))],
        out_specs=io,
        out_shape=jax.ShapeDtypeStruct(vv.shape, BF16),
        compiler_params=_cparams(("parallel", "parallel")),
        name="fft_stage3",
    )(bm, g1.astype(BF16), vv, xx, sc, dd)
    return out.reshape(bx, seq, c)


ROW_PAD = 8
LC_GROUP = 4


def _pack2(hi, lo):
    hb = lax.bitcast_convert_type(hi.astype(BF16).astype(F32), jnp.uint32)
    lb = lax.bitcast_convert_type(lo.astype(BF16).astype(F32), jnp.uint32)
    return hb | (lb >> 16)


def _unpack2(w):
    return (lax.bitcast_convert_type(w & jnp.uint32(0xFFFF0000), F32),
            lax.bitcast_convert_type(w << 16, F32))


def _time_tile(ref, part, i, n_tiles, conv):
    h = FFT_N2
    r0 = pl.multiple_of(i * h, h)
    cur = ref[part, pl.ds(r0, h), :].astype(F32)
    if conv is None:
        return cur
    w_ref, b_ref = conv
    lo = pl.multiple_of(jnp.maximum(r0 - HALO, 0), HALO)
    hi = pl.multiple_of(jnp.minimum(r0 + h, n_tiles * h - HALO), HALO)
    prev = ref[part, pl.ds(lo, HALO), :].astype(F32)[HALO - 1:HALO]
    nxt = ref[part, pl.ds(hi, HALO), :].astype(F32)[0:1]
    prev = jnp.where(i > 0, prev, 0.0)
    nxt = jnp.where(i < n_tiles - 1, nxt, 0.0)
    row = lax.broadcasted_iota(jnp.int32, cur.shape, 0)
    up = jnp.where(row == 0, prev, pltpu.roll(cur, 1, 0))
    un = jnp.where(row == h - 1, nxt, pltpu.roll(cur, h - 1, 0))
    w = w_ref[...]
    return b_ref[...] + up * w[0:1] + cur * w[1:2] + un * w[2:3]


def _lc_kernel(v_ref, x_ref, f1_ref, g1_ref, m2_ref, m2t_ref, kf_ref, sc_ref, d_ref, cwv_ref, cbv_ref,
               cwx_ref, cbx_ref, o_ref, zst, a3, *, n1, n1h, k1t, nk, grp, conv_v, conv_x):
    k = pl.program_id(1)
    cb = v_ref.shape[2]
    zp = FFT_N2 + ROW_PAD
    ap = n1 + ROW_PAD
    h = FFT_N2
    vconv = (cwv_ref, cbv_ref) if conv_v else None
    xconv = (cwx_ref, cbx_ref) if conv_x else None

    @pl.when(k == 0)
    def _():
        def stage(i, c):
            w = _pack2(_time_tile(v_ref, 0, i, n1h, vconv), _time_tile(v_ref, 1, i, n1h, vconv))
            zst[pl.ds(pl.multiple_of(i * zp, 8), h), :] = w
            return c
        lax.fori_loop(0, n1h, stage, 0)

        def level1(t, c):
            zs = []
            for u in range(grp):
                zr, zi = _unpack2(zst[pl.ds(t * grp + u, n1h, stride=zp), :])
                zs.append(jnp.concatenate([zr, zi], axis=0).astype(BF16))
            res = jnp.dot(f1_ref[...], jnp.concatenate(zs, axis=1), preferred_element_type=F32)
            for u in range(grp):
                r0 = pl.multiple_of((t * grp + u) * ap, 8)
                a3[pl.ds(r0, n1), :] = _pack2(res[:n1, u * cb:(u + 1) * cb], res[n1:, u * cb:(u + 1) * cb])
            return c
        lax.fori_loop(0, h // grp, level1, 0, unroll=2)

    for jj in range(k1t):
        k1 = k * k1t + jj
        ar, ai = _unpack2(a3[pl.ds(k1, h, stride=ap), :])
        a = jnp.concatenate([ar, ai], axis=0).astype(BF16)
        x = jnp.dot(m2_ref[jj], a, preferred_element_type=F32)
        xr, xi = x[:h], x[h:]
        kr, ki = kf_ref[0, 0, jj], kf_ref[0, 1, jj]
        y = jnp.concatenate([xr * kr - xi * ki, xr * ki + xi * kr], axis=0).astype(BF16)
        b = jnp.dot(m2t_ref[jj], y, preferred_element_type=F32)
        a3[pl.ds(k1, h, stride=ap), :] = _pack2(b[:h], b[h:])

    @pl.when(k == nk - 1)
    def _():
        def level3(t, c):
            bms = []
            for u in range(grp):
                r0 = pl.multiple_of((t * grp + u) * ap, 8)
                br, bi = _unpack2(a3[pl.ds(r0, n1), :])
                bms.append(jnp.concatenate([br, bi], axis=0).astype(BF16))
            y = jnp.dot(g1_ref[...], jnp.concatenate(bms, axis=1), preferred_element_type=F32)
            for u in range(grp):
                r0 = pl.multiple_of((t * grp + u) * ap, 8)
                a3[pl.ds(r0, n1), :] = lax.bitcast_convert_type(y[:, u * cb:(u + 1) * cb], jnp.uint32)
            return c
        lax.fori_loop(0, h // grp, level3, 0, unroll=2)

        def gate(i, c):
            r0 = pl.multiple_of(i * h, h)
            vs = _unpack2(zst[pl.ds(pl.multiple_of(i * zp, 8), h), :])
            for part in range(2):
                yv = lax.bitcast_convert_type(a3[pl.ds(part * n1h + i, h, stride=ap), :], F32)
                xx = _time_tile(x_ref, part, i, n1h, xconv)
                o_ref[part, pl.ds(r0, h), :] = (xx * (yv * sc_ref[...] + vs[part] * d_ref[...])).astype(o_ref.dtype)
            return c
        lax.fori_loop(0, n1h, gate, 0)


def _longconv(v, v_seg, x, x_seg, conv_w, conv_b, kf, order, inv_norm, skip, tables, k1t=8, cb=LANE):
    f1, g1, m2, m2t = tables
    bx, seq, _ = v.shape
    c = BRANCH_W
    n1 = 2 * seq // FFT_N2
    n1h = seq // FFT_N2
    k1t = min(k1t, n1)
    nk = n1 // k1t
    ncb = c // cb
    grp = 4 if n1 >= FFT_N2 else 8
    hy0 = H_HY * 256 // cb

    def io(seg):
        col = 0 if seg is None else hy0 + seg * ncb
        return pl.BlockSpec((2, seq, cb), lambda g, k: (g // ncb, 0, col + g % ncb))

    def cvec(rows, seg):
        col = 0 if seg is None else seg * ncb
        return pl.BlockSpec((rows, cb), lambda g, k: (0, col + g % ncb))

    vec = pl.BlockSpec((1, cb), lambda g, k: (0, g % ncb))
    mat = pl.BlockSpec((k1t, 2 * FFT_N2, 2 * FFT_N2), lambda g, k: (k, 0, 0))
    return pl.pallas_call(
        functools.partial(_lc_kernel, n1=n1, n1h=n1h, k1t=k1t, nk=nk, grp=grp,
                          conv_v=v_seg is not None, conv_x=x_seg is not None),
        grid=((bx // 2) * ncb, nk),
        in_specs=[io(v_seg), io(x_seg),
                  pl.BlockSpec(f1.shape, lambda g, k: (0, 0)),
                  pl.BlockSpec(g1.shape, lambda g, k: (0, 0)),
                  mat, mat,
                  pl.BlockSpec((1, 2, k1t, FFT_N2, cb), lambda g, k: (order, 0, k, 0, g % ncb)),
                  vec, vec,
                  cvec(SHORT_CONV, v_seg), cvec(1, v_seg), cvec(SHORT_CONV, x_seg), cvec(1, x_seg)],
        out_specs=io(None),
        out_shape=jax.ShapeDtypeStruct((bx, seq, c), BF16),
        scratch_shapes=[pltpu.VMEM((n1h * (FFT_N2 + ROW_PAD), cb), jnp.uint32),
                        pltpu.VMEM((FFT_N2 * (n1 + ROW_PAD), cb), jnp.uint32)],
        compiler_params=_cparams(("arbitrary", "arbitrary")),
        name="hyena_longconv",
    )(v, x, f1, g1, m2, m2t, kf, inv_norm.astype(F32).reshape(1, c), skip.astype(F32).reshape(1, c),
      conv_w, conv_b, conv_w, conv_b)


def _hyena(h3, conv_w, conv_b, filt, skip):
    seq = h3.shape[1]
    f1, f1_real, g1, m2, m2t = _dft_tables(seq)
    taps, asum = _filters(seq, *filt)
    kf = _fft_mid_fwd(_fft_s1(taps, f1_real, pair=False, hi=True), m2)
    inv_norm = 1.0 / asum[:, 0]
    tables = (f1.astype(BF16), g1.astype(BF16), m2.astype(BF16), m2t.astype(BF16))
    z1 = _longconv(h3, 0, h3, 1, conv_w, conv_b, kf, 0, inv_norm[0], skip[0], tables)
    return _longconv(z1, None, h3, 2, conv_w, conv_b, kf, 1, inv_norm[1], skip[1], tables)


def _mix_kernel(x_ref, a_ref, hb_ref, c_ref, g0_ref, g1_ref, g2_ref, wb_ref, wo_ref,
                lig_ref, lib_ref, l1g_ref, l1b_ref, wrh_ref, wrl_ref, br_ref, x1_ref, x1b_ref, rt_ref, cnt_ref):
    acc = None
    for n, (br, gr) in enumerate(((a_ref, g0_ref), (hb_ref, g1_ref), (c_ref, g2_ref))):
        proj = jnp.dot(br[...], wb_ref[n], preferred_element_type=F32)
        term = jax.nn.sigmoid(gr[...].astype(F32)) * proj
        acc = term if acc is None else acc + term
    mix = jnp.dot(acc.astype(BF16), wo_ref[...], preferred_element_type=F32)
    xn = _ln(x_ref[...], lig_ref[...], lib_ref[...])
    x1 = _ln(DN_ALPHA * xn + mix, l1g_ref[...], l1b_ref[...])
    x1_ref[...] = x1
    xh = x1.astype(BF16)
    x1b_ref[...] = xh
    xl = (x1 - xh.astype(F32)).astype(BF16)
    nt = (((1,), (1,)), ((), ()))
    work = lax.dot_general(wrh_ref[...], xh, nt, preferred_element_type=F32)
    work = work + lax.dot_general(wrh_ref[...], xl, nt, preferred_element_type=F32)
    work = work + lax.dot_general(wrl_ref[...], xh, nt, preferred_element_type=F32)
    work = work + br_ref[...]

    tm = work.shape[1]
    erow = lax.broadcasted_iota(jnp.int32, work.shape, 0)
    ids, vals, hots = [], [], []
    for _ in range(TOP_K):
        mx = jnp.max(work, axis=0, keepdims=True)
        idx = jnp.min(jnp.where(work == mx, erow, N_EXPERTS), axis=0, keepdims=True)
        hot = erow == idx
        ids.append(idx)
        vals.append(mx)
        hots.append(hot)
        work = jnp.where(hot, F32(-3e38), work)
    exps = [jnp.exp(v - vals[0]) for v in vals]
    den = exps[0]
    for e in exps[1:]:
        den = den + e
    chosen = hots[0]
    for hot in hots[1:]:
        chosen = chosen | hot
    chosen = chosen.astype(F32)

    @pl.when(pl.program_id(0) == 0)
    def _():
        cnt_ref[...] = jnp.zeros_like(cnt_ref)

    r_i = lax.broadcasted_iota(jnp.int32, (tm, tm), 0)
    c_i = lax.broadcasted_iota(jnp.int32, (tm, tm), 1)
    earlier = (r_i < c_i).astype(BF16)
    before = jnp.dot(chosen.astype(BF16), earlier, preferred_element_type=F32)
    before = before + jnp.tile(cnt_ref[...], (1, tm // LANE))
    cnt_ref[...] += jnp.sum(chosen, axis=1, keepdims=True)
    rows = [i.astype(F32) for i in ids] + [e / den for e in exps]
    rows += [jnp.sum(jnp.where(hot, before, 0.0), axis=0, keepdims=True) for hot in hots]
    rows.append(jnp.zeros((rt_ref.shape[0] - len(rows), tm), F32))
    rt_ref[...] = jnp.concatenate(rows, axis=0)


ROUTE_ROWS = 16


def _mix(x2, a, hb, c, h2, wb, wo, lig, lib, l1g, l1b, wr, br, tm=512):
    T, D = x2.shape
    wrt = wr.T
    wrh = wrt.astype(BF16)
    wrl = (wrt - wrh.astype(F32)).astype(BF16)
    br = br.reshape(N_EXPERTS, 1)
    row = lambda wdt, col: pl.BlockSpec((tm, wdt), lambda i: (i, col))
    const = lambda arr: pl.BlockSpec(arr.shape, lambda i: (0,) * arr.ndim)
    return pl.pallas_call(
        _mix_kernel,
        grid=(T // tm,),
        in_specs=[row(D, 0), row(BRANCH_W, 0), row(BRANCH_W, 0), row(BRANCH_W, 0),
                  row(D, H_G), row(D, H_G + 1), row(D, H_G + 2),
                  const(wb), const(wo), const(lig), const(lib), const(l1g), const(l1b), const(wrh), const(wrl), const(br)],
        out_specs=[row(D, 0), row(D, 0), pl.BlockSpec((ROUTE_ROWS, tm), lambda i: (0, i)),
                   pl.BlockSpec((N_EXPERTS, LANE), lambda i: (0, 0))],
        out_shape=[jax.ShapeDtypeStruct((T, D), F32), jax.ShapeDtypeStruct((T, D), BF16),
                   jax.ShapeDtypeStruct((ROUTE_ROWS, T), F32), jax.ShapeDtypeStruct((N_EXPERTS, LANE), F32)],
        compiler_params=_cparams(("arbitrary",)),
        name="mix",
    )(x2, a, hb, c, h2, h2, h2, wb, wo, lig, lib, l1g, l1b, wrh, wrl, br)


def _ffn_kernel(ce_ref, nu_ref, x_ref, wgu_ref, bgu_ref, wd_ref, bd_ref, o_ref):
    del ce_ref
    used = pl.program_id(0) < nu_ref[0]

    @pl.when(used)
    def _():
        gu = jnp.dot(x_ref[...], wgu_ref[0], preferred_element_type=F32) + bgu_ref[0]
        f = gu.shape[1] // 2
        g = jnp.minimum(gu[:, :f], SWIGLU_LIMIT)
        u = jnp.clip(gu[:, f:], -SWIGLU_LIMIT, SWIGLU_LIMIT)
        hmid = (u + 1.0) * (g * jax.nn.sigmoid(g * SWIGLU_ALPHA))
        y = jnp.dot(hmid.astype(BF16), wd_ref[0], preferred_element_type=F32) + bd_ref[0]
        o_ref[...] = y.astype(o_ref.dtype)

    @pl.when(jnp.logical_not(used))
    def _():
        o_ref[...] = jnp.zeros_like(o_ref)


def _ffn(chunk_e, n_used, x_slots, wgu, bgu, wd, bd):
    P, D = x_slots.shape
    F = wd.shape[1]
    n_chunks = P // MOE_TILE
    wspec = lambda a, b: pl.BlockSpec((1, a, b), lambda i, ce, nu: (ce[i], 0, 0))
    grid_spec = pltpu.PrefetchScalarGridSpec(
        num_scalar_prefetch=2,
        grid=(n_chunks,),
        in_specs=[pl.BlockSpec((MOE_TILE, D), lambda i, ce, nu: (i, 0)),
                  wspec(D, 2 * F), wspec(1, 2 * F), wspec(F, D), wspec(1, D)],
        out_specs=pl.BlockSpec((MOE_TILE, D), lambda i, ce, nu: (i, 0)),
    )
    return pl.pallas_call(
        _ffn_kernel,
        grid_spec=grid_spec,
        out_shape=jax.ShapeDtypeStruct((P, D), BF16),
        compiler_params=_cparams(("arbitrary",)),
        name="moe_ffn",
    )(chunk_e, n_used, x_slots, wgu, bgu, wd, bd)


DEINT_W = 256


def _deint_kernel(w_ref, s_ref, o_ref):
    w = w_ref[0].astype(BF16)
    f = w.shape[1] // 2
    half = DEINT_W // 2
    for blk in range(w.shape[1] // DEINT_W):
        r = jnp.dot(w[:, blk * DEINT_W:(blk + 1) * DEINT_W], s_ref[...], preferred_element_type=F32)
        o_ref[0, :, blk * half:(blk + 1) * half] = r[:, :half].astype(o_ref.dtype)
        o_ref[0, :, f + blk * half:f + (blk + 1) * half] = r[:, half:].astype(o_ref.dtype)


def _deinterleave(w, tr=512):
    E, D, F2 = w.shape
    i = jnp.arange(DEINT_W, dtype=jnp.int32)
    src = jnp.where(i < DEINT_W // 2, 2 * i, 2 * (i - DEINT_W // 2) + 1)
    sel = (i[:, None] == src[None, :]).astype(BF16)
    return pl.pallas_call(
        _deint_kernel,
        grid=(E, D // tr),
        in_specs=[pl.BlockSpec((1, tr, F2), lambda e, r: (e, r, 0)),
                  pl.BlockSpec((DEINT_W, DEINT_W), lambda e, r: (0, 0))],
        out_specs=pl.BlockSpec((1, tr, F2), lambda e, r: (e, r, 0)),
        out_shape=jax.ShapeDtypeStruct((E, D, F2), BF16),
        compiler_params=_cparams(("parallel", "parallel")),
        name="deinterleave_gate_up",
    )(w, sel)


def _final_kernel(x1_ref, y_ref, gt_ref, g_ref, b_ref, o_ref):
    gt = gt_ref[...]
    moe = y_ref[0].astype(F32) * gt[:, 0:1]
    for k in range(1, TOP_K):
        moe = moe + y_ref[k].astype(F32) * gt[:, k:k + 1]
    o_ref[...] = _ln(DN_ALPHA * x1_ref[...] + moe, g_ref[...], b_ref[...])


def _final(x1, yg, gates, g, b, tm=512):
    T, D = x1.shape
    return pl.pallas_call(
        _final_kernel,
        grid=(T // tm,),
        in_specs=[pl.BlockSpec((tm, D), lambda i: (i, 0)),
                  pl.BlockSpec((TOP_K, tm, D), lambda i: (0, i, 0)),
                  pl.BlockSpec((tm, TOP_K), lambda i: (i, 0)),
                  pl.BlockSpec((1, D), lambda i: (0, 0)),
                  pl.BlockSpec((1, D), lambda i: (0, 0))],
        out_specs=pl.BlockSpec((tm, D), lambda i: (i, 0)),
        out_shape=jax.ShapeDtypeStruct((T, D), F32),
        compiler_params=_cparams(("parallel",)),
        name="final_ln",
    )(x1, yg, gates, g, b)


def _moe(x1, x1b, route, cnt, ffn_w, ln2_g, ln2_b):
    T, D = x1.shape
    A = T * TOP_K
    i32 = jnp.int32
    ids = route[:TOP_K].astype(i32)
    gates = route[TOP_K:2 * TOP_K]
    rank = route[2 * TOP_K:3 * TOP_K].astype(i32)
    counts = cnt[:, 0].astype(i32)
    pcounts = (counts + MOE_TILE - 1) // MOE_TILE * MOE_TILE
    pends = jnp.cumsum(pcounts)
    pstarts = pends - pcounts
    n_chunks = -(-A // MOE_TILE) + N_EXPERTS
    P = n_chunks * MOE_TILE
    e3 = jnp.arange(N_EXPERTS, dtype=i32)[:, None, None]
    pos = rank + jnp.sum(jnp.where(ids[None] == e3, pstarts[:, None, None], 0), axis=0)
    padc = jnp.concatenate([pcounts - counts, (P - pends[-1])[None]])
    pad_lo = jnp.concatenate([pstarts + counts, pends[-1:]])
    cum = jnp.cumsum(padc)
    j = jnp.arange(P - A, dtype=i32)
    run = jnp.sum((j[None, :] >= cum[:, None]).astype(i32), axis=0)
    hot = run[None, :] == jnp.arange(N_EXPERTS + 1, dtype=i32)[:, None]
    pad_key = j + jnp.sum(jnp.where(hot, (pad_lo - (cum - padc))[:, None], 0), axis=0)
    keys = jnp.concatenate([pos.reshape(A), pad_key])
    toks = jnp.concatenate([jnp.tile(jnp.arange(T, dtype=i32), TOP_K), jnp.zeros((P - A,), i32)])
    _, slot_tok = lax.sort((keys, toks), num_keys=1)
    chunk_lo = jnp.arange(n_chunks, dtype=i32) * MOE_TILE
    chunk_e = jnp.minimum(jnp.sum((chunk_lo[None, :] >= pends[:, None]).astype(i32), axis=0), N_EXPERTS - 1)
    x_slots = x1b[slot_tok]
    y_slots = _ffn(chunk_e, (pends[-1:] // MOE_TILE).astype(i32), x_slots, *ffn_w)
    yg = y_slots[pos.reshape(A)].reshape(TOP_K, T, D)
    return _final(x1, yg, gates.T, ln2_g, ln2_b)


def _mixer_half(x, mem, p):
    B, L, D = x.shape
    x2 = x.reshape(B * L, D)
    h2 = _inproj(x2, p["ln_in_g"], p["ln_in_b"], p["w_in"])
    h3 = h2.reshape(B, L, IN_W)
    a = _wattn(h3, p["sink"], p["bias"])
    hb = _hyena(h3, p["conv_w"], p["conv_b"], p["filt"], p["skip"])
    c = _memattn(h3, mem, p["w_mem_kv"])
    return _mix(x2, a.reshape(B * L, -1), hb.reshape(B * L, -1), c.reshape(B * L, -1), h2,
                p["w_branch"], p["w_out"], p["ln_in_g"], p["ln_in_b"], p["ln1_g"], p["ln1_b"],
                p["w_router"], p["b_router"])


def _trunk(x, mem, p):
    return _moe(*_mixer_half(x, mem, p), p["ffn"], p["ln2_g"], p["ln2_b"]).reshape(x.shape)


def _prep(ln_in_g, ln_in_b, rel_bias, w_in, attn_sink, conv_w, conv_b, filt_w1, filt_b1, filt_w2, filt_b2,
          filt_w3, filt_b3, filt_freq, hyena_skip, w_mem_kv, w_branch, w_out, ln1_g, ln1_b, w_router, b_router,
          w_gate_up, b_gate_up, w_down, b_down, ln2_g, ln2_b):
    w = w_in
    g_lo = Q_W + 2 * KV_W + 3 * BRANCH_W + BRANCH_W
    hy_lo = Q_W + 2 * KV_W
    assert HEAD_DIM ** -0.5 == 2.0 ** -3
    w_p = jnp.concatenate([w[:, g_lo:], w[:, :Q_W] * (HEAD_DIM ** -0.5), w[:, hy_lo + 3 * BRANCH_W:g_lo],
                           w[:, hy_lo:hy_lo + 3 * BRANCH_W], w[:, Q_W:hy_lo]], axis=1).astype(BF16)
    row = lambda v: v.astype(F32)[None]
    wr = w_router.astype(F32)
    br = b_router.astype(F32)
    return dict(
        ln_in_g=row(ln_in_g), ln_in_b=row(ln_in_b), w_in=w_p,
        sink=attn_sink.astype(F32), bias=_bias_table(rel_bias),
        conv_w=conv_w.astype(F32), conv_b=row(conv_b),
        filt=(filt_w1, filt_b1, filt_w2, filt_b2, filt_w3, filt_b3, filt_freq), skip=hyena_skip,
        w_mem_kv=w_mem_kv.astype(BF16), w_branch=w_branch.astype(BF16), w_out=w_out.astype(BF16),
        ln1_g=row(ln1_g), ln1_b=row(ln1_b), w_router=wr, b_router=br,
        ffn=(_deinterleave(w_gate_up),
             jnp.concatenate([b_gate_up[:, 0::2], b_gate_up[:, 1::2]], axis=1)[:, None, :].astype(F32),
             w_down.astype(BF16), b_down[:, None, :].astype(F32)),
        ln2_g=row(ln2_g), ln2_b=row(ln2_b),
    )


def kernel(x_prompt, x_sample, mem_prompt, mem_sample, ln_in_g, ln_in_b, rel_bias, w_in, attn_sink, conv_w, conv_b, filt_w1, filt_b1, filt_w2, filt_b2, filt_w3, filt_b3, filt_freq, hyena_skip, w_mem_kv, w_branch, w_out, ln1_g, ln1_b, w_router, b_router, w_gate_up, b_gate_up, w_down, b_down, ln2_g, ln2_b):
    p = _prep(ln_in_g, ln_in_b, rel_bias, w_in[0], attn_sink[0], conv_w[0], conv_b[0], filt_w1[0], filt_b1[0],
              filt_w2[0], filt_b2[0], filt_w3[0], filt_b3[0], filt_freq[0], hyena_skip[0], w_mem_kv[0],
              w_branch[0], w_out[0], ln1_g[0], ln1_b[0], w_router[0], b_router[0], w_gate_up[0], b_gate_up[0],
              w_down[0], b_down[0], ln2_g[0], ln2_b[0])
    halves = [_mixer_half(x, mem, p) for x, mem in ((x_prompt, mem_prompt), (x_sample, mem_sample))]
    outs = [_moe(*half, p["ffn"], p["ln2_g"], p["ln2_b"]) for half in halves]
    return (outs[0].reshape(x_prompt.shape), outs[1].reshape(x_sample.shape))
```

```python
import functools
import math

import jax
import jax.numpy as jnp
from jax import lax
from jax.experimental import pallas as pl
from jax.experimental.pallas import tpu as pltpu

F32 = jnp.float32
BF16 = jnp.bfloat16
HI = lax.Precision.HIGHEST

D_MODEL = 1024
BRANCH_W = 512
N_Q_HEADS = 8
N_KV_HEADS = 2
HEAD_DIM = 64
WINDOW = 128
BLOCK = 128
N_BUCKETS = 32
MAX_DISTANCE = 128
HYENA_ORDER = 2
SHORT_CONV = 3
POS_EMB_DIM = 33
FILTER_HIDDEN = 64
DECAY_TARGET = 1e-2
FAST_DECAY_PCT = 0.3
SLOW_DECAY_PCT = 1.5
N_MEM_HEADS = 4
MEM_HEAD_DIM = BRANCH_W // N_MEM_HEADS
N_BRANCHES = 3
N_EXPERTS = 32
TOP_K = 4
D_EXPERT = 1024
SWIGLU_LIMIT = 7.0
SWIGLU_ALPHA = 1.702
LN_EPS = 1e-5
DEPTH = 1
DN_ALPHA = (2 * DEPTH) ** 0.25

Q_W = N_Q_HEADS * HEAD_DIM
KV_W = N_KV_HEADS * HEAD_DIM
IN_W = Q_W + 2 * KV_W + 3 * BRANCH_W + BRANCH_W + N_BRANCHES * D_MODEL

H_G = 0
H_Q = 6
H_QM = 7
H_HY = 16
H_K = 44
H_V = 45

FFT_N2 = 128
MOE_TILE = 512
LANE = 128
VMEM_LIMIT = 52 * 1024 * 1024


def _cparams(sem):
    return pltpu.CompilerParams(dimension_semantics=sem, vmem_limit_bytes=VMEM_LIMIT)


def _split(x):
    hi = x.astype(BF16)
    return hi, (x - hi.astype(F32)).astype(BF16)


def _dot3(a, b):
    ah, al = _split(a)
    bh, bl = _split(b)
    out = jnp.dot(ah, bh, preferred_element_type=F32)
    out = out + jnp.dot(al, bh, preferred_element_type=F32)
    return out + jnp.dot(ah, bl, preferred_element_type=F32)


def _ln(x, g, b):
    mu = jnp.mean(x, axis=-1, keepdims=True)
    xc = x - mu
    var = jnp.mean(xc * xc, axis=-1, keepdims=True)
    return xc * lax.rsqrt(var + LN_EPS) * g + b


def _inproj_kernel(x_ref, g_ref, b_ref, w_ref, o_ref):
    xn = _ln(x_ref[...], g_ref[...], b_ref[...])
    o_ref[...] = jnp.dot(xn.astype(BF16), w_ref[...], preferred_element_type=F32).astype(o_ref.dtype)


def _inproj(x2, ln_g, ln_b, w_p, tm=512):
    T, D = x2.shape
    N = w_p.shape[1]
    tn = N // 2
    return pl.pallas_call(
        _inproj_kernel,
        grid=(N // tn, T // tm),
        in_specs=[
            pl.BlockSpec((tm, D), lambda j, i: (i, 0)),
            pl.BlockSpec((1, D), lambda j, i: (0, 0)),
            pl.BlockSpec((1, D), lambda j, i: (0, 0)),
            pl.BlockSpec((D, tn), lambda j, i: (0, j)),
        ],
        out_specs=pl.BlockSpec((tm, tn), lambda j, i: (i, j)),
        out_shape=jax.ShapeDtypeStruct((T, N), BF16),
        compiler_params=_cparams(("parallel", "parallel")),
        name="inproj",
    )(x2, ln_g, ln_b, w_p)


def _wattn_kernel(sink_ref, q_ref, kp_ref, kc_ref, kn_ref, vp_ref, vc_ref, vn_ref, bias_ref, o_ref, *, seq):
    n = pl.program_id(1)
    q = q_ref[0]
    kcat = jnp.concatenate([kp_ref[0], kc_ref[0], kn_ref[0]], axis=0)
    vcat = jnp.concatenate([vp_ref[0], vc_ref[0], vn_ref[0]], axis=0)
    qi = lax.broadcasted_iota(jnp.int32, (BLOCK, 3 * BLOCK), 0)
    kj = lax.broadcasted_iota(jnp.int32, (BLOCK, 3 * BLOCK), 1)
    rel = kj - BLOCK - qi
    kpos = (n - 1) * BLOCK + kj
    valid = (jnp.abs(rel) <= WINDOW) & (kpos >= 0) & (kpos < seq)
    group = N_Q_HEADS // N_KV_HEADS
    scores = []
    for h in range(N_Q_HEADS):
        g = h // group
        qh = q[:, h * HEAD_DIM:(h + 1) * HEAD_DIM]
        kh = kcat[:, g * HEAD_DIM:(g + 1) * HEAD_DIM]
        scores.append(lax.dot_general(qh, kh, (((1,), (1,)), ((), ())), preferred_element_type=F32))
    probs, denoms = [], []
    for h in range(N_Q_HEADS):
        s = jnp.where(valid, scores[h] + bias_ref[h], F32(-1e30))
        sk = sink_ref[h]
        m = jnp.maximum(jnp.max(s, axis=-1, keepdims=True), sk)
        p = jnp.exp(s - m)
        denoms.append(jnp.sum(p, axis=-1, keepdims=True) + jnp.exp(sk - m))
        probs.append(p.astype(BF16))
    outs = []
    for h in range(N_Q_HEADS):
        g = h // group
        vh = vcat[:, g * HEAD_DIM:(g + 1) * HEAD_DIM]
        outs.append(jnp.dot(probs[h], vh, preferred_element_type=F32) / denoms[h])
    o_ref[0] = jnp.concatenate(outs, axis=1).astype(o_ref.dtype)


def _wattn(h3, sink, bias):
    B, L, _ = h3.shape
    nb = L // BLOCK
    kv_spec = lambda col, shift: pl.BlockSpec(
        (1, BLOCK, KV_W), lambda b, n: (b, jnp.clip(n + shift, 0, nb - 1), col))
    return pl.pallas_call(
        functools.partial(_wattn_kernel, seq=L),
        grid=(B, nb),
        in_specs=[
            pl.BlockSpec(memory_space=pltpu.SMEM),
            pl.BlockSpec((1, BLOCK, Q_W), lambda b, n: (b, n, H_Q)),
            kv_spec(H_K, -1), kv_spec(H_K, 0), kv_spec(H_K, 1),
            kv_spec(H_V, -1), kv_spec(H_V, 0), kv_spec(H_V, 1),
            pl.BlockSpec((N_Q_HEADS, BLOCK, 3 * BLOCK), lambda b, n: (0, 0, 0)),
        ],
        out_specs=pl.BlockSpec((1, BLOCK, Q_W), lambda b, n: (b, n, 0)),
        out_shape=jax.ShapeDtypeStruct((B, L, Q_W), BF16),
        compiler_params=_cparams(("parallel", "parallel")),
        name="wattn",
    )(sink, h3, h3, h3, h3, h3, h3, h3, bias)


def _t5_bucket(rel):
    nb = N_BUCKETS // 2
    max_exact = nb // 2
    ret = jnp.where(rel > 0, nb, 0)
    n = jnp.abs(rel)
    nf = jnp.maximum(n, 1).astype(F32)
    large = max_exact + (jnp.log(nf / max_exact) / math.log(MAX_DISTANCE / max_exact)
                         * (nb - max_exact)).astype(jnp.int32)
    large = jnp.minimum(large, nb - 1)
    return ret + jnp.where(n < max_exact, n, large)


def _bias_table(rel_bias):
    qi = jnp.arange(BLOCK, dtype=jnp.int32)[:, None]
    kj = jnp.arange(3 * BLOCK, dtype=jnp.int32)[None, :]
    bias = rel_bias[_t5_bucket(kj - BLOCK - qi)].astype(F32)
    return jnp.transpose(bias, (2, 0, 1))


def _memattn_kernel(q_ref, mem_ref, w_ref, o_ref, kv_ref):
    @pl.when(pl.program_id(1) == 0)
    def _():
        kv_ref[...] = jnp.dot(mem_ref[0].astype(BF16), w_ref[...],
                              preferred_element_type=F32).astype(kv_ref.dtype)

    q = q_ref[0]
    outs = []
    for h in range(N_MEM_HEADS):
        lo = h * MEM_HEAD_DIM
        qh = q[:, lo:lo + MEM_HEAD_DIM]
        kh = kv_ref[:, lo:lo + MEM_HEAD_DIM]
        vh = kv_ref[:, BRANCH_W + lo:BRANCH_W + lo + MEM_HEAD_DIM]
        s = lax.dot_general(qh, kh, (((1,), (1,)), ((), ())), preferred_element_type=F32) * (MEM_HEAD_DIM ** -0.5)
        m = jnp.max(s, axis=-1, keepdims=True)
        p = jnp.exp(s - m)
        denom = jnp.sum(p, axis=-1, keepdims=True)
        outs.append(jnp.dot(p.astype(BF16), vh, preferred_element_type=F32) / denom)
    o_ref[0] = jnp.concatenate(outs, axis=1).astype(o_ref.dtype)


def _memattn(h3, mem, w_kv, tm=512):
    B, L, _ = h3.shape
    M = mem.shape[1]
    tm = min(tm, L)
    return pl.pallas_call(
        _memattn_kernel,
        grid=(B, L // tm),
        in_specs=[
            pl.BlockSpec((1, tm, BRANCH_W), lambda b, l: (b, l, H_QM)),
            pl.BlockSpec((1, M, D_MODEL), lambda b, l: (b, 0, 0)),
            pl.BlockSpec((D_MODEL, 2 * BRANCH_W), lambda b, l: (0, 0)),
        ],
        out_specs=pl.BlockSpec((1, tm, BRANCH_W), lambda b, l: (b, l, 0)),
        out_shape=jax.ShapeDtypeStruct((B, L, BRANCH_W), BF16),
        scratch_shapes=[pltpu.VMEM((M, 2 * BRANCH_W), BF16)],
        compiler_params=_cparams(("parallel", "arbitrary")),
        name="memattn",
    )(h3, mem, w_kv)


HALO = 16


def _filt_kernel(w1_ref, b1_ref, w2_ref, b2_ref, w3_ref, b3_ref, fr_ref, fq_ref, dl_ref, k_ref, s_ref, *, seq, tr):
    i = pl.program_id(0)
    m = i * tr + lax.broadcasted_iota(jnp.int32, (tr, LANE), 0)
    lane = lax.broadcasted_iota(jnp.int32, (tr, LANE), 1)
    p = jnp.where(m < seq, m, 2 * seq - m).astype(F32)
    t = p / F32(seq - 1)
    w = F32(2.0 * math.pi) * p / F32(seq)
    bands = (POS_EMB_DIM - 1) // 2
    arg = fq_ref[...] * w
    arg = jnp.where(lane <= bands, arg + F32(0.5 * math.pi), -arg)
    z = jnp.where(lane == 0, t, jnp.where(lane <= 2 * bands, jnp.sin(arg), 0.0))
    fr = fr_ref[...]
    h = jnp.sin(fr[0:1] * (jnp.dot(z, w1_ref[...], precision=HI, preferred_element_type=F32) + b1_ref[...]))
    h = jnp.sin(fr[1:2] * (jnp.dot(h, w2_ref[...], precision=HI, preferred_element_type=F32) + b2_ref[...]))
    h = _dot3(h, w3_ref[...]) + b3_ref[...]
    mc = m[:, 0:1]
    tc = t[:, 0:1]
    decay = jnp.exp(-tc * dl_ref[...])
    oc = HYENA_ORDER * BRANCH_W

    @pl.when(i == 0)
    def _():
        s_ref[...] = jnp.zeros_like(s_ref)

    for o in range(HYENA_ORDER):
        fwd = h[:, o * BRANCH_W:(o + 1) * BRANCH_W]
        bwd = h[:, oc + o * BRANCH_W:oc + (o + 1) * BRANCH_W]
        k = jnp.where(mc < seq, fwd, bwd) * decay
        k = jnp.where(mc == seq, 0.0, k)
        k_ref[o] = k
        s_ref[o] += jnp.sum(jnp.abs(k), axis=0, keepdims=True)


def _filters(seq, w1, b1, w2, b2, w3, b3, freq, tr=256):
    n = 2 * seq
    tr = min(tr, n)
    bands = (POS_EMB_DIM - 1) // 2
    hid = LANE
    w1p = jnp.zeros((LANE, hid), F32).at[:POS_EMB_DIM, :FILTER_HIDDEN].set(w1.astype(F32))
    b1p = jnp.zeros((1, hid), F32).at[0, :FILTER_HIDDEN].set(b1.astype(F32))
    w2p = jnp.zeros((hid, hid), F32).at[:FILTER_HIDDEN, :FILTER_HIDDEN].set(w2.astype(F32))
    b2p = jnp.zeros((1, hid), F32).at[0, :FILTER_HIDDEN].set(b2.astype(F32))
    w3p = jnp.zeros((hid, w3.shape[1]), F32).at[:FILTER_HIDDEN].set(w3.astype(F32))
    b3p = b3.astype(F32)[None]
    frp = jnp.zeros((2, hid), F32).at[:, :FILTER_HIDDEN].set(freq.astype(F32))
    f = jnp.linspace(1e-4, bands - 1, bands, dtype=F32)
    fq = jnp.zeros((1, LANE), F32).at[0, 1:1 + bands].set(f).at[0, 1 + bands:1 + 2 * bands].set(f)
    deltas = jnp.abs(jnp.linspace(math.log(DECAY_TARGET) / FAST_DECAY_PCT,
                                  math.log(DECAY_TARGET) / SLOW_DECAY_PCT, BRANCH_W, dtype=F32))[None]
    full = lambda a: pl.BlockSpec(a.shape, lambda i: (0,) * a.ndim)
    args = (w1p, b1p, w2p, b2p, w3p, b3p, frp, fq, deltas)
    return pl.pallas_call(
        functools.partial(_filt_kernel, seq=seq, tr=tr),
        grid=(n // tr,),
        in_specs=[full(a) for a in args],
        out_specs=[pl.BlockSpec((HYENA_ORDER, tr, BRANCH_W), lambda i: (0, i, 0)),
                   pl.BlockSpec((HYENA_ORDER, 1, BRANCH_W), lambda i: (0, 0, 0))],
        out_shape=[jax.ShapeDtypeStruct((HYENA_ORDER, n, BRANCH_W), F32),
                   jax.ShapeDtypeStruct((HYENA_ORDER, 1, BRANCH_W), F32)],
        compiler_params=_cparams(("arbitrary",)),
        name="hyena_filter",
    )(*args)


def _dft_tables(seq):
    n1 = 2 * seq // FFT_N2
    n1h = seq // FFT_N2
    n = 2 * seq
    k1 = jnp.arange(n1, dtype=jnp.int32)
    th = (2.0 * math.pi / n1) * ((k1[:, None] * k1[None, :]) % n1).astype(F32)
    c1, s1 = jnp.cos(th), jnp.sin(th)
    ch, sh = c1[:, :n1h], s1[:, :n1h]
    f1 = jnp.concatenate([jnp.concatenate([ch, sh], 1), jnp.concatenate([-sh, ch], 1)], 0)
    f1_real = jnp.concatenate([c1, -s1], 0)
    cht, sht = ch.T / n, sh.T / n
    g1 = jnp.concatenate([jnp.concatenate([cht, -sht], 1), jnp.concatenate([sht, cht], 1)], 0)
    n2 = jnp.arange(FFT_N2, dtype=jnp.int32)
    kk = k1[:, None, None] + n1 * n2[None, :, None]
    ph = (2.0 * math.pi / n) * ((kk * n2[None, None, :]) % n).astype(F32)
    c2, s2 = jnp.cos(ph), jnp.sin(ph)
    m2 = jnp.concatenate([jnp.concatenate([c2, s2], 2), jnp.concatenate([-s2, c2], 2)], 1)
    return f1, f1_real, g1, m2, jnp.swapaxes(m2, 1, 2)


def _s1_kernel(z_ref, f_ref, o_ref):
    res = _dot3(f_ref[...], z_ref[0])
    half = res.shape[0] // 2
    o_ref[0, 0] = res[:half]
    o_ref[0, 1] = res[half:]


def _fft_s1(z, f, n2t=8):
    bx, lz, c = z.shape
    r = lz // FFT_N2
    n1 = f.shape[0] // 2
    w = n2t * c
    zz = z.reshape(bx, r, FFT_N2 * c)
    return pl.pallas_call(
        _s1_kernel,
        grid=(bx, FFT_N2 // n2t),
        in_specs=[pl.BlockSpec((1, r, w), lambda p, j: (p, 0, j)),
                  pl.BlockSpec(f.shape, lambda p, j: (0, 0))],
        out_specs=pl.BlockSpec((1, 2, n1, w), lambda p, j: (p, 0, 0, j)),
        out_shape=jax.ShapeDtypeStruct((bx, 2, n1, FFT_N2 * c), F32),
        compiler_params=_cparams(("parallel", "parallel")),
        name="fft_stage1",
    )(zz, f)


def _mid_fwd_kernel(a_ref, m_ref, o_ref):
    k1t = a_ref.shape[2]
    h = FFT_N2
    for j in range(k1t):
        a = jnp.concatenate([a_ref[0, 0, j], a_ref[0, 1, j]], axis=0)
        x = _dot3(m_ref[j], a)
        o_ref[0, 0, j] = x[:h]
        o_ref[0, 1, j] = x[h:]


def _fft_mid_fwd(a, m2, k1t=8):
    p, _, n1, wc = a.shape
    c = wc // FFT_N2
    k1t = min(k1t, n1)
    a5 = a.reshape(p, 2, n1, FFT_N2, c)
    blk = (1, 2, k1t, FFT_N2, c)
    return pl.pallas_call(
        _mid_fwd_kernel,
        grid=(n1 // k1t, p),
        in_specs=[pl.BlockSpec(blk, lambda k, q: (q, 0, k, 0, 0)),
                  pl.BlockSpec((k1t, 2 * FFT_N2, 2 * FFT_N2), lambda k, q: (k, 0, 0))],
        out_specs=pl.BlockSpec(blk, lambda k, q: (q, 0, k, 0, 0)),
        out_shape=jax.ShapeDtypeStruct(a5.shape, F32),
        compiler_params=_cparams(("parallel", "parallel")),
        name="fft_mid_filter",
    )(a5, m2)


ROW_PAD = 8


def _pack2(hi, lo):
    hb = lax.bitcast_convert_type(hi.astype(BF16).astype(F32), jnp.uint32)
    lb = lax.bitcast_convert_type(lo.astype(BF16).astype(F32), jnp.uint32)
    return hb | (lb >> 16)


def _unpack2(w):
    return (lax.bitcast_convert_type(w & jnp.uint32(0xFFFF0000), F32),
            lax.bitcast_convert_type(w << 16, F32))


def _rows(start, size, stride):
    return pl.ds(start, size) if stride is None else pl.ds(start, size, stride=stride)


def _ld(ref, start, size, stride=None):
    parts = [ref[l, _rows(start, size, stride), :] for l in range(ref.shape[0])]
    return parts[0] if len(parts) == 1 else jnp.concatenate(parts, axis=1)


def _st(ref, start, size, val, stride=None):
    for l in range(ref.shape[0]):
        ref[l, _rows(start, size, stride), :] = val[:, l * LANE:(l + 1) * LANE]


def _time_tile(ref, part, i, n_tiles, conv):
    h = FFT_N2
    r0 = pl.multiple_of(i * h, h)
    cur = ref[part, pl.ds(r0, h), :].astype(F32)
    if conv is None:
        return cur
    w_ref, b_ref = conv
    lo = pl.multiple_of(jnp.maximum(r0 - HALO, 0), HALO)
    hi = pl.multiple_of(jnp.minimum(r0 + h, n_tiles * h - HALO), HALO)
    prev = ref[part, pl.ds(lo, HALO), :].astype(F32)[HALO - 1:HALO]
    nxt = ref[part, pl.ds(hi, HALO), :].astype(F32)[0:1]
    prev = jnp.where(i > 0, prev, 0.0)
    nxt = jnp.where(i < n_tiles - 1, nxt, 0.0)
    row = lax.broadcasted_iota(jnp.int32, cur.shape, 0)
    up = jnp.where(row == 0, prev, pltpu.roll(cur, 1, 0))
    un = jnp.where(row == h - 1, nxt, pltpu.roll(cur, h - 1, 0))
    w = w_ref[...]
    return b_ref[...] + up * w[0:1] + cur * w[1:2] + un * w[2:3]


def _lc_kernel(v_ref, x_ref, f1_ref, g1_ref, m2_ref, m2t_ref, kf_ref, sc_ref, d_ref, cwv_ref, cbv_ref,
               cwx_ref, cbx_ref, o_ref, zst, a3, *, n1, n1h, k1t, nk, grp, conv_v, conv_x):
    k = pl.program_id(1)
    cb = v_ref.shape[2]
    zp = FFT_N2 + ROW_PAD
    ap = n1 + ROW_PAD
    h = FFT_N2
    vconv = (cwv_ref, cbv_ref) if conv_v else None
    xconv = (cwx_ref, cbx_ref) if conv_x else None

    @pl.when(k == 0)
    def _():
        def stage(i, c):
            w = _pack2(_time_tile(v_ref, 0, i, n1h, vconv), _time_tile(v_ref, 1, i, n1h, vconv))
            _st(zst, pl.multiple_of(i * zp, 8), h, w)
            return c
        lax.fori_loop(0, n1h, stage, 0)

        def level1(t, c):
            zs = []
            for u in range(grp):
                zr, zi = _unpack2(_ld(zst, t * grp + u, n1h, stride=zp))
                zs.append(jnp.concatenate([zr, zi], axis=0).astype(BF16))
            res = jnp.dot(f1_ref[...], jnp.concatenate(zs, axis=1), preferred_element_type=F32)
            for u in range(grp):
                r0 = pl.multiple_of((t * grp + u) * ap, 8)
                _st(a3, r0, n1, _pack2(res[:n1, u * cb:(u + 1) * cb], res[n1:, u * cb:(u + 1) * cb]))
            return c
        lax.fori_loop(0, h // grp, level1, 0, unroll=2)

    for jj in range(k1t):
        k1 = k * k1t + jj
        ar, ai = _unpack2(_ld(a3, k1, h, stride=ap))
        a = jnp.concatenate([ar, ai], axis=0).astype(BF16)
        x = jnp.dot(m2_ref[jj], a, preferred_element_type=F32)
        xr, xi = x[:h], x[h:]
        kr, ki = kf_ref[0, 0, jj], kf_ref[0, 1, jj]
        y = jnp.concatenate([xr * kr - xi * ki, xr * ki + xi * kr], axis=0).astype(BF16)
        b = jnp.dot(m2t_ref[jj], y, preferred_element_type=F32)
        _st(a3, k1, h, _pack2(b[:h], b[h:]), stride=ap)

    @pl.when(k == nk - 1)
    def _():
        def level3(t, c):
            bms = []
            for u in range(grp):
                r0 = pl.multiple_of((t * grp + u) * ap, 8)
                br, bi = _unpack2(_ld(a3, r0, n1))
                bms.append(jnp.concatenate([br, bi], axis=0).astype(BF16))
            y = jnp.dot(g1_ref[...], jnp.concatenate(bms, axis=1), preferred_element_type=F32)
            for u in range(grp):
                r0 = pl.multiple_of((t * grp + u) * ap, 8)
                _st(a3, r0, n1, lax.bitcast_convert_type(y[:, u * cb:(u + 1) * cb], jnp.uint32))
            return c
        lax.fori_loop(0, h // grp, level3, 0, unroll=2)

        def gate(i, c):
            r0 = pl.multiple_of(i * h, h)
            vs = _unpack2(_ld(zst, pl.multiple_of(i * zp, 8), h))
            for part in range(2):
                yv = lax.bitcast_convert_type(_ld(a3, part * n1h + i, h, stride=ap), F32)
                xx = _time_tile(x_ref, part, i, n1h, xconv)
                o_ref[part, pl.ds(r0, h), :] = (xx * (yv * sc_ref[...] + vs[part] * d_ref[...])).astype(o_ref.dtype)
            return c
        lax.fori_loop(0, n1h, gate, 0)


LC_VMEM_ROWS = 4096


def _longconv(v, v_seg, x, x_seg, conv_w, conv_b, kf, order, inv_norm, skip, tables, k1t=8):
    f1, g1, m2, m2t = tables
    bx, seq, _ = v.shape
    c = BRANCH_W
    cb = 2 * LANE if seq <= LC_VMEM_ROWS else LANE
    n1 = 2 * seq // FFT_N2
    n1h = seq // FFT_N2
    k1t = min(k1t, n1)
    nk = n1 // k1t
    ncb = c // cb
    grp = 4 if n1 >= FFT_N2 else 8
    hy0 = H_HY * 256 // cb

    def io(seg):
        col = 0 if seg is None else hy0 + seg * ncb
        return pl.BlockSpec((2, seq, cb), lambda g, k: (g // ncb, 0, col + g % ncb))

    def cvec(rows, seg):
        col = 0 if seg is None else seg * ncb
        return pl.BlockSpec((rows, cb), lambda g, k: (0, col + g % ncb))

    vec = pl.BlockSpec((1, cb), lambda g, k: (0, g % ncb))
    mat = pl.BlockSpec((k1t, 2 * FFT_N2, 2 * FFT_N2), lambda g, k: (k, 0, 0))
    return pl.pallas_call(
        functools.partial(_lc_kernel, n1=n1, n1h=n1h, k1t=k1t, nk=nk, grp=grp,
                          conv_v=v_seg is not None, conv_x=x_seg is not None),
        grid=((bx // 2) * ncb, nk),
        in_specs=[io(v_seg), io(x_seg),
                  pl.BlockSpec(f1.shape, lambda g, k: (0, 0)),
                  pl.BlockSpec(g1.shape, lambda g, k: (0, 0)),
                  mat, mat,
                  pl.BlockSpec((1, 2, k1t, FFT_N2, cb), lambda g, k: (order, 0, k, 0, g % ncb)),
                  vec, vec,
                  cvec(SHORT_CONV, v_seg), cvec(1, v_seg), cvec(SHORT_CONV, x_seg), cvec(1, x_seg)],
        out_specs=io(None),
        out_shape=jax.ShapeDtypeStruct((bx, seq, c), BF16),
        scratch_shapes=[pltpu.VMEM((cb // LANE, n1h * (FFT_N2 + ROW_PAD), LANE), jnp.uint32),
                        pltpu.VMEM((cb // LANE, FFT_N2 * (n1 + ROW_PAD), LANE), jnp.uint32)],
        compiler_params=_cparams(("arbitrary", "arbitrary")),
        name="hyena_longconv",
    )(v, x, f1, g1, m2, m2t, kf, inv_norm.astype(F32).reshape(1, c), skip.astype(F32).reshape(1, c),
      conv_w, conv_b, conv_w, conv_b)


def _hyena(h3, conv_w, conv_b, filt, skip):
    seq = h3.shape[1]
    f1, f1_real, g1, m2, m2t = _dft_tables(seq)
    taps, asum = _filters(seq, *filt)
    kf = _fft_mid_fwd(_fft_s1(taps, f1_real), m2)
    inv_norm = 1.0 / asum[:, 0]
    tables = (f1.astype(BF16), g1.astype(BF16), m2.astype(BF16), m2t.astype(BF16))
    z1 = _longconv(h3, 0, h3, 1, conv_w, conv_b, kf, 0, inv_norm[0], skip[0], tables)
    return _longconv(z1, None, h3, 2, conv_w, conv_b, kf, 1, inv_norm[1], skip[1], tables)


def _mix_kernel(x_ref, a_ref, hb_ref, c_ref, g0_ref, g1_ref, g2_ref, wb_ref, wo_ref,
                lig_ref, lib_ref, l1g_ref, l1b_ref, wrh_ref, wrl_ref, br_ref, x1_ref, x1b_ref, rt_ref, cnt_ref):
    acc = None
    for n, (br, gr) in enumerate(((a_ref, g0_ref), (hb_ref, g1_ref), (c_ref, g2_ref))):
        proj = jnp.dot(br[...], wb_ref[n], preferred_element_type=F32)
        term = jax.nn.sigmoid(gr[...].astype(F32)) * proj
        acc = term if acc is None else acc + term
    mix = jnp.dot(acc.astype(BF16), wo_ref[...], preferred_element_type=F32)
    xn = _ln(x_ref[...], lig_ref[...], lib_ref[...])
    x1 = _ln(DN_ALPHA * xn + mix, l1g_ref[...], l1b_ref[...])
    x1_ref[...] = x1
    xh = x1.astype(BF16)
    x1b_ref[...] = xh
    xl = (x1 - xh.astype(F32)).astype(BF16)
    nt = (((1,), (1,)), ((), ()))
    work = lax.dot_general(wrh_ref[...], xh, nt, preferred_element_type=F32)
    work = work + lax.dot_general(wrh_ref[...], xl, nt, preferred_element_type=F32)
    work = work + lax.dot_general(wrl_ref[...], xh, nt, preferred_element_type=F32)
    work = work + br_ref[...]

    tm = work.shape[1]
    erow = lax.broadcasted_iota(jnp.int32, work.shape, 0)
    ids, vals, hots = [], [], []
    for _ in range(TOP_K):
        mx = jnp.max(work, axis=0, keepdims=True)
        idx = jnp.min(jnp.where(work == mx, erow, N_EXPERTS), axis=0, keepdims=True)
        hot = erow == idx
        ids.append(idx)
        vals.append(mx)
        hots.append(hot)
        work = jnp.where(hot, F32(-3e38), work)
    exps = [jnp.exp(v - vals[0]) for v in vals]
    den = exps[0]
    for e in exps[1:]:
        den = den + e
    chosen = hots[0]
    for hot in hots[1:]:
        chosen = chosen | hot
    chosen = chosen.astype(F32)

    @pl.when(pl.program_id(0) == 0)
    def _():
        cnt_ref[...] = jnp.zeros_like(cnt_ref)

    r_i = lax.broadcasted_iota(jnp.int32, (tm, tm), 0)
    c_i = lax.broadcasted_iota(jnp.int32, (tm, tm), 1)
    earlier = (r_i < c_i).astype(BF16)
    before = jnp.dot(chosen.astype(BF16), earlier, preferred_element_type=F32)
    before = before + jnp.tile(cnt_ref[...], (1, tm // LANE))
    cnt_ref[...] += jnp.sum(chosen, axis=1, keepdims=True)
    rows = [i.astype(F32) for i in ids] + [e / den for e in exps]
    rows += [jnp.sum(jnp.where(hot, before, 0.0), axis=0, keepdims=True) for hot in hots]
    rows.append(jnp.zeros((rt_ref.shape[0] - len(rows), tm), F32))
    rt_ref[...] = jnp.concatenate(rows, axis=0)


ROUTE_ROWS = 16


def _mix(x2, a, hb, c, h2, wb, wo, lig, lib, l1g, l1b, wr, br, tm=512):
    T, D = x2.shape
    wrt = wr.T
    wrh = wrt.astype(BF16)
    wrl = (wrt - wrh.astype(F32)).astype(BF16)
    br = br.reshape(N_EXPERTS, 1)
    row = lambda wdt, col: pl.BlockSpec((tm, wdt), lambda i: (i, col))
    const = lambda arr: pl.BlockSpec(arr.shape, lambda i: (0,) * arr.ndim)
    return pl.pallas_call(
        _mix_kernel,
        grid=(T // tm,),
        in_specs=[row(D, 0), row(BRANCH_W, 0), row(BRANCH_W, 0), row(BRANCH_W, 0),
                  row(D, H_G), row(D, H_G + 1), row(D, H_G + 2),
                  const(wb), const(wo), const(lig), const(lib), const(l1g), const(l1b), const(wrh), const(wrl), const(br)],
        out_specs=[row(D, 0), row(D, 0), pl.BlockSpec((ROUTE_ROWS, tm), lambda i: (0, i)),
                   pl.BlockSpec((N_EXPERTS, LANE), lambda i: (0, 0))],
        out_shape=[jax.ShapeDtypeStruct((T, D), F32), jax.ShapeDtypeStruct((T, D), BF16),
                   jax.ShapeDtypeStruct((ROUTE_ROWS, T), F32), jax.ShapeDtypeStruct((N_EXPERTS, LANE), F32)],
        compiler_params=_cparams(("arbitrary",)),
        name="mix",
    )(x2, a, hb, c, h2, h2, h2, wb, wo, lig, lib, l1g, l1b, wrh, wrl, br)


def _ffn_kernel(ce_ref, nu_ref, x_ref, wgu_ref, bgu_ref, wd_ref, bd_ref, o_ref):
    del ce_ref
    used = pl.program_id(0) < nu_ref[0]

    @pl.when(used)
    def _():
        gu = jnp.dot(x_ref[...], wgu_ref[0], preferred_element_type=F32) + bgu_ref[0]
        f = gu.shape[1] // 2
        g = jnp.minimum(gu[:, :f], SWIGLU_LIMIT)
        u = jnp.clip(gu[:, f:], -SWIGLU_LIMIT, SWIGLU_LIMIT)
        hmid = (u + 1.0) * (g * jax.nn.sigmoid(g * SWIGLU_ALPHA))
        y = jnp.dot(hmid.astype(BF16), wd_ref[0], preferred_element_type=F32) + bd_ref[0]
        o_ref[...] = y.astype(o_ref.dtype)

    @pl.when(jnp.logical_not(used))
    def _():
        o_ref[...] = jnp.zeros_like(o_ref)


def _ffn(chunk_e, n_used, x_slots, wgu, bgu, wd, bd):
    P, D = x_slots.shape
    F = wd.shape[1]
    n_chunks = P // MOE_TILE
    wspec = lambda a, b: pl.BlockSpec((1, a, b), lambda i, ce, nu: (ce[i], 0, 0))
    grid_spec = pltpu.PrefetchScalarGridSpec(
        num_scalar_prefetch=2,
        grid=(n_chunks,),
        in_specs=[pl.BlockSpec((MOE_TILE, D), lambda i, ce, nu: (i, 0)),
                  wspec(D, 2 * F), wspec(1, 2 * F), wspec(F, D), wspec(1, D)],
        out_specs=pl.BlockSpec((MOE_TILE, D), lambda i, ce, nu: (i, 0)),
    )
    return pl.pallas_call(
        _ffn_kernel,
        grid_spec=grid_spec,
        out_shape=jax.ShapeDtypeStruct((P, D), BF16),
        compiler_params=_cparams(("arbitrary",)),
        name="moe_ffn",
    )(chunk_e, n_used, x_slots, wgu, bgu, wd, bd)


DEINT_W = 256


def _deint_kernel(w_ref, s_ref, o_ref):
    w = w_ref[0].astype(BF16)
    f = w.shape[1] // 2
    half = DEINT_W // 2
    for blk in range(w.shape[1] // DEINT_W):
        r = jnp.dot(w[:, blk * DEINT_W:(blk + 1) * DEINT_W], s_ref[...], preferred_element_type=F32)
        o_ref[0, :, blk * half:(blk + 1) * half] = r[:, :half].astype(o_ref.dtype)
        o_ref[0, :, f + blk * half:f + (blk + 1) * half] = r[:, half:].astype(o_ref.dtype)


def _deinterleave(w, tr=512):
    E, D, F2 = w.shape
    i = jnp.arange(DEINT_W, dtype=jnp.int32)
    src = jnp.where(i < DEINT_W // 2, 2 * i, 2 * (i - DEINT_W // 2) + 1)
    sel = (i[:, None] == src[None, :]).astype(BF16)
    return pl.pallas_call(
        _deint_kernel,
        grid=(E, D // tr),
        in_specs=[pl.BlockSpec((1, tr, F2), lambda e, r: (e, r, 0)),
                  pl.BlockSpec((DEINT_W, DEINT_W), lambda e, r: (0, 0))],
        out_specs=pl.BlockSpec((1, tr, F2), lambda e, r: (e, r, 0)),
        out_shape=jax.ShapeDtypeStruct((E, D, F2), BF16),
        compiler_params=_cparams(("parallel", "parallel")),
        name="deinterleave_gate_up",
    )(w, sel)


def _final_kernel(x1_ref, y_ref, gt_ref, g_ref, b_ref, o_ref):
    gt = gt_ref[...]
    moe = y_ref[0].astype(F32) * gt[:, 0:1]
    for k in range(1, TOP_K):
        moe = moe + y_ref[k].astype(F32) * gt[:, k:k + 1]
    o_ref[...] = _ln(DN_ALPHA * x1_ref[...] + moe, g_ref[...], b_ref[...])


def _final(x1, yg, gates, g, b, tm=512):
    T, D = x1.shape
    return pl.pallas_call(
        _final_kernel,
        grid=(T // tm,),
        in_specs=[pl.BlockSpec((tm, D), lambda i: (i, 0)),
                  pl.BlockSpec((TOP_K, tm, D), lambda i: (0, i, 0)),
                  pl.BlockSpec((tm, TOP_K), lambda i: (i, 0)),
                  pl.BlockSpec((1, D), lambda i: (0, 0)),
                  pl.BlockSpec((1, D), lambda i: (0, 0))],
        out_specs=pl.BlockSpec((tm, D), lambda i: (i, 0)),
        out_shape=jax.ShapeDtypeStruct((T, D), F32),
        compiler_params=_cparams(("parallel",)),
        name="final_ln",
    )(x1, yg, gates, g, b)


def _moe(x1, x1b, route, cnt, ffn_w, ln2_g, ln2_b):
    T, D = x1.shape
    A = T * TOP_K
    i32 = jnp.int32
    ids = route[:TOP_K].astype(i32)
    gates = route[TOP_K:2 * TOP_K]
    rank = route[2 * TOP_K:3 * TOP_K].astype(i32)
    counts = cnt[:, 0].astype(i32)
    pcounts = (counts + MOE_TILE - 1) // MOE_TILE * MOE_TILE
    pends = jnp.cumsum(pcounts)
    pstarts = pends - pcounts
    n_chunks = -(-A // MOE_TILE) + N_EXPERTS
    P = n_chunks * MOE_TILE
    e3 = jnp.arange(N_EXPERTS, dtype=i32)[:, None, None]
    pos = rank + jnp.sum(jnp.where(ids[None] == e3, pstarts[:, None, None], 0), axis=0)
    padc = jnp.concatenate([pcounts - counts, (P - pends[-1])[None]])
    pad_lo = jnp.concatenate([pstarts + counts, pends[-1:]])
    cum = jnp.cumsum(padc)
    j = jnp.arange(P - A, dtype=i32)
    run = jnp.sum((j[None, :] >= cum[:, None]).astype(i32), axis=0)
    hot = run[None, :] == jnp.arange(N_EXPERTS + 1, dtype=i32)[:, None]
    pad_key = j + jnp.sum(jnp.where(hot, (pad_lo - (cum - padc))[:, None], 0), axis=0)
    keys = jnp.concatenate([pos.reshape(A), pad_key])
    toks = jnp.concatenate([jnp.tile(jnp.arange(T, dtype=i32), TOP_K), jnp.zeros((P - A,), i32)])
    _, slot_tok = lax.sort((keys, toks), num_keys=1)
    chunk_lo = jnp.arange(n_chunks, dtype=i32) * MOE_TILE
    chunk_e = jnp.minimum(jnp.sum((chunk_lo[None, :] >= pends[:, None]).astype(i32), axis=0), N_EXPERTS - 1)
    x_slots = x1b[slot_tok]
    y_slots = _ffn(chunk_e, (pends[-1:] // MOE_TILE).astype(i32), x_slots, *ffn_w)
    yg = y_slots[pos.reshape(A)].reshape(TOP_K, T, D)
    return _final(x1, yg, gates.T, ln2_g, ln2_b)


def _mixer_half(x, mem, p):
    B, L, D = x.shape
    x2 = x.reshape(B * L, D)
    h2 = _inproj(x2, p["ln_in_g"], p["ln_in_b"], p["w_in"])
    h3 = h2.reshape(B, L, IN_W)
    a = _wattn(h3, p["sink"], p["bias"])
    hb = _hyena(h3, p["conv_w"], p["conv_b"], p["filt"], p["skip"])
    c = _memattn(h3, mem, p["w_mem_kv"])
    return _mix(x2, a.reshape(B * L, -1), hb.reshape(B * L, -1), c.reshape(B * L, -1), h2,
                p["w_branch"], p["w_out"], p["ln_in_g"], p["ln_in_b"], p["ln1_g"], p["ln1_b"],
                p["w_router"], p["b_router"])


def _trunk(x, mem, p):
    return _moe(*_mixer_half(x, mem, p), p["ffn"], p["ln2_g"], p["ln2_b"]).reshape(x.shape)


def _prep(ln_in_g, ln_in_b, rel_bias, w_in, attn_sink, conv_w, conv_b, filt_w1, filt_b1, filt_w2, filt_b2,
          filt_w3, filt_b3, filt_freq, hyena_skip, w_mem_kv, w_branch, w_out, ln1_g, ln1_b, w_router, b_router,
          w_gate_up, b_gate_up, w_down, b_down, ln2_g, ln2_b):
    w = w_in
    g_lo = Q_W + 2 * KV_W + 3 * BRANCH_W + BRANCH_W
    hy_lo = Q_W + 2 * KV_W
    assert HEAD_DIM ** -0.5 == 2.0 ** -3
    w_p = jnp.concatenate([w[:, g_lo:], w[:, :Q_W] * (HEAD_DIM ** -0.5), w[:, hy_lo + 3 * BRANCH_W:g_lo],
                           w[:, hy_lo:hy_lo + 3 * BRANCH_W], w[:, Q_W:hy_lo]], axis=1).astype(BF16)
    row = lambda v: v.astype(F32)[None]
    wr = w_router.astype(F32)
    br = b_router.astype(F32)
    return dict(
        ln_in_g=row(ln_in_g), ln_in_b=row(ln_in_b), w_in=w_p,
        sink=attn_sink.astype(F32), bias=_bias_table(rel_bias),
        conv_w=conv_w.astype(F32), conv_b=row(conv_b),
        filt=(filt_w1, filt_b1, filt_w2, filt_b2, filt_w3, filt_b3, filt_freq), skip=hyena_skip,
        w_mem_kv=w_mem_kv.astype(BF16), w_branch=w_branch.astype(BF16), w_out=w_out.astype(BF16),
        ln1_g=row(ln1_g), ln1_b=row(ln1_b), w_router=wr, b_router=br,
        ffn=(_deinterleave(w_gate_up),
             jnp.concatenate([b_gate_up[:, 0::2], b_gate_up[:, 1::2]], axis=1)[:, None, :].astype(F32),
             w_down.astype(BF16), b_down[:, None, :].astype(F32)),
        ln2_g=row(ln2_g), ln2_b=row(ln2_b),
    )


def kernel(x_prompt, x_sample, mem_prompt, mem_sample, ln_in_g, ln_in_b, rel_bias, w_in, attn_sink, conv_w, conv_b, filt_w1, filt_b1, filt_w2, filt_b2, filt_w3, filt_b3, filt_freq, hyena_skip, w_mem_kv, w_branch, w_out, ln1_g, ln1_b, w_router, b_router, w_gate_up, b_gate_up, w_down, b_down, ln2_g, ln2_b):
    p = _prep(ln_in_g, ln_in_b, rel_bias, w_in[0], attn_sink[0], conv_w[0], conv_b[0], filt_w1[0], filt_b1[0],
              filt_w2[0], filt_b2[0], filt_w3[0], filt_b3[0], filt_freq[0], hyena_skip[0], w_mem_kv[0],
              w_branch[0], w_out[0], ln1_g[0], ln1_b[0], w_router[0], b_router[0], w_gate_up[0], b_gate_up[0],
              w_down[0], b_down[0], ln2_g[0], ln2_b[0])
    halves = [_mixer_half(x, mem, p) for x, mem in ((x_prompt, mem_prompt), (x_sample, mem_sample))]
    outs = [_moe(*half, p["ffn"], p["ln2_g"], p["ln2_b"]) for half in halves]
    return (outs[0].reshape(x_prompt.shape), outs[1].reshape(x_sample.shape))
```

```python
import functools
import math

import jax
import jax.numpy as jnp
from jax import lax
from jax.experimental import pallas as pl
from jax.experimental.pallas import tpu as pltpu

F32 = jnp.float32
BF16 = jnp.bfloat16
HI = lax.Precision.HIGHEST

D_MODEL = 1024
BRANCH_W = 512
N_Q_HEADS = 8
N_KV_HEADS = 2
HEAD_DIM = 64
WINDOW = 128
BLOCK = 128
N_BUCKETS = 32
MAX_DISTANCE = 128
HYENA_ORDER = 2
SHORT_CONV = 3
POS_EMB_DIM = 33
FILTER_HIDDEN = 64
DECAY_TARGET = 1e-2
FAST_DECAY_PCT = 0.3
SLOW_DECAY_PCT = 1.5
N_MEM_HEADS = 4
MEM_HEAD_DIM = BRANCH_W // N_MEM_HEADS
N_BRANCHES = 3
N_EXPERTS = 32
TOP_K = 4
D_EXPERT = 1024
SWIGLU_LIMIT = 7.0
SWIGLU_ALPHA = 1.702
LN_EPS = 1e-5
DEPTH = 1
DN_ALPHA = (2 * DEPTH) ** 0.25

Q_W = N_Q_HEADS * HEAD_DIM
KV_W = N_KV_HEADS * HEAD_DIM
IN_W = Q_W + 2 * KV_W + 3 * BRANCH_W + BRANCH_W + N_BRANCHES * D_MODEL

H_G = 0
H_Q = 6
H_QM = 7
H_HY = 16
H_K = 44
H_V = 45

FFT_N2 = 128
MOE_TILE = 512
LANE = 128
VMEM_LIMIT = 52 * 1024 * 1024


def _cparams(sem):
    return pltpu.CompilerParams(dimension_semantics=sem, vmem_limit_bytes=VMEM_LIMIT)


def _split(x):
    hi = x.astype(BF16)
    return hi, (x - hi.astype(F32)).astype(BF16)


def _dot3(a, b):
    ah, al = _split(a)
    bh, bl = _split(b)
    out = jnp.dot(ah, bh, preferred_element_type=F32)
    out = out + jnp.dot(al, bh, preferred_element_type=F32)
    return out + jnp.dot(ah, bl, preferred_element_type=F32)


def _ln(x, g, b):
    mu = jnp.mean(x, axis=-1, keepdims=True)
    xc = x - mu
    var = jnp.mean(xc * xc, axis=-1, keepdims=True)
    return xc * lax.rsqrt(var + LN_EPS) * g + b


def _inproj_kernel(x_ref, g_ref, b_ref, w_ref, o_ref):
    xn = _ln(x_ref[...], g_ref[...], b_ref[...])
    o_ref[...] = jnp.dot(xn.astype(BF16), w_ref[...], preferred_element_type=F32).astype(o_ref.dtype)


def _inproj(x2, ln_g, ln_b, w_p, tm=512):
    T, D = x2.shape
    N = w_p.shape[1]
    tn = N
    return pl.pallas_call(
        _inproj_kernel,
        grid=(N // tn, T // tm),
        in_specs=[
            pl.BlockSpec((tm, D), lambda j, i: (i, 0)),
            pl.BlockSpec((1, D), lambda j, i: (0, 0)),
            pl.BlockSpec((1, D), lambda j, i: (0, 0)),
            pl.BlockSpec((D, tn), lambda j, i: (0, j)),
        ],
        out_specs=pl.BlockSpec((tm, tn), lambda j, i: (i, j)),
        out_shape=jax.ShapeDtypeStruct((T, N), BF16),
        compiler_params=_cparams(("parallel", "parallel")),
        name="inproj",
    )(x2, ln_g, ln_b, w_p)


def _wattn_kernel(sink_ref, q_ref, kp_ref, kc_ref, kn_ref, vp_ref, vc_ref, vn_ref, bias_ref, o_ref, *, seq):
    n = pl.program_id(1)
    q = q_ref[0]
    kcat = jnp.concatenate([kp_ref[0], kc_ref[0], kn_ref[0]], axis=0)
    vcat = jnp.concatenate([vp_ref[0], vc_ref[0], vn_ref[0]], axis=0)
    qi = lax.broadcasted_iota(jnp.int32, (BLOCK, 3 * BLOCK), 0)
    kj = lax.broadcasted_iota(jnp.int32, (BLOCK, 3 * BLOCK), 1)
    rel = kj - BLOCK - qi
    kpos = (n - 1) * BLOCK + kj
    valid = (jnp.abs(rel) <= WINDOW) & (kpos >= 0) & (kpos < seq)
    group = N_Q_HEADS // N_KV_HEADS
    scores = []
    for h in range(N_Q_HEADS):
        g = h // group
        qh = q[:, h * HEAD_DIM:(h + 1) * HEAD_DIM]
        kh = kcat[:, g * HEAD_DIM:(g + 1) * HEAD_DIM]
        scores.append(lax.dot_general(qh, kh, (((1,), (1,)), ((), ())), preferred_element_type=F32))
    probs, denoms = [], []
    for h in range(N_Q_HEADS):
        s = jnp.where(valid, scores[h] + bias_ref[h], F32(-1e30))
        sk = sink_ref[h]
        m = jnp.maximum(jnp.max(s, axis=-1, keepdims=True), sk)
        p = jnp.exp(s - m)
        denoms.append(jnp.sum(p, axis=-1, keepdims=True) + jnp.exp(sk - m))
        probs.append(p.astype(BF16))
    outs = []
    for h in range(N_Q_HEADS):
        g = h // group
        vh = vcat[:, g * HEAD_DIM:(g + 1) * HEAD_DIM]
        outs.append(jnp.dot(probs[h], vh, preferred_element_type=F32) / denoms[h])
    o_ref[0] = jnp.concatenate(outs, axis=1).astype(o_ref.dtype)


def _wattn(h3, sink, bias):
    B, L, _ = h3.shape
    nb = L // BLOCK
    kv_spec = lambda col, shift: pl.BlockSpec(
        (1, BLOCK, KV_W), lambda b, n: (b, jnp.clip(n + shift, 0, nb - 1), col))
    return pl.pallas_call(
        functools.partial(_wattn_kernel, seq=L),
        grid=(B, nb),
        in_specs=[
            pl.BlockSpec(memory_space=pltpu.SMEM),
            pl.BlockSpec((1, BLOCK, Q_W), lambda b, n: (b, n, H_Q)),
            kv_spec(H_K, -1), kv_spec(H_K, 0), kv_spec(H_K, 1),
            kv_spec(H_V, -1), kv_spec(H_V, 0), kv_spec(H_V, 1),
            pl.BlockSpec((N_Q_HEADS, BLOCK, 3 * BLOCK), lambda b, n: (0, 0, 0)),
        ],
        out_specs=pl.BlockSpec((1, BLOCK, Q_W), lambda b, n: (b, n, 0)),
        out_shape=jax.ShapeDtypeStruct((B, L, Q_W), BF16),
        compiler_params=_cparams(("parallel", "parallel")),
        name="wattn",
    )(sink, h3, h3, h3, h3, h3, h3, h3, bias)


def _t5_bucket(rel):
    nb = N_BUCKETS // 2
    max_exact = nb // 2
    ret = jnp.where(rel > 0, nb, 0)
    n = jnp.abs(rel)
    nf = jnp.maximum(n, 1).astype(F32)
    large = max_exact + (jnp.log(nf / max_exact) / math.log(MAX_DISTANCE / max_exact)
                         * (nb - max_exact)).astype(jnp.int32)
    large = jnp.minimum(large, nb - 1)
    return ret + jnp.where(n < max_exact, n, large)


def _bias_table(rel_bias):
    qi = jnp.arange(BLOCK, dtype=jnp.int32)[:, None]
    kj = jnp.arange(3 * BLOCK, dtype=jnp.int32)[None, :]
    bias = rel_bias[_t5_bucket(kj - BLOCK - qi)].astype(F32)
    return jnp.transpose(bias, (2, 0, 1))


def _memattn_kernel(q_ref, mem_ref, w_ref, o_ref, kv_ref):
    @pl.when(pl.program_id(1) == 0)
    def _():
        kv_ref[...] = jnp.dot(mem_ref[0].astype(BF16), w_ref[...],
                              preferred_element_type=F32).astype(kv_ref.dtype)

    q = q_ref[0]
    outs = []
    for h in range(N_MEM_HEADS):
        lo = h * MEM_HEAD_DIM
        qh = q[:, lo:lo + MEM_HEAD_DIM]
        kh = kv_ref[:, lo:lo + MEM_HEAD_DIM]
        vh = kv_ref[:, BRANCH_W + lo:BRANCH_W + lo + MEM_HEAD_DIM]
        s = lax.dot_general(qh, kh, (((1,), (1,)), ((), ())), preferred_element_type=F32) * (MEM_HEAD_DIM ** -0.5)
        m = jnp.max(s, axis=-1, keepdims=True)
        p = jnp.exp(s - m)
        denom = jnp.sum(p, axis=-1, keepdims=True)
        outs.append(jnp.dot(p.astype(BF16), vh, preferred_element_type=F32) / denom)
    o_ref[0] = jnp.concatenate(outs, axis=1).astype(o_ref.dtype)


def _memattn(h3, mem, w_kv, tm=512):
    B, L, _ = h3.shape
    M = mem.shape[1]
    tm = min(tm, L)
    return pl.pallas_call(
        _memattn_kernel,
        grid=(B, L // tm),
        in_specs=[
            pl.BlockSpec((1, tm, BRANCH_W), lambda b, l: (b, l, H_QM)),
            pl.BlockSpec((1, M, D_MODEL), lambda b, l: (b, 0, 0)),
            pl.BlockSpec((D_MODEL, 2 * BRANCH_W), lambda b, l: (0, 0)),
        ],
        out_specs=pl.BlockSpec((1, tm, BRANCH_W), lambda b, l: (b, l, 0)),
        out_shape=jax.ShapeDtypeStruct((B, L, BRANCH_W), BF16),
        scratch_shapes=[pltpu.VMEM((M, 2 * BRANCH_W), BF16)],
        compiler_params=_cparams(("parallel", "arbitrary")),
        name="memattn",
    )(h3, mem, w_kv)


HALO = 16


def _filt_kernel(w1_ref, b1_ref, w2_ref, b2_ref, w3_ref, b3_ref, fr_ref, fq_ref, dl_ref, k_ref, s_ref, *, seq, tr):
    i = pl.program_id(0)
    m = i * tr + lax.broadcasted_iota(jnp.int32, (tr, LANE), 0)
    lane = lax.broadcasted_iota(jnp.int32, (tr, LANE), 1)
    p = jnp.where(m < seq, m, 2 * seq - m).astype(F32)
    t = p / F32(seq - 1)
    w = F32(2.0 * math.pi) * p / F32(seq)
    bands = (POS_EMB_DIM - 1) // 2
    arg = fq_ref[...] * w
    arg = jnp.where(lane <= bands, arg + F32(0.5 * math.pi), -arg)
    z = jnp.where(lane == 0, t, jnp.where(lane <= 2 * bands, jnp.sin(arg), 0.0))
    fr = fr_ref[...]
    h = jnp.sin(fr[0:1] * (jnp.dot(z, w1_ref[...], precision=HI, preferred_element_type=F32) + b1_ref[...]))
    h = jnp.sin(fr[1:2] * (jnp.dot(h, w2_ref[...], precision=HI, preferred_element_type=F32) + b2_ref[...]))
    h = _dot3(h, w3_ref[...]) + b3_ref[...]
    mc = m[:, 0:1]
    tc = t[:, 0:1]
    decay = jnp.exp(-tc * dl_ref[...])
    oc = HYENA_ORDER * BRANCH_W

    @pl.when(i == 0)
    def _():
        s_ref[...] = jnp.zeros_like(s_ref)

    for o in range(HYENA_ORDER):
        fwd = h[:, o * BRANCH_W:(o + 1) * BRANCH_W]
        bwd = h[:, oc + o * BRANCH_W:oc + (o + 1) * BRANCH_W]
        k = jnp.where(mc < seq, fwd, bwd) * decay
        k = jnp.where(mc == seq, 0.0, k)
        k_ref[o] = k
        s_ref[o] += jnp.sum(jnp.abs(k), axis=0, keepdims=True)


def _filters(seq, w1, b1, w2, b2, w3, b3, freq, tr=256):
    n = 2 * seq
    tr = min(tr, n)
    bands = (POS_EMB_DIM - 1) // 2
    hid = LANE
    w1p = jnp.zeros((LANE, hid), F32).at[:POS_EMB_DIM, :FILTER_HIDDEN].set(w1.astype(F32))
    b1p = jnp.zeros((1, hid), F32).at[0, :FILTER_HIDDEN].set(b1.astype(F32))
    w2p = jnp.zeros((hid, hid), F32).at[:FILTER_HIDDEN, :FILTER_HIDDEN].set(w2.astype(F32))
    b2p = jnp.zeros((1, hid), F32).at[0, :FILTER_HIDDEN].set(b2.astype(F32))
    w3p = jnp.zeros((hid, w3.shape[1]), F32).at[:FILTER_HIDDEN].set(w3.astype(F32))
    b3p = b3.astype(F32)[None]
    frp = jnp.zeros((2, hid), F32).at[:, :FILTER_HIDDEN].set(freq.astype(F32))
    f = jnp.linspace(1e-4, bands - 1, bands, dtype=F32)
    fq = jnp.zeros((1, LANE), F32).at[0, 1:1 + bands].set(f).at[0, 1 + bands:1 + 2 * bands].set(f)
    deltas = jnp.abs(jnp.linspace(math.log(DECAY_TARGET) / FAST_DECAY_PCT,
                                  math.log(DECAY_TARGET) / SLOW_DECAY_PCT, BRANCH_W, dtype=F32))[None]
    full = lambda a: pl.BlockSpec(a.shape, lambda i: (0,) * a.ndim)
    args = (w1p, b1p, w2p, b2p, w3p, b3p, frp, fq, deltas)
    return pl.pallas_call(
        functools.partial(_filt_kernel, seq=seq, tr=tr),
        grid=(n // tr,),
        in_specs=[full(a) for a in args],
        out_specs=[pl.BlockSpec((HYENA_ORDER, tr, BRANCH_W), lambda i: (0, i, 0)),
                   pl.BlockSpec((HYENA_ORDER, 1, BRANCH_W), lambda i: (0, 0, 0))],
        out_shape=[jax.ShapeDtypeStruct((HYENA_ORDER, n, BRANCH_W), F32),
                   jax.ShapeDtypeStruct((HYENA_ORDER, 1, BRANCH_W), F32)],
        compiler_params=_cparams(("arbitrary",)),
        name="hyena_filter",
    )(*args)


def _dft_tables(seq):
    n1 = 2 * seq // FFT_N2
    n1h = seq // FFT_N2
    n = 2 * seq
    k1 = jnp.arange(n1, dtype=jnp.int32)
    th = (2.0 * math.pi / n1) * ((k1[:, None] * k1[None, :]) % n1).astype(F32)
    c1, s1 = jnp.cos(th), jnp.sin(th)
    ch, sh = c1[:, :n1h], s1[:, :n1h]
    f1 = jnp.concatenate([jnp.concatenate([ch, sh], 1), jnp.concatenate([-sh, ch], 1)], 0)
    f1_real = jnp.concatenate([c1, -s1], 0)
    cht, sht = ch.T / n, sh.T / n
    g1 = jnp.concatenate([jnp.concatenate([cht, -sht], 1), jnp.concatenate([sht, cht], 1)], 0)
    n2 = jnp.arange(FFT_N2, dtype=jnp.int32)
    kk = k1[:, None, None] + n1 * n2[None, :, None]
    ph = (2.0 * math.pi / n) * ((kk * n2[None, None, :]) % n).astype(F32)
    c2, s2 = jnp.cos(ph), jnp.sin(ph)
    m2 = jnp.concatenate([jnp.concatenate([c2, s2], 2), jnp.concatenate([-s2, c2], 2)], 1)
    return f1, f1_real, g1, m2, jnp.swapaxes(m2, 1, 2)


def _s1_kernel(z_ref, f_ref, o_ref):
    res = _dot3(f_ref[...], z_ref[0])
    half = res.shape[0] // 2
    o_ref[0, 0] = res[:half]
    o_ref[0, 1] = res[half:]


def _fft_s1(z, f, n2t=8):
    bx, lz, c = z.shape
    r = lz // FFT_N2
    n1 = f.shape[0] // 2
    w = n2t * c
    zz = z.reshape(bx, r, FFT_N2 * c)
    return pl.pallas_call(
        _s1_kernel,
        grid=(bx, FFT_N2 // n2t),
        in_specs=[pl.BlockSpec((1, r, w), lambda p, j: (p, 0, j)),
                  pl.BlockSpec(f.shape, lambda p, j: (0, 0))],
        out_specs=pl.BlockSpec((1, 2, n1, w), lambda p, j: (p, 0, 0, j)),
        out_shape=jax.ShapeDtypeStruct((bx, 2, n1, FFT_N2 * c), F32),
        compiler_params=_cparams(("parallel", "parallel")),
        name="fft_stage1",
    )(zz, f)


def _mid_fwd_kernel(a_ref, m_ref, o_ref):
    k1t = a_ref.shape[2]
    h = FFT_N2
    for j in range(k1t):
        a = jnp.concatenate([a_ref[0, 0, j], a_ref[0, 1, j]], axis=0)
        x = _dot3(m_ref[j], a)
        o_ref[0, 0, j] = x[:h]
        o_ref[0, 1, j] = x[h:]


def _fft_mid_fwd(a, m2, k1t=8):
    p, _, n1, wc = a.shape
    c = wc // FFT_N2
    k1t = min(k1t, n1)
    a5 = a.reshape(p, 2, n1, FFT_N2, c)
    blk = (1, 2, k1t, FFT_N2, c)
    return pl.pallas_call(
        _mid_fwd_kernel,
        grid=(n1 // k1t, p),
        in_specs=[pl.BlockSpec(blk, lambda k, q: (q, 0, k, 0, 0)),
                  pl.BlockSpec((k1t, 2 * FFT_N2, 2 * FFT_N2), lambda k, q: (k, 0, 0))],
        out_specs=pl.BlockSpec(blk, lambda k, q: (q, 0, k, 0, 0)),
        out_shape=jax.ShapeDtypeStruct(a5.shape, F32),
        compiler_params=_cparams(("parallel", "parallel")),
        name="fft_mid_filter",
    )(a5, m2)


ROW_PAD = 8


def _pack2(hi, lo):
    hb = lax.bitcast_convert_type(hi.astype(BF16).astype(F32), jnp.uint32)
    lb = lax.bitcast_convert_type(lo.astype(BF16).astype(F32), jnp.uint32)
    return hb | (lb >> 16)


def _unpack2(w):
    return (lax.bitcast_convert_type(w & jnp.uint32(0xFFFF0000), F32),
            lax.bitcast_convert_type(w << 16, F32))


def _rows(start, size, stride):
    return pl.ds(start, size) if stride is None else pl.ds(start, size, stride=stride)


def _ld(ref, start, size, stride=None):
    parts = [ref[l, _rows(start, size, stride), :] for l in range(ref.shape[0])]
    return parts[0] if len(parts) == 1 else jnp.concatenate(parts, axis=1)


def _st(ref, start, size, val, stride=None):
    for l in range(ref.shape[0]):
        ref[l, _rows(start, size, stride), :] = val[:, l * LANE:(l + 1) * LANE]


def _time_tile(ref, part, i, n_tiles, conv):
    h = FFT_N2
    r0 = pl.multiple_of(i * h, h)
    cur = ref[part, pl.ds(r0, h), :].astype(F32)
    if conv is None:
        return cur
    w_ref, b_ref = conv
    lo = pl.multiple_of(jnp.maximum(r0 - HALO, 0), HALO)
    hi = pl.multiple_of(jnp.minimum(r0 + h, n_tiles * h - HALO), HALO)
    prev = ref[part, pl.ds(lo, HALO), :].astype(F32)[HALO - 1:HALO]
    nxt = ref[part, pl.ds(hi, HALO), :].astype(F32)[0:1]
    prev = jnp.where(i > 0, prev, 0.0)
    nxt = jnp.where(i < n_tiles - 1, nxt, 0.0)
    row = lax.broadcasted_iota(jnp.int32, cur.shape, 0)
    up = jnp.where(row == 0, prev, pltpu.roll(cur, 1, 0))
    un = jnp.where(row == h - 1, nxt, pltpu.roll(cur, h - 1, 0))
    w = w_ref[...]
    return b_ref[...] + up * w[0:1] + cur * w[1:2] + un * w[2:3]


def _lc_kernel(v_ref, x_ref, f1_ref, g1_ref, m2_ref, m2t_ref, kf_ref, sc_ref, d_ref, cwv_ref, cbv_ref,
               cwx_ref, cbx_ref, o_ref, zst, a3, *, n1, n1h, k1t, nk, grp, conv_v, conv_x):
    k = pl.program_id(1)
    cb = v_ref.shape[2]
    zp = FFT_N2 + ROW_PAD
    ap = n1 + ROW_PAD
    h = FFT_N2
    vconv = (cwv_ref, cbv_ref) if conv_v else None
    xconv = (cwx_ref, cbx_ref) if conv_x else None

    @pl.when(k == 0)
    def _():
        def stage(i, c):
            w = _pack2(_time_tile(v_ref, 0, i, n1h, vconv), _time_tile(v_ref, 1, i, n1h, vconv))
            _st(zst, pl.multiple_of(i * zp, 8), h, w)
            return c
        lax.fori_loop(0, n1h, stage, 0)

        def level1(t, c):
            zs = []
            for u in range(grp):
                zr, zi = _unpack2(_ld(zst, t * grp + u, n1h, stride=zp))
                zs.append(jnp.concatenate([zr, zi], axis=0).astype(BF16))
            res = jnp.dot(f1_ref[...], jnp.concatenate(zs, axis=1), preferred_element_type=F32)
            for u in range(grp):
                r0 = pl.multiple_of((t * grp + u) * ap, 8)
                _st(a3, r0, n1, _pack2(res[:n1, u * cb:(u + 1) * cb], res[n1:, u * cb:(u + 1) * cb]))
            return c
        lax.fori_loop(0, h // grp, level1, 0, unroll=2)

    for jj in range(k1t):
        k1 = k * k1t + jj
        ar, ai = _unpack2(_ld(a3, k1, h, stride=ap))
        a = jnp.concatenate([ar, ai], axis=0).astype(BF16)
        x = jnp.dot(m2_ref[jj], a, preferred_element_type=F32)
        xr, xi = x[:h], x[h:]
        kr, ki = kf_ref[0, 0, jj], kf_ref[0, 1, jj]
        y = jnp.concatenate([xr * kr - xi * ki, xr * ki + xi * kr], axis=0).astype(BF16)
        b = jnp.dot(m2t_ref[jj], y, preferred_element_type=F32)
        _st(a3, k1, h, _pack2(b[:h], b[h:]), stride=ap)

    @pl.when(k == nk - 1)
    def _():
        def level3(t, c):
            bms = []
            for u in range(grp):
                r0 = pl.multiple_of((t * grp + u) * ap, 8)
                br, bi = _unpack2(_ld(a3, r0, n1))
                bms.append(jnp.concatenate([br, bi], axis=0).astype(BF16))
            y = jnp.dot(g1_ref[...], jnp.concatenate(bms, axis=1), preferred_element_type=F32)
            for u in range(grp):
                r0 = pl.multiple_of((t * grp + u) * ap, 8)
                _st(a3, r0, n1, lax.bitcast_convert_type(y[:, u * cb:(u + 1) * cb], jnp.uint32))
            return c
        lax.fori_loop(0, h // grp, level3, 0, unroll=2)

        def gate(i, c):
            r0 = pl.multiple_of(i * h, h)
            vs = _unpack2(_ld(zst, pl.multiple_of(i * zp, 8), h))
            for part in range(2):
                yv = lax.bitcast_convert_type(_ld(a3, part * n1h + i, h, stride=ap), F32)
                xx = _time_tile(x_ref, part, i, n1h, xconv)
                o_ref[part, pl.ds(r0, h), :] = (xx * (yv * sc_ref[...] + vs[part] * d_ref[...])).astype(o_ref.dtype)
            return c
        lax.fori_loop(0, n1h, gate, 0)


LC_VMEM_ROWS = 4096


def _longconv(v, v_seg, x, x_seg, conv_w, conv_b, kf, order, inv_norm, skip, tables, k1t=8):
    f1, g1, m2, m2t = tables
    bx, seq, _ = v.shape
    c = BRANCH_W
    cb = 2 * LANE if seq <= LC_VMEM_ROWS else LANE
    n1 = 2 * seq // FFT_N2
    n1h = seq // FFT_N2
    k1t = min(k1t, n1)
    nk = n1 // k1t
    ncb = c // cb
    grp = 4 if n1 >= FFT_N2 else 8
    hy0 = H_HY * 256 // cb

    def io(seg):
        col = 0 if seg is None else hy0 + seg * ncb
        return pl.BlockSpec((2, seq, cb), lambda g, k: (g // ncb, 0, col + g % ncb))

    def cvec(rows, seg):
        col = 0 if seg is None else seg * ncb
        return pl.BlockSpec((rows, cb), lambda g, k: (0, col + g % ncb))

    vec = pl.BlockSpec((1, cb), lambda g, k: (0, g % ncb))
    mat = pl.BlockSpec((k1t, 2 * FFT_N2, 2 * FFT_N2), lambda g, k: (k, 0, 0))
    return pl.pallas_call(
        functools.partial(_lc_kernel, n1=n1, n1h=n1h, k1t=k1t, nk=nk, grp=grp,
                          conv_v=v_seg is not None, conv_x=x_seg is not None),
        grid=((bx // 2) * ncb, nk),
        in_specs=[io(v_seg), io(x_seg),
                  pl.BlockSpec(f1.shape, lambda g, k: (0, 0)),
                  pl.BlockSpec(g1.shape, lambda g, k: (0, 0)),
                  mat, mat,
                  pl.BlockSpec((1, 2, k1t, FFT_N2, cb), lambda g, k: (order, 0, k, 0, g % ncb)),
                  vec, vec,
                  cvec(SHORT_CONV, v_seg), cvec(1, v_seg), cvec(SHORT_CONV, x_seg), cvec(1, x_seg)],
        out_specs=io(None),
        out_shape=jax.ShapeDtypeStruct((bx, seq, c), BF16),
        scratch_shapes=[pltpu.VMEM((cb // LANE, n1h * (FFT_N2 + ROW_PAD), LANE), jnp.uint32),
                        pltpu.VMEM((cb // LANE, FFT_N2 * (n1 + ROW_PAD), LANE), jnp.uint32)],
        compiler_params=_cparams(("arbitrary", "arbitrary")),
        name="hyena_longconv",
    )(v, x, f1, g1, m2, m2t, kf, inv_norm.astype(F32).reshape(1, c), skip.astype(F32).reshape(1, c),
      conv_w, conv_b, conv_w, conv_b)


def _hyena(h3, conv_w, conv_b, filt, skip):
    seq = h3.shape[1]
    f1, f1_real, g1, m2, m2t = _dft_tables(seq)
    taps, asum = _filters(seq, *filt)
    kf = _fft_mid_fwd(_fft_s1(taps, f1_real), m2)
    inv_norm = 1.0 / asum[:, 0]
    tables = (f1.astype(BF16), g1.astype(BF16), m2.astype(BF16), m2t.astype(BF16))
    z1 = _longconv(h3, 0, h3, 1, conv_w, conv_b, kf, 0, inv_norm[0], skip[0], tables)
    return _longconv(z1, None, h3, 2, conv_w, conv_b, kf, 1, inv_norm[1], skip[1], tables)


def _mix_kernel(x_ref, a_ref, hb_ref, c_ref, g0_ref, g1_ref, g2_ref, wb_ref, wo_ref,
                lig_ref, lib_ref, l1g_ref, l1b_ref, wrh_ref, wrl_ref, br_ref, x1_ref, x1b_ref, rt_ref, cnt_ref):
    acc = None
    for n, (br, gr) in enumerate(((a_ref, g0_ref), (hb_ref, g1_ref), (c_ref, g2_ref))):
        proj = jnp.dot(br[...], wb_ref[n], preferred_element_type=F32)
        term = jax.nn.sigmoid(gr[...].astype(F32)) * proj
        acc = term if acc is None else acc + term
    mix = jnp.dot(acc.astype(BF16), wo_ref[...], preferred_element_type=F32)
    xn = _ln(x_ref[...], lig_ref[...], lib_ref[...])
    x1 = _ln(DN_ALPHA * xn + mix, l1g_ref[...], l1b_ref[...])
    x1_ref[...] = x1
    xh = x1.astype(BF16)
    x1b_ref[...] = xh
    xl = (x1 - xh.astype(F32)).astype(BF16)
    nt = (((1,), (1,)), ((), ()))
    work = lax.dot_general(wrh_ref[...], xh, nt, preferred_element_type=F32)
    work = work + lax.dot_general(wrh_ref[...], xl, nt, preferred_element_type=F32)
    work = work + lax.dot_general(wrl_ref[...], xh, nt, preferred_element_type=F32)
    work = work + br_ref[...]

    tm = work.shape[1]
    erow = lax.broadcasted_iota(jnp.int32, work.shape, 0)
    ids, vals, hots = [], [], []
    for _ in range(TOP_K):
        mx = jnp.max(work, axis=0, keepdims=True)
        idx = jnp.min(jnp.where(work == mx, erow, N_EXPERTS), axis=0, keepdims=True)
        hot = erow == idx
        ids.append(idx)
        vals.append(mx)
        hots.append(hot)
        work = jnp.where(hot, F32(-3e38), work)
    exps = [jnp.exp(v - vals[0]) for v in vals]
    den = exps[0]
    for e in exps[1:]:
        den = den + e
    chosen = hots[0]
    for hot in hots[1:]:
        chosen = chosen | hot
    chosen = chosen.astype(F32)

    @pl.when(pl.program_id(0) == 0)
    def _():
        cnt_ref[...] = jnp.zeros_like(cnt_ref)

    r_i = lax.broadcasted_iota(jnp.int32, (tm, tm), 0)
    c_i = lax.broadcasted_iota(jnp.int32, (tm, tm), 1)
    earlier = (r_i < c_i).astype(BF16)
    before = jnp.dot(chosen.astype(BF16), earlier, preferred_element_type=F32)
    before = before + jnp.tile(cnt_ref[...], (1, tm // LANE))
    cnt_ref[...] += jnp.sum(chosen, axis=1, keepdims=True)
    rows = [i.astype(F32) for i in ids] + [e / den for e in exps]
    rows += [jnp.sum(jnp.where(hot, before, 0.0), axis=0, keepdims=True) for hot in hots]
    rows.append(jnp.zeros((rt_ref.shape[0] - len(rows), tm), F32))
    rt_ref[...] = jnp.concatenate(rows, axis=0)


ROUTE_ROWS = 16


def _mix(x2, a, hb, c, h2, wb, wo, lig, lib, l1g, l1b, wr, br, tm=512):
    T, D = x2.shape
    wrt = wr.T
    wrh = wrt.astype(BF16)
    wrl = (wrt - wrh.astype(F32)).astype(BF16)
    br = br.reshape(N_EXPERTS, 1)
    row = lambda wdt, col: pl.BlockSpec((tm, wdt), lambda i: (i, col))
    const = lambda arr: pl.BlockSpec(arr.shape, lambda i: (0,) * arr.ndim)
    return pl.pallas_call(
        _mix_kernel,
        grid=(T // tm,),
        in_specs=[row(D, 0), row(BRANCH_W, 0), row(BRANCH_W, 0), row(BRANCH_W, 0),
                  row(D, H_G), row(D, H_G + 1), row(D, H_G + 2),
                  const(wb), const(wo), const(lig), const(lib), const(l1g), const(l1b), const(wrh), const(wrl), const(br)],
        out_specs=[row(D, 0), row(D, 0), pl.BlockSpec((ROUTE_ROWS, tm), lambda i: (0, i)),
                   pl.BlockSpec((N_EXPERTS, LANE), lambda i: (0, 0))],
        out_shape=[jax.ShapeDtypeStruct((T, D), F32), jax.ShapeDtypeStruct((T, D), BF16),
                   jax.ShapeDtypeStruct((ROUTE_ROWS, T), F32), jax.ShapeDtypeStruct((N_EXPERTS, LANE), F32)],
        compiler_params=_cparams(("arbitrary",)),
        name="mix",
    )(x2, a, hb, c, h2, h2, h2, wb, wo, lig, lib, l1g, l1b, wrh, wrl, br)


def _ffn_kernel(ce_ref, nu_ref, x_ref, wgu_ref, bgu_ref, wd_ref, bd_ref, o_ref):
    del ce_ref
    used = pl.program_id(0) < nu_ref[0]

    @pl.when(used)
    def _():
        gu = jnp.dot(x_ref[...], wgu_ref[0], preferred_element_type=F32) + bgu_ref[0]
        f = gu.shape[1] // 2
        g = jnp.minimum(gu[:, :f], SWIGLU_LIMIT)
        u = jnp.clip(gu[:, f:], -SWIGLU_LIMIT, SWIGLU_LIMIT)
        hmid = (u + 1.0) * (g * jax.nn.sigmoid(g * SWIGLU_ALPHA))
        y = jnp.dot(hmid.astype(BF16), wd_ref[0].astype(BF16), preferred_element_type=F32) + bd_ref[0]
        o_ref[...] = y.astype(o_ref.dtype)

    @pl.when(jnp.logical_not(used))
    def _():
        o_ref[...] = jnp.zeros_like(o_ref)


def _ffn(chunk_e, n_used, x_slots, wgu, bgu, wd, bd):
    P, D = x_slots.shape
    F = wd.shape[1]
    n_chunks = P // MOE_TILE
    wspec = lambda a, b: pl.BlockSpec((1, a, b), lambda i, ce, nu: (ce[i], 0, 0))
    grid_spec = pltpu.PrefetchScalarGridSpec(
        num_scalar_prefetch=2,
        grid=(n_chunks,),
        in_specs=[pl.BlockSpec((MOE_TILE, D), lambda i, ce, nu: (i, 0)),
                  wspec(D, 2 * F), wspec(1, 2 * F), wspec(F, D), wspec(1, D)],
        out_specs=pl.BlockSpec((MOE_TILE, D), lambda i, ce, nu: (i, 0)),
    )
    return pl.pallas_call(
        _ffn_kernel,
        grid_spec=grid_spec,
        out_shape=jax.ShapeDtypeStruct((P, D), BF16),
        compiler_params=_cparams(("arbitrary",)),
        name="moe_ffn",
    )(chunk_e, n_used, x_slots, wgu, bgu, wd, bd)


DEINT_W = 256


def _deint_kernel(w_ref, s_ref, o_ref):
    w = w_ref[0].astype(BF16)
    f = w.shape[1] // 2
    half = DEINT_W // 2
    for blk in range(w.shape[1] // DEINT_W):
        r = jnp.dot(w[:, blk * DEINT_W:(blk + 1) * DEINT_W], s_ref[...], preferred_element_type=F32)
        o_ref[0, :, blk * half:(blk + 1) * half] = r[:, :half].astype(o_ref.dtype)
        o_ref[0, :, f + blk * half:f + (blk + 1) * half] = r[:, half:].astype(o_ref.dtype)


def _deinterleave(w, tr=512):
    E, D, F2 = w.shape
    i = jnp.arange(DEINT_W, dtype=jnp.int32)
    src = jnp.where(i < DEINT_W // 2, 2 * i, 2 * (i - DEINT_W // 2) + 1)
    sel = (i[:, None] == src[None, :]).astype(BF16)
    return pl.pallas_call(
        _deint_kernel,
        grid=(E, D // tr),
        in_specs=[pl.BlockSpec((1, tr, F2), lambda e, r: (e, r, 0)),
                  pl.BlockSpec((DEINT_W, DEINT_W), lambda e, r: (0, 0))],
        out_specs=pl.BlockSpec((1, tr, F2), lambda e, r: (e, r, 0)),
        out_shape=jax.ShapeDtypeStruct((E, D, F2), BF16),
        compiler_params=_cparams(("parallel", "parallel")),
        name="deinterleave_gate_up",
    )(w, sel)


def _final_kernel(x1_ref, y_ref, gt_ref, g_ref, b_ref, o_ref):
    gt = gt_ref[...]
    moe = y_ref[0].astype(F32) * gt[:, 0:1]
    for k in range(1, TOP_K):
        moe = moe + y_ref[k].astype(F32) * gt[:, k:k + 1]
    o_ref[...] = _ln(DN_ALPHA * x1_ref[...] + moe, g_ref[...], b_ref[...])


def _final(x1, yg, gates, g, b, tm=512):
    T, D = x1.shape
    return pl.pallas_call(
        _final_kernel,
        grid=(T // tm,),
        in_specs=[pl.BlockSpec((tm, D), lambda i: (i, 0)),
                  pl.BlockSpec((TOP_K, tm, D), lambda i: (0, i, 0)),
                  pl.BlockSpec((tm, TOP_K), lambda i: (i, 0)),
                  pl.BlockSpec((1, D), lambda i: (0, 0)),
                  pl.BlockSpec((1, D), lambda i: (0, 0))],
        out_specs=pl.BlockSpec((tm, D), lambda i: (i, 0)),
        out_shape=jax.ShapeDtypeStruct((T, D), F32),
        compiler_params=_cparams(("parallel",)),
        name="final_ln",
    )(x1, yg, gates, g, b)


def _moe(x1, x1b, route, cnt, ffn_w, ln2_g, ln2_b):
    T, D = x1.shape
    A = T * TOP_K
    i32 = jnp.int32
    ids = route[:TOP_K].astype(i32)
    gates = route[TOP_K:2 * TOP_K]
    rank = route[2 * TOP_K:3 * TOP_K].astype(i32)
    counts = cnt[:, 0].astype(i32)
    pcounts = (counts + MOE_TILE - 1) // MOE_TILE * MOE_TILE
    pends = jnp.cumsum(pcounts)
    pstarts = pends - pcounts
    n_chunks = -(-A // MOE_TILE) + N_EXPERTS
    P = n_chunks * MOE_TILE
    e3 = jnp.arange(N_EXPERTS, dtype=i32)[:, None, None]
    pos = rank + jnp.sum(jnp.where(ids[None] == e3, pstarts[:, None, None], 0), axis=0)
    padc = jnp.concatenate([pcounts - counts, (P - pends[-1])[None]])
    pad_lo = jnp.concatenate([pstarts + counts, pends[-1:]])
    cum = jnp.cumsum(padc)
    j = jnp.arange(P - A, dtype=i32)
    run = jnp.sum((j[None, :] >= cum[:, None]).astype(i32), axis=0)
    hot = run[None, :] == jnp.arange(N_EXPERTS + 1, dtype=i32)[:, None]
    pad_key = j + jnp.sum(jnp.where(hot, (pad_lo - (cum - padc))[:, None], 0), axis=0)
    keys = jnp.concatenate([pos.reshape(A), pad_key])
    toks = jnp.concatenate([jnp.tile(jnp.arange(T, dtype=i32), TOP_K), jnp.zeros((P - A,), i32)])
    _, slot_tok = lax.sort((keys, toks), num_keys=1)
    chunk_lo = jnp.arange(n_chunks, dtype=i32) * MOE_TILE
    chunk_e = jnp.minimum(jnp.sum((chunk_lo[None, :] >= pends[:, None]).astype(i32), axis=0), N_EXPERTS - 1)
    x_slots = x1b[slot_tok]
    y_slots = _ffn(chunk_e, (pends[-1:] // MOE_TILE).astype(i32), x_slots, *ffn_w)
    yg = y_slots[pos.reshape(A)].reshape(TOP_K, T, D)
    return _final(x1, yg, gates.T, ln2_g, ln2_b)


def _mixer_half(x, mem, p):
    B, L, D = x.shape
    x2 = x.reshape(B * L, D)
    h2 = _inproj(x2, p["ln_in_g"], p["ln_in_b"], p["w_in"])
    h3 = h2.reshape(B, L, IN_W)
    a = _wattn(h3, p["sink"], p["bias"])
    hb = _hyena(h3, p["conv_w"], p["conv_b"], p["filt"], p["skip"])
    c = _memattn(h3, mem, p["w_mem_kv"])
    return _mix(x2, a.reshape(B * L, -1), hb.reshape(B * L, -1), c.reshape(B * L, -1), h2,
                p["w_branch"], p["w_out"], p["ln_in_g"], p["ln_in_b"], p["ln1_g"], p["ln1_b"],
                p["w_router"], p["b_router"])


def _trunk(x, mem, p):
    return _moe(*_mixer_half(x, mem, p), p["ffn"], p["ln2_g"], p["ln2_b"]).reshape(x.shape)


def _prep(ln_in_g, ln_in_b, rel_bias, w_in, attn_sink, conv_w, conv_b, filt_w1, filt_b1, filt_w2, filt_b2,
          filt_w3, filt_b3, filt_freq, hyena_skip, w_mem_kv, w_branch, w_out, ln1_g, ln1_b, w_router, b_router,
          w_gate_up, b_gate_up, w_down, b_down, ln2_g, ln2_b):
    w = w_in
    g_lo = Q_W + 2 * KV_W + 3 * BRANCH_W + BRANCH_W
    hy_lo = Q_W + 2 * KV_W
    assert HEAD_DIM ** -0.5 == 2.0 ** -3
    w_p = jnp.concatenate([w[:, g_lo:], w[:, :Q_W] * (HEAD_DIM ** -0.5), w[:, hy_lo + 3 * BRANCH_W:g_lo],
                           w[:, hy_lo:hy_lo + 3 * BRANCH_W], w[:, Q_W:hy_lo]], axis=1).astype(BF16)
    row = lambda v: v.astype(F32)[None]
    wr = w_router.astype(F32)
    br = b_router.astype(F32)
    return dict(
        ln_in_g=row(ln_in_g), ln_in_b=row(ln_in_b), w_in=w_p,
        sink=attn_sink.astype(F32), bias=_bias_table(rel_bias),
        conv_w=conv_w.astype(F32), conv_b=row(conv_b),
        filt=(filt_w1, filt_b1, filt_w2, filt_b2, filt_w3, filt_b3, filt_freq), skip=hyena_skip,
        w_mem_kv=w_mem_kv.astype(BF16), w_branch=w_branch.astype(BF16), w_out=w_out.astype(BF16),
        ln1_g=row(ln1_g), ln1_b=row(ln1_b), w_router=wr, b_router=br,
        ffn=(_deinterleave(w_gate_up),
             jnp.concatenate([b_gate_up[:, 0::2], b_gate_up[:, 1::2]], axis=1)[:, None, :].astype(F32),
             w_down, b_down[:, None, :].astype(F32)),
        ln2_g=row(ln2_g), ln2_b=row(ln2_b),
    )


def kernel(x_prompt, x_sample, mem_prompt, mem_sample, ln_in_g, ln_in_b, rel_bias, w_in, attn_sink, conv_w, conv_b, filt_w1, filt_b1, filt_w2, filt_b2, filt_w3, filt_b3, filt_freq, hyena_skip, w_mem_kv, w_branch, w_out, ln1_g, ln1_b, w_router, b_router, w_gate_up, b_gate_up, w_down, b_down, ln2_g, ln2_b):
    p = _prep(ln_in_g, ln_in_b, rel_bias, w_in[0], attn_sink[0], conv_w[0], conv_b[0], filt_w1[0], filt_b1[0],
              filt_w2[0], filt_b2[0], filt_w3[0], filt_b3[0], filt_freq[0], hyena_skip[0], w_mem_kv[0],
              w_branch[0], w_out[0], ln1_g[0], ln1_b[0], w_router[0], b_router[0], w_gate_up[0], b_gate_up[0],
              w_down[0], b_down[0], ln2_g[0], ln2_b[0])
    halves = [_mixer_half(x, mem, p) for x, mem in ((x_prompt, mem_prompt), (x_sample, mem_sample))]
    outs = [_moe(*half, p["ffn"], p["ln2_g"], p["ln2_b"]) for half in halves]
    return (outs[0].reshape(x_prompt.shape), outs[1].reshape(x_sample.shape))
```

```python
import functools
import math

import jax
import jax.numpy as jnp
from jax import lax
from jax.experimental import pallas as pl
from jax.experimental.pallas import tpu as pltpu

F32 = jnp.float32
BF16 = jnp.bfloat16
HI = lax.Precision.HIGHEST

D_MODEL = 1024
BRANCH_W = 512
N_Q_HEADS = 8
N_KV_HEADS = 2
HEAD_DIM = 64
WINDOW = 128
BLOCK = 128
N_BUCKETS = 32
MAX_DISTANCE = 128
HYENA_ORDER = 2
SHORT_CONV = 3
POS_EMB_DIM = 33
FILTER_HIDDEN = 64
DECAY_TARGET = 1e-2
FAST_DECAY_PCT = 0.3
SLOW_DECAY_PCT = 1.5
N_MEM_HEADS = 4
MEM_HEAD_DIM = BRANCH_W // N_MEM_HEADS
N_BRANCHES = 3
N_EXPERTS = 32
TOP_K = 4
D_EXPERT = 1024
SWIGLU_LIMIT = 7.0
SWIGLU_ALPHA = 1.702
LN_EPS = 1e-5
DEPTH = 1
DN_ALPHA = (2 * DEPTH) ** 0.25

Q_W = N_Q_HEADS * HEAD_DIM
KV_W = N_KV_HEADS * HEAD_DIM
IN_W = Q_W + 2 * KV_W + 3 * BRANCH_W + BRANCH_W + N_BRANCHES * D_MODEL

LANE = 128
V7X_VMEM_BYTES = 64 * 1024 * 1024
VMEM_LIMIT = V7X_VMEM_BYTES - 12 * 1024 * 1024

OFF_G = 0
OFF_Q = OFF_G + N_BRANCHES * D_MODEL
OFF_QM = OFF_Q + Q_W
OFF_HY = OFF_QM + BRANCH_W
OFF_K = OFF_HY + 3 * BRANCH_W
OFF_V = OFF_K + KV_W
H_G = OFF_G // D_MODEL
H_Q = OFF_Q // Q_W
H_QM = OFF_QM // BRANCH_W
H_K = OFF_K // KV_W
H_V = OFF_V // KV_W

FFT_N2 = 128
MOE_TILE = 512


def _cparams(sem):
    return pltpu.CompilerParams(dimension_semantics=sem, vmem_limit_bytes=VMEM_LIMIT)


def _split(x):
    hi = x.astype(BF16)
    return hi, (x - hi.astype(F32)).astype(BF16)


def _dot3(a, b):
    ah, al = _split(a)
    bh, bl = _split(b)
    out = jnp.dot(ah, bh, preferred_element_type=F32)
    out = out + jnp.dot(al, bh, preferred_element_type=F32)
    return out + jnp.dot(ah, bl, preferred_element_type=F32)


def _ln(x, g, b):
    mu = jnp.mean(x, axis=-1, keepdims=True)
    xc = x - mu
    var = jnp.mean(xc * xc, axis=-1, keepdims=True)
    return xc * lax.rsqrt(var + LN_EPS) * g + b


def _inproj_kernel(x_ref, g_ref, b_ref, w_ref, o_ref):
    xn = _ln(x_ref[...], g_ref[...], b_ref[...])
    o_ref[...] = jnp.dot(xn.astype(BF16), w_ref[...], preferred_element_type=F32).astype(o_ref.dtype)


def _inproj(x2, ln_g, ln_b, w_p, tm=512):
    T, D = x2.shape
    N = w_p.shape[1]
    return pl.pallas_call(
        _inproj_kernel,
        grid=(T // tm,),
        in_specs=[
            pl.BlockSpec((tm, D), lambda i: (i, 0)),
            pl.BlockSpec((1, D), lambda i: (0, 0)),
            pl.BlockSpec((1, D), lambda i: (0, 0)),
            pl.BlockSpec((D, N), lambda i: (0, 0)),
        ],
        out_specs=pl.BlockSpec((tm, N), lambda i: (i, 0)),
        out_shape=jax.ShapeDtypeStruct((T, N), BF16),
        compiler_params=_cparams(("parallel",)),
        name="inproj",
    )(x2, ln_g, ln_b, w_p)


def _wattn_kernel(sink_ref, q_ref, kp_ref, kc_ref, kn_ref, vp_ref, vc_ref, vn_ref, bias_ref, o_ref, *, seq):
    n = pl.program_id(1)
    q = q_ref[0]
    kcat = jnp.concatenate([kp_ref[0], kc_ref[0], kn_ref[0]], axis=0)
    vcat = jnp.concatenate([vp_ref[0], vc_ref[0], vn_ref[0]], axis=0)
    qi = lax.broadcasted_iota(jnp.int32, (BLOCK, 3 * BLOCK), 0)
    kj = lax.broadcasted_iota(jnp.int32, (BLOCK, 3 * BLOCK), 1)
    rel = kj - BLOCK - qi
    kpos = (n - 1) * BLOCK + kj
    valid = (jnp.abs(rel) <= WINDOW) & (kpos >= 0) & (kpos < seq)
    group = N_Q_HEADS // N_KV_HEADS
    scores = []
    for h in range(N_Q_HEADS):
        g = h // group
        qh = q[:, h * HEAD_DIM:(h + 1) * HEAD_DIM]
        kh = kcat[:, g * HEAD_DIM:(g + 1) * HEAD_DIM]
        scores.append(lax.dot_general(qh, kh, (((1,), (1,)), ((), ())), preferred_element_type=F32))
    probs, denoms = [], []
    for h in range(N_Q_HEADS):
        s = jnp.where(valid, scores[h] + bias_ref[h], F32(-1e30))
        sk = sink_ref[h]
        m = jnp.maximum(jnp.max(s, axis=-1, keepdims=True), sk)
        p = jnp.exp(s - m)
        denoms.append(jnp.sum(p, axis=-1, keepdims=True) + jnp.exp(sk - m))
        probs.append(p.astype(BF16))
    outs = []
    for h in range(N_Q_HEADS):
        g = h // group
        vh = vcat[:, g * HEAD_DIM:(g + 1) * HEAD_DIM]
        outs.append(jnp.dot(probs[h], vh, preferred_element_type=F32) / denoms[h])
    o_ref[0] = jnp.concatenate(outs, axis=1).astype(o_ref.dtype)


def _wattn(h3, sink, bias):
    B, L, _ = h3.shape
    nb = L // BLOCK
    kv_spec = lambda col, shift: pl.BlockSpec(
        (1, BLOCK, KV_W), lambda b, n: (b, jnp.clip(n + shift, 0, nb - 1), col))
    return pl.pallas_call(
        functools.partial(_wattn_kernel, seq=L),
        grid=(B, nb),
        in_specs=[
            pl.BlockSpec(memory_space=pltpu.SMEM),
            pl.BlockSpec((1, BLOCK, Q_W), lambda b, n: (b, n, H_Q)),
            kv_spec(H_K, -1), kv_spec(H_K, 0), kv_spec(H_K, 1),
            kv_spec(H_V, -1), kv_spec(H_V, 0), kv_spec(H_V, 1),
            pl.BlockSpec((N_Q_HEADS, BLOCK, 3 * BLOCK), lambda b, n: (0, 0, 0)),
        ],
        out_specs=pl.BlockSpec((1, BLOCK, Q_W), lambda b, n: (b, n, 0)),
        out_shape=jax.ShapeDtypeStruct((B, L, Q_W), BF16),
        compiler_params=_cparams(("parallel", "parallel")),
        name="wattn",
    )(sink, h3, h3, h3, h3, h3, h3, h3, bias)


def _t5_bucket(rel):
    nb = N_BUCKETS // 2
    max_exact = nb // 2
    ret = jnp.where(rel > 0, nb, 0)
    n = jnp.abs(rel)
    nf = jnp.maximum(n, 1).astype(F32)
    large = max_exact + (jnp.log(nf / max_exact) / math.log(MAX_DISTANCE / max_exact)
                         * (nb - max_exact)).astype(jnp.int32)
    large = jnp.minimum(large, nb - 1)
    return ret + jnp.where(n < max_exact, n, large)


def _bias_table(rel_bias):
    qi = jnp.arange(BLOCK, dtype=jnp.int32)[:, None]
    kj = jnp.arange(3 * BLOCK, dtype=jnp.int32)[None, :]
    bias = rel_bias[_t5_bucket(kj - BLOCK - qi)].astype(F32)
    return jnp.transpose(bias, (2, 0, 1))


def _memattn_kernel(q_ref, mem_ref, w_ref, o_ref, kv_ref):
    @pl.when(pl.program_id(1) == 0)
    def _():
        kv_ref[...] = jnp.dot(mem_ref[0].astype(BF16), w_ref[...],
                              preferred_element_type=F32).astype(kv_ref.dtype)

    q = q_ref[0]
    outs = []
    for h in range(N_MEM_HEADS):
        lo = h * MEM_HEAD_DIM
        qh = q[:, lo:lo + MEM_HEAD_DIM]
        kh = kv_ref[:, lo:lo + MEM_HEAD_DIM]
        vh = kv_ref[:, BRANCH_W + lo:BRANCH_W + lo + MEM_HEAD_DIM]
        s = lax.dot_general(qh, kh, (((1,), (1,)), ((), ())), preferred_element_type=F32) * (MEM_HEAD_DIM ** -0.5)
        m = jnp.max(s, axis=-1, keepdims=True)
        p = jnp.exp(s - m)
        denom = jnp.sum(p, axis=-1, keepdims=True)
        outs.append(jnp.dot(p.astype(BF16), vh, preferred_element_type=F32) / denom)
    o_ref[0] = jnp.concatenate(outs, axis=1).astype(o_ref.dtype)


def _memattn(h3, mem, w_kv, tm=512):
    B, L, _ = h3.shape
    M = mem.shape[1]
    tm = min(tm, L)
    return pl.pallas_call(
        _memattn_kernel,
        grid=(B, L // tm),
        in_specs=[
            pl.BlockSpec((1, tm, BRANCH_W), lambda b, l: (b, l, H_QM)),
            pl.BlockSpec((1, M, D_MODEL), lambda b, l: (b, 0, 0)),
            pl.BlockSpec((D_MODEL, 2 * BRANCH_W), lambda b, l: (0, 0)),
        ],
        out_specs=pl.BlockSpec((1, tm, BRANCH_W), lambda b, l: (b, l, 0)),
        out_shape=jax.ShapeDtypeStruct((B, L, BRANCH_W), BF16),
        scratch_shapes=[pltpu.VMEM((M, 2 * BRANCH_W), BF16)],
        compiler_params=_cparams(("parallel", "arbitrary")),
        name="memattn",
    )(h3, mem, w_kv)


HALO = 16


def _filt_kernel(w1_ref, b1_ref, w2_ref, b2_ref, w3_ref, b3_ref, fr_ref, fq_ref, dl_ref, k_ref, s_ref, *, seq, tr):
    i = pl.program_id(0)
    m = i * tr + lax.broadcasted_iota(jnp.int32, (tr, LANE), 0)
    lane = lax.broadcasted_iota(jnp.int32, (tr, LANE), 1)
    p = jnp.where(m < seq, m, 2 * seq - m).astype(F32)
    t = p / F32(seq - 1)
    w = F32(2.0 * math.pi) * p / F32(seq)
    bands = (POS_EMB_DIM - 1) // 2
    arg = fq_ref[...] * w
    arg = jnp.where(lane <= bands, arg + F32(0.5 * math.pi), -arg)
    z = jnp.where(lane == 0, t, jnp.where(lane <= 2 * bands, jnp.sin(arg), 0.0))
    fr = fr_ref[...]
    h = jnp.sin(fr[0:1] * (jnp.dot(z, w1_ref[...], precision=HI, preferred_element_type=F32) + b1_ref[...]))
    h = jnp.sin(fr[1:2] * (jnp.dot(h, w2_ref[...], precision=HI, preferred_element_type=F32) + b2_ref[...]))
    h = _dot3(h, w3_ref[...]) + b3_ref[...]
    mc = m[:, 0:1]
    tc = t[:, 0:1]
    decay = jnp.exp(-tc * dl_ref[...])
    oc = HYENA_ORDER * BRANCH_W

    @pl.when(i == 0)
    def _():
        s_ref[...] = jnp.zeros_like(s_ref)

    for o in range(HYENA_ORDER):
        fwd = h[:, o * BRANCH_W:(o + 1) * BRANCH_W]
        bwd = h[:, oc + o * BRANCH_W:oc + (o + 1) * BRANCH_W]
        k = jnp.where(mc < seq, fwd, bwd) * decay
        k = jnp.where(mc == seq, 0.0, k)
        k_ref[o] = k
        s_ref[o] += jnp.sum(jnp.abs(k), axis=0, keepdims=True)


def _filters(seq, w1, b1, w2, b2, w3, b3, freq, tr=256):
    n = 2 * seq
    tr = min(tr, n)
    bands = (POS_EMB_DIM - 1) // 2
    hid = LANE
    w1p = jnp.zeros((LANE, hid), F32).at[:POS_EMB_DIM, :FILTER_HIDDEN].set(w1.astype(F32))
    b1p = jnp.zeros((1, hid), F32).at[0, :FILTER_HIDDEN].set(b1.astype(F32))
    w2p = jnp.zeros((hid, hid), F32).at[:FILTER_HIDDEN, :FILTER_HIDDEN].set(w2.astype(F32))
    b2p = jnp.zeros((1, hid), F32).at[0, :FILTER_HIDDEN].set(b2.astype(F32))
    w3p = jnp.zeros((hid, w3.shape[1]), F32).at[:FILTER_HIDDEN].set(w3.astype(F32))
    b3p = b3.astype(F32)[None]
    frp = jnp.zeros((2, hid), F32).at[:, :FILTER_HIDDEN].set(freq.astype(F32))
    f = jnp.linspace(1e-4, bands - 1, bands, dtype=F32)
    fq = jnp.zeros((1, LANE), F32).at[0, 1:1 + bands].set(f).at[0, 1 + bands:1 + 2 * bands].set(f)
    deltas = jnp.abs(jnp.linspace(math.log(DECAY_TARGET) / FAST_DECAY_PCT,
                                  math.log(DECAY_TARGET) / SLOW_DECAY_PCT, BRANCH_W, dtype=F32))[None]
    full = lambda a: pl.BlockSpec(a.shape, lambda i: (0,) * a.ndim)
    args = (w1p, b1p, w2p, b2p, w3p, b3p, frp, fq, deltas)
    return pl.pallas_call(
        functools.partial(_filt_kernel, seq=seq, tr=tr),
        grid=(n // tr,),
        in_specs=[full(a) for a in args],
        out_specs=[pl.BlockSpec((HYENA_ORDER, tr, BRANCH_W), lambda i: (0, i, 0)),
                   pl.BlockSpec((HYENA_ORDER, 1, BRANCH_W), lambda i: (0, 0, 0))],
        out_shape=[jax.ShapeDtypeStruct((HYENA_ORDER, n, BRANCH_W), F32),
                   jax.ShapeDtypeStruct((HYENA_ORDER, 1, BRANCH_W), F32)],
        compiler_params=_cparams(("arbitrary",)),
        name="hyena_filter",
    )(*args)


def _dft_tables(seq):
    n1 = 2 * seq // FFT_N2
    n1h = seq // FFT_N2
    n = 2 * seq
    k1 = jnp.arange(n1, dtype=jnp.int32)
    th = (2.0 * math.pi / n1) * ((k1[:, None] * k1[None, :]) % n1).astype(F32)
    c1, s1 = jnp.cos(th), jnp.sin(th)
    ch, sh = c1[:, :n1h], s1[:, :n1h]
    f1 = jnp.concatenate([jnp.concatenate([ch, sh], 1), jnp.concatenate([-sh, ch], 1)], 0)
    f1_real = jnp.concatenate([c1, -s1], 0)
    cht, sht = ch.T / n, sh.T / n
    g1 = jnp.concatenate([jnp.concatenate([cht, -sht], 1), jnp.concatenate([sht, cht], 1)], 0)
    n2 = jnp.arange(FFT_N2, dtype=jnp.int32)
    kk = k1[:, None, None] + n1 * n2[None, :, None]
    ph = (2.0 * math.pi / n) * ((kk * n2[None, None, :]) % n).astype(F32)
    c2, s2 = jnp.cos(ph), jnp.sin(ph)
    m2 = jnp.concatenate([jnp.concatenate([c2, s2], 2), jnp.concatenate([-s2, c2], 2)], 1)
    return f1, f1_real, g1, m2, jnp.swapaxes(m2, 1, 2)


def _s1_kernel(z_ref, f_ref, o_ref):
    res = _dot3(f_ref[...], z_ref[0])
    half = res.shape[0] // 2
    o_ref[0, 0] = res[:half]
    o_ref[0, 1] = res[half:]


def _fft_s1(z, f, n2t=8):
    bx, lz, c = z.shape
    r = lz // FFT_N2
    n1 = f.shape[0] // 2
    w = n2t * c
    zz = z.reshape(bx, r, FFT_N2 * c)
    return pl.pallas_call(
        _s1_kernel,
        grid=(bx, FFT_N2 // n2t),
        in_specs=[pl.BlockSpec((1, r, w), lambda p, j: (p, 0, j)),
                  pl.BlockSpec(f.shape, lambda p, j: (0, 0))],
        out_specs=pl.BlockSpec((1, 2, n1, w), lambda p, j: (p, 0, 0, j)),
        out_shape=jax.ShapeDtypeStruct((bx, 2, n1, FFT_N2 * c), F32),
        compiler_params=_cparams(("parallel", "parallel")),
        name="fft_stage1",
    )(zz, f)


def _mid_fwd_kernel(a_ref, m_ref, o_ref):
    k1t = a_ref.shape[2]
    h = FFT_N2
    for j in range(k1t):
        a = jnp.concatenate([a_ref[0, 0, j], a_ref[0, 1, j]], axis=0)
        x = _dot3(m_ref[j], a)
        o_ref[0, 0, j] = x[:h]
        o_ref[0, 1, j] = x[h:]


def _fft_mid_fwd(a, m2, k1t=8):
    p, _, n1, wc = a.shape
    c = wc // FFT_N2
    k1t = min(k1t, n1)
    a5 = a.reshape(p, 2, n1, FFT_N2, c)
    blk = (1, 2, k1t, FFT_N2, c)
    return pl.pallas_call(
        _mid_fwd_kernel,
        grid=(n1 // k1t, p),
        in_specs=[pl.BlockSpec(blk, lambda k, q: (q, 0, k, 0, 0)),
                  pl.BlockSpec((k1t, 2 * FFT_N2, 2 * FFT_N2), lambda k, q: (k, 0, 0))],
        out_specs=pl.BlockSpec(blk, lambda k, q: (q, 0, k, 0, 0)),
        out_shape=jax.ShapeDtypeStruct(a5.shape, F32),
        compiler_params=_cparams(("parallel", "parallel")),
        name="fft_mid_filter",
    )(a5, m2)


ROW_PAD = 8


def _pack2(hi, lo):
    hb = lax.bitcast_convert_type(hi.astype(BF16).astype(F32), jnp.uint32)
    lb = lax.bitcast_convert_type(lo.astype(BF16).astype(F32), jnp.uint32)
    return hb | (lb >> 16)


def _unpack2(w):
    return (lax.bitcast_convert_type(w & jnp.uint32(0xFFFF0000), F32),
            lax.bitcast_convert_type(w << 16, F32))


def _rows(start, size, stride):
    return pl.ds(start, size) if stride is None else pl.ds(start, size, stride=stride)


def _ld(ref, start, size, stride=None):
    parts = [ref[l, _rows(start, size, stride), :] for l in range(ref.shape[0])]
    return parts[0] if len(parts) == 1 else jnp.concatenate(parts, axis=1)


def _st(ref, start, size, val, stride=None):
    for l in range(ref.shape[0]):
        ref[l, _rows(start, size, stride), :] = val[:, l * LANE:(l + 1) * LANE]


def _time_tile(ref, part, i, n_tiles, conv):
    h = FFT_N2
    r0 = pl.multiple_of(i * h, h)
    cur = ref[part, pl.ds(r0, h), :].astype(F32)
    if conv is None:
        return cur
    w_ref, b_ref = conv
    lo = pl.multiple_of(jnp.maximum(r0 - HALO, 0), HALO)
    hi = pl.multiple_of(jnp.minimum(r0 + h, n_tiles * h - HALO), HALO)
    prev = ref[part, pl.ds(lo, HALO), :].astype(F32)[HALO - 1:HALO]
    nxt = ref[part, pl.ds(hi, HALO), :].astype(F32)[0:1]
    prev = jnp.where(i > 0, prev, 0.0)
    nxt = jnp.where(i < n_tiles - 1, nxt, 0.0)
    row = lax.broadcasted_iota(jnp.int32, cur.shape, 0)
    up = jnp.where(row == 0, prev, pltpu.roll(cur, 1, 0))
    un = jnp.where(row == h - 1, nxt, pltpu.roll(cur, h - 1, 0))
    w = w_ref[...]
    return b_ref[...] + up * w[0:1] + cur * w[1:2] + un * w[2:3]


def _lc_kernel(v_ref, x_ref, f1_ref, g1_ref, m2_ref, m2t_ref, kf_ref, sc_ref, d_ref, cwv_ref, cbv_ref,
               cwx_ref, cbx_ref, o_ref, zst, a3, *, n1, n1h, k1t, nk, grp, conv_v, conv_x):
    k = pl.program_id(1)
    cb = v_ref.shape[2]
    zp = FFT_N2 + ROW_PAD
    ap = n1 + ROW_PAD
    h = FFT_N2
    vconv = (cwv_ref, cbv_ref) if conv_v else None
    xconv = (cwx_ref, cbx_ref) if conv_x else None

    @pl.when(k == 0)
    def _():
        def stage(i, c):
            w = _pack2(_time_tile(v_ref, 0, i, n1h, vconv), _time_tile(v_ref, 1, i, n1h, vconv))
            _st(zst, pl.multiple_of(i * zp, 8), h, w)
            return c
        lax.fori_loop(0, n1h, stage, 0)

        def level1(t, c):
            zs = []
            for u in range(grp):
                zr, zi = _unpack2(_ld(zst, t * grp + u, n1h, stride=zp))
                zs.append(jnp.concatenate([zr, zi], axis=0).astype(BF16))
            res = jnp.dot(f1_ref[...], jnp.concatenate(zs, axis=1), preferred_element_type=F32)
            for u in range(grp):
                r0 = pl.multiple_of((t * grp + u) * ap, 8)
                _st(a3, r0, n1, _pack2(res[:n1, u * cb:(u + 1) * cb], res[n1:, u * cb:(u + 1) * cb]))
            return c
        lax.fori_loop(0, h // grp, level1, 0, unroll=2)

    for jj in range(k1t):
        k1 = k * k1t + jj
        ar, ai = _unpack2(_ld(a3, k1, h, stride=ap))
        a = jnp.concatenate([ar, ai], axis=0).astype(BF16)
        x = jnp.dot(m2_ref[jj], a, preferred_element_type=F32)
        xr, xi = x[:h], x[h:]
        kr, ki = kf_ref[0, 0, jj], kf_ref[0, 1, jj]
        y = jnp.concatenate([xr * kr - xi * ki, xr * ki + xi * kr], axis=0).astype(BF16)
        b = jnp.dot(m2t_ref[jj], y, preferred_element_type=F32)
        _st(a3, k1, h, _pack2(b[:h], b[h:]), stride=ap)

    @pl.when(k == nk - 1)
    def _():
        def level3(t, c):
            bms = []
            for u in range(grp):
                r0 = pl.multiple_of((t * grp + u) * ap, 8)
                br, bi = _unpack2(_ld(a3, r0, n1))
                bms.append(jnp.concatenate([br, bi], axis=0).astype(BF16))
            y = jnp.dot(g1_ref[...], jnp.concatenate(bms, axis=1), preferred_element_type=F32)
            for u in range(grp):
                r0 = pl.multiple_of((t * grp + u) * ap, 8)
                _st(a3, r0, n1, lax.bitcast_convert_type(y[:, u * cb:(u + 1) * cb], jnp.uint32))
            return c
        lax.fori_loop(0, h // grp, level3, 0, unroll=2)

        def gate(i, c):
            r0 = pl.multiple_of(i * h, h)
            vs = _unpack2(_ld(zst, pl.multiple_of(i * zp, 8), h))
            for part in range(2):
                yv = lax.bitcast_convert_type(_ld(a3, part * n1h + i, h, stride=ap), F32)
                xx = _time_tile(x_ref, part, i, n1h, xconv)
                o_ref[part, pl.ds(r0, h), :] = (xx * (yv * sc_ref[...] + vs[part] * d_ref[...])).astype(o_ref.dtype)
            return c
        lax.fori_loop(0, n1h, gate, 0)


LC_VMEM_ROWS = 4096


def _longconv(v, v_seg, x, x_seg, conv_w, conv_b, kf, order, inv_norm, skip, tables, k1t=8):
    f1, g1, m2, m2t = tables
    bx, seq, _ = v.shape
    c = BRANCH_W
    cb = 2 * LANE if seq <= LC_VMEM_ROWS else LANE
    n1 = 2 * seq // FFT_N2
    n1h = seq // FFT_N2
    k1t = min(k1t, n1)
    nk = n1 // k1t
    ncb = c // cb
    grp = 4 if n1 >= FFT_N2 else 8
    hy0 = OFF_HY // cb

    def io(seg):
        col = 0 if seg is None else hy0 + seg * ncb
        return pl.BlockSpec((2, seq, cb), lambda g, k: (g // ncb, 0, col + g % ncb))

    def cvec(rows, seg):
        col = 0 if seg is None else seg * ncb
        return pl.BlockSpec((rows, cb), lambda g, k: (0, col + g % ncb))

    vec = pl.BlockSpec((1, cb), lambda g, k: (0, g % ncb))
    mat = pl.BlockSpec((k1t, 2 * FFT_N2, 2 * FFT_N2), lambda g, k: (k, 0, 0))
    return pl.pallas_call(
        functools.partial(_lc_kernel, n1=n1, n1h=n1h, k1t=k1t, nk=nk, grp=grp,
                          conv_v=v_seg is not None, conv_x=x_seg is not None),
        grid=((bx // 2) * ncb, nk),
        in_specs=[io(v_seg), io(x_seg),
                  pl.BlockSpec(f1.shape, lambda g, k: (0, 0)),
                  pl.BlockSpec(g1.shape, lambda g, k: (0, 0)),
                  mat, mat,
                  pl.BlockSpec((1, 2, k1t, FFT_N2, cb), lambda g, k: (order, 0, k, 0, g % ncb)),
                  vec, vec,
                  cvec(SHORT_CONV, v_seg), cvec(1, v_seg), cvec(SHORT_CONV, x_seg), cvec(1, x_seg)],
        out_specs=io(None),
        out_shape=jax.ShapeDtypeStruct((bx, seq, c), BF16),
        scratch_shapes=[pltpu.VMEM((cb // LANE, n1h * (FFT_N2 + ROW_PAD), LANE), jnp.uint32),
                        pltpu.VMEM((cb // LANE, FFT_N2 * (n1 + ROW_PAD), LANE), jnp.uint32)],
        compiler_params=_cparams(("arbitrary", "arbitrary")),
        name="hyena_longconv",
    )(v, x, f1, g1, m2, m2t, kf, inv_norm.astype(F32).reshape(1, c), skip.astype(F32).reshape(1, c),
      conv_w, conv_b, conv_w, conv_b)


def _hyena(h3, conv_w, conv_b, filt, skip):
    seq = h3.shape[1]
    f1, f1_real, g1, m2, m2t = _dft_tables(seq)
    taps, asum = _filters(seq, *filt)
    kf = _fft_mid_fwd(_fft_s1(taps, f1_real), m2)
    inv_norm = 1.0 / asum[:, 0]
    tables = (f1.astype(BF16), g1.astype(BF16), m2.astype(BF16), m2t.astype(BF16))
    z1 = _longconv(h3, 0, h3, 1, conv_w, conv_b, kf, 0, inv_norm[0], skip[0], tables)
    return _longconv(z1, None, h3, 2, conv_w, conv_b, kf, 1, inv_norm[1], skip[1], tables)


def _mix_kernel(x_ref, a_ref, hb_ref, c_ref, g0_ref, g1_ref, g2_ref, wb_ref, wo_ref,
                lig_ref, lib_ref, l1g_ref, l1b_ref, wrh_ref, wrl_ref, br_ref, x1_ref, x1b_ref, rt_ref, cnt_ref):
    acc = None
    for n, (br, gr) in enumerate(((a_ref, g0_ref), (hb_ref, g1_ref), (c_ref, g2_ref))):
        proj = jnp.dot(br[...], wb_ref[n], preferred_element_type=F32)
        term = jax.nn.sigmoid(gr[...].astype(F32)) * proj
        acc = term if acc is None else acc + term
    mix = jnp.dot(acc.astype(BF16), wo_ref[...], preferred_element_type=F32)
    xn = _ln(x_ref[...], lig_ref[...], lib_ref[...])
    x1 = _ln(DN_ALPHA * xn + mix, l1g_ref[...], l1b_ref[...])
    x1_ref[...] = x1
    xh = x1.astype(BF16)
    x1b_ref[...] = xh
    xl = (x1 - xh.astype(F32)).astype(BF16)
    nt = (((1,), (1,)), ((), ()))
    work = lax.dot_general(wrh_ref[...], xh, nt, preferred_element_type=F32)
    work = work + lax.dot_general(wrh_ref[...], xl, nt, preferred_element_type=F32)
    work = work + lax.dot_general(wrl_ref[...], xh, nt, preferred_element_type=F32)
    work = work + br_ref[...]

    tm = work.shape[1]
    erow = lax.broadcasted_iota(jnp.int32, work.shape, 0)
    ids, vals, hots = [], [], []
    for _ in range(TOP_K):
        mx = jnp.max(work, axis=0, keepdims=True)
        idx = jnp.min(jnp.where(work == mx, erow, N_EXPERTS), axis=0, keepdims=True)
        hot = erow == idx
        ids.append(idx)
        vals.append(mx)
        hots.append(hot)
        work = jnp.where(hot, F32(-3e38), work)
    exps = [jnp.exp(v - vals[0]) for v in vals]
    den = exps[0]
    for e in exps[1:]:
        den = den + e
    chosen = hots[0]
    for hot in hots[1:]:
        chosen = chosen | hot
    chosen = chosen.astype(F32)

    @pl.when(pl.program_id(0) == 0)
    def _():
        cnt_ref[...] = jnp.zeros_like(cnt_ref)

    r_i = lax.broadcasted_iota(jnp.int32, (tm, tm), 0)
    c_i = lax.broadcasted_iota(jnp.int32, (tm, tm), 1)
    earlier = (r_i < c_i).astype(BF16)
    before = jnp.dot(chosen.astype(BF16), earlier, preferred_element_type=F32)
    before = before + jnp.tile(cnt_ref[...], (1, tm // LANE))
    cnt_ref[...] += jnp.sum(chosen, axis=1, keepdims=True)
    rows = [i.astype(F32) for i in ids] + [e / den for e in exps]
    rows += [jnp.sum(jnp.where(hot, before, 0.0), axis=0, keepdims=True) for hot in hots]
    rows.append(jnp.zeros((rt_ref.shape[0] - len(rows), tm), F32))
    rt_ref[...] = jnp.concatenate(rows, axis=0)


ROUTE_ROWS = 16


def _mix(x2, a, hb, c, h2, wb, wo, lig, lib, l1g, l1b, wr, br, tm=512):
    T, D = x2.shape
    wrt = wr.T
    wrh = wrt.astype(BF16)
    wrl = (wrt - wrh.astype(F32)).astype(BF16)
    br = br.reshape(N_EXPERTS, 1)
    row = lambda wdt, col: pl.BlockSpec((tm, wdt), lambda i: (i, col))
    const = lambda arr: pl.BlockSpec(arr.shape, lambda i: (0,) * arr.ndim)
    return pl.pallas_call(
        _mix_kernel,
        grid=(T // tm,),
        in_specs=[row(D, 0), row(BRANCH_W, 0), row(BRANCH_W, 0), row(BRANCH_W, 0),
                  row(D, H_G), row(D, H_G + 1), row(D, H_G + 2),
                  const(wb), const(wo), const(lig), const(lib), const(l1g), const(l1b), const(wrh), const(wrl), const(br)],
        out_specs=[row(D, 0), row(D, 0), pl.BlockSpec((ROUTE_ROWS, tm), lambda i: (0, i)),
                   pl.BlockSpec((N_EXPERTS, LANE), lambda i: (0, 0))],
        out_shape=[jax.ShapeDtypeStruct((T, D), F32), jax.ShapeDtypeStruct((T, D), BF16),
                   jax.ShapeDtypeStruct((ROUTE_ROWS, T), F32), jax.ShapeDtypeStruct((N_EXPERTS, LANE), F32)],
        compiler_params=_cparams(("arbitrary",)),
        name="mix",
    )(x2, a, hb, c, h2, h2, h2, wb, wo, lig, lib, l1g, l1b, wrh, wrl, br)


def _ffn_kernel(ce_ref, nu_ref, x_ref, wgu_ref, bgu_ref, wd_ref, bd_ref, o_ref):
    del ce_ref
    used = pl.program_id(0) < nu_ref[0]

    @pl.when(used)
    def _():
        gu = jnp.dot(x_ref[...], wgu_ref[0], preferred_element_type=F32) + bgu_ref[0]
        f = gu.shape[1] // 2
        g = jnp.minimum(gu[:, :f], SWIGLU_LIMIT)
        u = jnp.clip(gu[:, f:], -SWIGLU_LIMIT, SWIGLU_LIMIT)
        hmid = (u + 1.0) * (g * jax.nn.sigmoid(g * SWIGLU_ALPHA))
        y = jnp.dot(hmid.astype(BF16), wd_ref[0].astype(BF16), preferred_element_type=F32) + bd_ref[0]
        o_ref[...] = y.astype(o_ref.dtype)

    @pl.when(jnp.logical_not(used))
    def _():
        o_ref[...] = jnp.zeros_like(o_ref)


def _ffn(chunk_e, n_used, x_slots, wgu, bgu, wd, bd):
    P, D = x_slots.shape
    F = wd.shape[1]
    n_chunks = P // MOE_TILE
    wspec = lambda a, b: pl.BlockSpec((1, a, b), lambda i, ce, nu: (ce[i], 0, 0))
    grid_spec = pltpu.PrefetchScalarGridSpec(
        num_scalar_prefetch=2,
        grid=(n_chunks,),
        in_specs=[pl.BlockSpec((MOE_TILE, D), lambda i, ce, nu: (i, 0)),
                  wspec(D, 2 * F), wspec(1, 2 * F), wspec(F, D), wspec(1, D)],
        out_specs=pl.BlockSpec((MOE_TILE, D), lambda i, ce, nu: (i, 0)),
    )
    return pl.pallas_call(
        _ffn_kernel,
        grid_spec=grid_spec,
        out_shape=jax.ShapeDtypeStruct((P, D), BF16),
        compiler_params=_cparams(("arbitrary",)),
        name="moe_ffn",
    )(chunk_e, n_used, x_slots, wgu, bgu, wd, bd)


DEINT_W = 256


def _deint_kernel(w_ref, s_ref, o_ref):
    w = w_ref[0].astype(BF16)
    f = w.shape[1] // 2
    half = DEINT_W // 2
    for blk in range(w.shape[1] // DEINT_W):
        r = jnp.dot(w[:, blk * DEINT_W:(blk + 1) * DEINT_W], s_ref[...], preferred_element_type=F32)
        o_ref[0, :, blk * half:(blk + 1) * half] = r[:, :half].astype(o_ref.dtype)
        o_ref[0, :, f + blk * half:f + (blk + 1) * half] = r[:, half:].astype(o_ref.dtype)


def _deinterleave(w, tr=1024):
    E, D, F2 = w.shape
    i = jnp.arange(DEINT_W, dtype=jnp.int32)
    src = jnp.where(i < DEINT_W // 2, 2 * i, 2 * (i - DEINT_W // 2) + 1)
    sel = (i[:, None] == src[None, :]).astype(BF16)
    return pl.pallas_call(
        _deint_kernel,
        grid=(E, D // tr),
        in_specs=[pl.BlockSpec((1, tr, F2), lambda e, r: (e, r, 0)),
                  pl.BlockSpec((DEINT_W, DEINT_W), lambda e, r: (0, 0))],
        out_specs=pl.BlockSpec((1, tr, F2), lambda e, r: (e, r, 0)),
        out_shape=jax.ShapeDtypeStruct((E, D, F2), BF16),
        compiler_params=_cparams(("parallel", "parallel")),
        name="deinterleave_gate_up",
    )(w, sel)


def _final_kernel(x1_ref, y_ref, gt_ref, g_ref, b_ref, o_ref):
    gt = gt_ref[...]
    moe = y_ref[0].astype(F32) * gt[:, 0:1]
    for k in range(1, TOP_K):
        moe = moe + y_ref[k].astype(F32) * gt[:, k:k + 1]
    o_ref[...] = _ln(DN_ALPHA * x1_ref[...] + moe, g_ref[...], b_ref[...])


def _final(x1, yg, gates, g, b, tm=512):
    T, D = x1.shape
    return pl.pallas_call(
        _final_kernel,
        grid=(T // tm,),
        in_specs=[pl.BlockSpec((tm, D), lambda i: (i, 0)),
                  pl.BlockSpec((TOP_K, tm, D), lambda i: (0, i, 0)),
                  pl.BlockSpec((tm, TOP_K), lambda i: (i, 0)),
                  pl.BlockSpec((1, D), lambda i: (0, 0)),
                  pl.BlockSpec((1, D), lambda i: (0, 0))],
        out_specs=pl.BlockSpec((tm, D), lambda i: (i, 0)),
        out_shape=jax.ShapeDtypeStruct((T, D), F32),
        compiler_params=_cparams(("parallel",)),
        name="final_ln",
    )(x1, yg, gates, g, b)


def _moe(x1, x1b, route, cnt, ffn_w, ln2_g, ln2_b):
    T, D = x1.shape
    A = T * TOP_K
    i32 = jnp.int32
    ids = route[:TOP_K].astype(i32)
    gates = route[TOP_K:2 * TOP_K]
    rank = route[2 * TOP_K:3 * TOP_K].astype(i32)
    counts = cnt[:, 0].astype(i32)
    pcounts = (counts + MOE_TILE - 1) // MOE_TILE * MOE_TILE
    pends = jnp.cumsum(pcounts)
    pstarts = pends - pcounts
    n_chunks = -(-A // MOE_TILE) + N_EXPERTS
    P = n_chunks * MOE_TILE
    e3 = jnp.arange(N_EXPERTS, dtype=i32)[:, None, None]
    pos = rank + jnp.sum(jnp.where(ids[None] == e3, pstarts[:, None, None], 0), axis=0)
    padc = jnp.concatenate([pcounts - counts, (P - pends[-1])[None]])
    pad_lo = jnp.concatenate([pstarts + counts, pends[-1:]])
    cum = jnp.cumsum(padc)
    j = jnp.arange(P - A, dtype=i32)
    run = jnp.sum((j[None, :] >= cum[:, None]).astype(i32), axis=0)
    hot = run[None, :] == jnp.arange(N_EXPERTS + 1, dtype=i32)[:, None]
    pad_key = j + jnp.sum(jnp.where(hot, (pad_lo - (cum - padc))[:, None], 0), axis=0)
    keys = jnp.concatenate([pos.reshape(A), pad_key])
    toks = jnp.concatenate([jnp.tile(jnp.arange(T, dtype=i32), TOP_K), jnp.zeros((P - A,), i32)])
    _, slot_tok = lax.sort((keys, toks), num_keys=1)
    chunk_lo = jnp.arange(n_chunks, dtype=i32) * MOE_TILE
    chunk_e = jnp.minimum(jnp.sum((chunk_lo[None, :] >= pends[:, None]).astype(i32), axis=0), N_EXPERTS - 1)
    x_slots = x1b[slot_tok]
    y_slots = _ffn(chunk_e, (pends[-1:] // MOE_TILE).astype(i32), x_slots, *ffn_w)
    yg = y_slots[pos.reshape(A)].reshape(TOP_K, T, D)
    return _final(x1, yg, gates.T, ln2_g, ln2_b)


def _mixer_half(x, mem, p):
    B, L, D = x.shape
    x2 = x.reshape(B * L, D)
    h2 = _inproj(x2, p["ln_in_g"], p["ln_in_b"], p["w_in"])
    h3 = h2.reshape(B, L, IN_W)
    a = _wattn(h3, p["sink"], p["bias"])
    hb = _hyena(h3, p["conv_w"], p["conv_b"], p["filt"], p["skip"])
    c = _memattn(h3, mem, p["w_mem_kv"])
    return _mix(x2, a.reshape(B * L, -1), hb.reshape(B * L, -1), c.reshape(B * L, -1), h2,
                p["w_branch"], p["w_out"], p["ln_in_g"], p["ln_in_b"], p["ln1_g"], p["ln1_b"],
                p["w_router"], p["b_router"])


def _trunk(x, mem, p):
    return _moe(*_mixer_half(x, mem, p), p["ffn"], p["ln2_g"], p["ln2_b"]).reshape(x.shape)


def _prep(ln_in_g, ln_in_b, rel_bias, w_in, attn_sink, conv_w, conv_b, filt_w1, filt_b1, filt_w2, filt_b2,
          filt_w3, filt_b3, filt_freq, hyena_skip, w_mem_kv, w_branch, w_out, ln1_g, ln1_b, w_router, b_router,
          w_gate_up, b_gate_up, w_down, b_down, ln2_g, ln2_b):
    w = w_in
    g_lo = Q_W + 2 * KV_W + 3 * BRANCH_W + BRANCH_W
    hy_lo = Q_W + 2 * KV_W
    assert HEAD_DIM ** -0.5 == 2.0 ** -3
    w_p = jnp.concatenate([w[:, g_lo:], w[:, :Q_W] * (HEAD_DIM ** -0.5), w[:, hy_lo + 3 * BRANCH_W:g_lo],
                           w[:, hy_lo:hy_lo + 3 * BRANCH_W], w[:, Q_W:hy_lo]], axis=1).astype(BF16)
    row = lambda v: v.astype(F32)[None]
    wr = w_router.astype(F32)
    br = b_router.astype(F32)
    return dict(
        ln_in_g=row(ln_in_g), ln_in_b=row(ln_in_b), w_in=w_p,
        sink=attn_sink.astype(F32), bias=_bias_table(rel_bias),
        conv_w=conv_w.astype(F32), conv_b=row(conv_b),
        filt=(filt_w1, filt_b1, filt_w2, filt_b2, filt_w3, filt_b3, filt_freq), skip=hyena_skip,
        w_mem_kv=w_mem_kv.astype(BF16), w_branch=w_branch.astype(BF16), w_out=w_out.astype(BF16),
        ln1_g=row(ln1_g), ln1_b=row(ln1_b), w_router=wr, b_router=br,
        ffn=(_deinterleave(w_gate_up),
             jnp.concatenate([b_gate_up[:, 0::2], b_gate_up[:, 1::2]], axis=1)[:, None, :].astype(F32),
             w_down, b_down[:, None, :].astype(F32)),
        ln2_g=row(ln2_g), ln2_b=row(ln2_b),
    )


def kernel(x_prompt, x_sample, mem_prompt, mem_sample, ln_in_g, ln_in_b, rel_bias, w_in, attn_sink, conv_w, conv_b, filt_w1, filt_b1, filt_w2, filt_b2, filt_w3, filt_b3, filt_freq, hyena_skip, w_mem_kv, w_branch, w_out, ln1_g, ln1_b, w_router, b_router, w_gate_up, b_gate_up, w_down, b_down, ln2_g, ln2_b):
    p = _prep(ln_in_g, ln_in_b, rel_bias, w_in[0], attn_sink[0], conv_w[0], conv_b[0], filt_w1[0], filt_b1[0],
              filt_w2[0], filt_b2[0], filt_w3[0], filt_b3[0], filt_freq[0], hyena_skip[0], w_mem_kv[0],
              w_branch[0], w_out[0], ln1_g[0], ln1_b[0], w_router[0], b_router[0], w_gate_up[0], b_gate_up[0],
              w_down[0], b_down[0], ln2_g[0], ln2_b[0])
    halves = [_mixer_half(x, mem, p) for x, mem in ((x_prompt, mem_prompt), (x_sample, mem_sample))]
    outs = [_moe(*half, p["ffn"], p["ln2_g"], p["ln2_b"]) for half in halves]
    return (outs[0].reshape(x_prompt.shape), outs[1].reshape(x_sample.shape))
```

```python
import functools
import math

import jax
import jax.numpy as jnp
from jax import lax
from jax.experimental import pallas as pl
from jax.experimental.pallas import tpu as pltpu

F32 = jnp.float32
BF16 = jnp.bfloat16
HI = lax.Precision.HIGHEST

D_MODEL = 1024
BRANCH_W = 512
N_Q_HEADS = 8
N_KV_HEADS = 2
HEAD_DIM = 64
WINDOW = 128
BLOCK = 128
N_BUCKETS = 32
MAX_DISTANCE = 128
HYENA_ORDER = 2
SHORT_CONV = 3
POS_EMB_DIM = 33
FILTER_HIDDEN = 64
DECAY_TARGET = 1e-2
FAST_DECAY_PCT = 0.3
SLOW_DECAY_PCT = 1.5
N_MEM_HEADS = 4
MEM_HEAD_DIM = BRANCH_W // N_MEM_HEADS
N_BRANCHES = 3
N_EXPERTS = 32
TOP_K = 4
D_EXPERT = 1024
SWIGLU_LIMIT = 7.0
SWIGLU_ALPHA = 1.702
LN_EPS = 1e-5
DEPTH = 1
DN_ALPHA = (2 * DEPTH) ** 0.25

Q_W = N_Q_HEADS * HEAD_DIM
KV_W = N_KV_HEADS * HEAD_DIM
IN_W = Q_W + 2 * KV_W + 3 * BRANCH_W + BRANCH_W + N_BRANCHES * D_MODEL

LANE = 128
V7X_VMEM_BYTES = 64 * 1024 * 1024
VMEM_LIMIT = V7X_VMEM_BYTES - 12 * 1024 * 1024

OFF_G = 0
OFF_Q = OFF_G + N_BRANCHES * D_MODEL
OFF_QM = OFF_Q + Q_W
OFF_HY = OFF_QM + BRANCH_W
OFF_K = OFF_HY + 3 * BRANCH_W
OFF_V = OFF_K + KV_W
H_G = OFF_G // D_MODEL
H_Q = OFF_Q // Q_W
H_QM = OFF_QM // BRANCH_W
H_K = OFF_K // KV_W
H_V = OFF_V // KV_W

FFT_N2 = 128
MOE_TILE = 512


def _cparams(sem):
    return pltpu.CompilerParams(dimension_semantics=sem, vmem_limit_bytes=VMEM_LIMIT)


def _split(x):
    hi = x.astype(BF16)
    return hi, (x - hi.astype(F32)).astype(BF16)


def _dot3(a, b):
    ah, al = _split(a)
    bh, bl = _split(b)
    out = jnp.dot(ah, bh, preferred_element_type=F32)
    out = out + jnp.dot(al, bh, preferred_element_type=F32)
    return out + jnp.dot(ah, bl, preferred_element_type=F32)


def _ln(x, g, b):
    mu = jnp.mean(x, axis=-1, keepdims=True)
    xc = x - mu
    var = jnp.mean(xc * xc, axis=-1, keepdims=True)
    return xc * lax.rsqrt(var + LN_EPS) * g + b


def _inproj_kernel(x_ref, g_ref, b_ref, *refs):
    w_refs, o_ref = refs[:-1], refs[-1]
    xn = _ln(x_ref[...], g_ref[...], b_ref[...]).astype(BF16)
    col = 0
    for w_ref in w_refs:
        width = w_ref.shape[1]
        o_ref[:, col:col + width] = jnp.dot(xn, w_ref[...], preferred_element_type=F32).astype(o_ref.dtype)
        col += width


def _inproj(x2, ln_g, ln_b, w_parts, tm=512):
    T, D = x2.shape
    N = sum(w.shape[1] for w in w_parts)
    return pl.pallas_call(
        _inproj_kernel,
        grid=(T // tm,),
        in_specs=[pl.BlockSpec((tm, D), lambda i: (i, 0)),
                  pl.BlockSpec((1, D), lambda i: (0, 0)),
                  pl.BlockSpec((1, D), lambda i: (0, 0))]
                 + [pl.BlockSpec(w.shape, lambda i: (0, 0)) for w in w_parts],
        out_specs=pl.BlockSpec((tm, N), lambda i: (i, 0)),
        out_shape=jax.ShapeDtypeStruct((T, N), BF16),
        compiler_params=_cparams(("parallel",)),
        name="inproj",
    )(x2, ln_g, ln_b, *w_parts)


def _wattn_kernel(sink_ref, q_ref, kp_ref, kc_ref, kn_ref, vp_ref, vc_ref, vn_ref, bias_ref, o_ref, *, seq):
    n = pl.program_id(1)
    q = q_ref[0]
    kcat = jnp.concatenate([kp_ref[0], kc_ref[0], kn_ref[0]], axis=0)
    vcat = jnp.concatenate([vp_ref[0], vc_ref[0], vn_ref[0]], axis=0)
    qi = lax.broadcasted_iota(jnp.int32, (BLOCK, 3 * BLOCK), 0)
    kj = lax.broadcasted_iota(jnp.int32, (BLOCK, 3 * BLOCK), 1)
    rel = kj - BLOCK - qi
    kpos = (n - 1) * BLOCK + kj
    valid = (jnp.abs(rel) <= WINDOW) & (kpos >= 0) & (kpos < seq)
    group = N_Q_HEADS // N_KV_HEADS
    scores = []
    for h in range(N_Q_HEADS):
        g = h // group
        qh = q[:, h * HEAD_DIM:(h + 1) * HEAD_DIM]
        kh = kcat[:, g * HEAD_DIM:(g + 1) * HEAD_DIM]
        scores.append(lax.dot_general(qh, kh, (((1,), (1,)), ((), ())), preferred_element_type=F32))
    probs, denoms = [], []
    for h in range(N_Q_HEADS):
        s = jnp.where(valid, scores[h] + bias_ref[h], F32(-1e30))
        sk = sink_ref[h]
        m = jnp.maximum(jnp.max(s, axis=-1, keepdims=True), sk)
        p = jnp.exp(s - m)
        denoms.append(jnp.sum(p, axis=-1, keepdims=True) + jnp.exp(sk - m))
        probs.append(p.astype(BF16))
    outs = []
    for h in range(N_Q_HEADS):
        g = h // group
        vh = vcat[:, g * HEAD_DIM:(g + 1) * HEAD_DIM]
        outs.append(jnp.dot(probs[h], vh, preferred_element_type=F32) / denoms[h])
    o_ref[0] = jnp.concatenate(outs, axis=1).astype(o_ref.dtype)


def _wattn(h3, sink, bias):
    B, L, _ = h3.shape
    nb = L // BLOCK
    kv_spec = lambda col, shift: pl.BlockSpec(
        (1, BLOCK, KV_W), lambda b, n: (b, jnp.clip(n + shift, 0, nb - 1), col))
    return pl.pallas_call(
        functools.partial(_wattn_kernel, seq=L),
        grid=(B, nb),
        in_specs=[
            pl.BlockSpec(memory_space=pltpu.SMEM),
            pl.BlockSpec((1, BLOCK, Q_W), lambda b, n: (b, n, H_Q)),
            kv_spec(H_K, -1), kv_spec(H_K, 0), kv_spec(H_K, 1),
            kv_spec(H_V, -1), kv_spec(H_V, 0), kv_spec(H_V, 1),
            pl.BlockSpec((N_Q_HEADS, BLOCK, 3 * BLOCK), lambda b, n: (0, 0, 0)),
        ],
        out_specs=pl.BlockSpec((1, BLOCK, Q_W), lambda b, n: (b, n, 0)),
        out_shape=jax.ShapeDtypeStruct((B, L, Q_W), BF16),
        compiler_params=_cparams(("parallel", "parallel")),
        name="wattn",
    )(sink, h3, h3, h3, h3, h3, h3, h3, bias)


def _t5_bucket(rel):
    nb = N_BUCKETS // 2
    max_exact = nb // 2
    ret = jnp.where(rel > 0, nb, 0)
    n = jnp.abs(rel)
    nf = jnp.maximum(n, 1).astype(F32)
    large = max_exact + (jnp.log(nf / max_exact) / math.log(MAX_DISTANCE / max_exact)
                         * (nb - max_exact)).astype(jnp.int32)
    large = jnp.minimum(large, nb - 1)
    return ret + jnp.where(n < max_exact, n, large)


def _bias_table(rel_bias):
    qi = jnp.arange(BLOCK, dtype=jnp.int32)[:, None]
    kj = jnp.arange(3 * BLOCK, dtype=jnp.int32)[None, :]
    bias = rel_bias[_t5_bucket(kj - BLOCK - qi)].astype(F32)
    return jnp.transpose(bias, (2, 0, 1))


def _memattn_kernel(q_ref, mem_ref, w_ref, o_ref, kv_ref):
    @pl.when(pl.program_id(1) == 0)
    def _():
        kv_ref[...] = jnp.dot(mem_ref[0].astype(BF16), w_ref[...],
                              preferred_element_type=F32).astype(kv_ref.dtype)

    q = q_ref[0]
    outs = []
    for h in range(N_MEM_HEADS):
        lo = h * MEM_HEAD_DIM
        qh = q[:, lo:lo + MEM_HEAD_DIM]
        kh = kv_ref[:, lo:lo + MEM_HEAD_DIM]
        vh = kv_ref[:, BRANCH_W + lo:BRANCH_W + lo + MEM_HEAD_DIM]
        s = lax.dot_general(qh, kh, (((1,), (1,)), ((), ())), preferred_element_type=F32) * (MEM_HEAD_DIM ** -0.5)
        m = jnp.max(s, axis=-1, keepdims=True)
        p = jnp.exp(s - m)
        denom = jnp.sum(p, axis=-1, keepdims=True)
        outs.append(jnp.dot(p.astype(BF16), vh, preferred_element_type=F32) / denom)
    o_ref[0] = jnp.concatenate(outs, axis=1).astype(o_ref.dtype)


def _memattn(h3, mem, w_kv, tm=512):
    B, L, _ = h3.shape
    M = mem.shape[1]
    tm = min(tm, L)
    return pl.pallas_call(
        _memattn_kernel,
        grid=(B, L // tm),
        in_specs=[
            pl.BlockSpec((1, tm, BRANCH_W), lambda b, l: (b, l, H_QM)),
            pl.BlockSpec((1, M, D_MODEL), lambda b, l: (b, 0, 0)),
            pl.BlockSpec((D_MODEL, 2 * BRANCH_W), lambda b, l: (0, 0)),
        ],
        out_specs=pl.BlockSpec((1, tm, BRANCH_W), lambda b, l: (b, l, 0)),
        out_shape=jax.ShapeDtypeStruct((B, L, BRANCH_W), BF16),
        scratch_shapes=[pltpu.VMEM((M, 2 * BRANCH_W), BF16)],
        compiler_params=_cparams(("parallel", "arbitrary")),
        name="memattn",
    )(h3, mem, w_kv)


HALO = 16


def _filt_kernel(w1_ref, b1_ref, w2_ref, b2_ref, w3_ref, b3_ref, fr_ref, fq_ref, dl_ref, k_ref, s_ref, *, seq, tr):
    i = pl.program_id(0)
    m = i * tr + lax.broadcasted_iota(jnp.int32, (tr, LANE), 0)
    lane = lax.broadcasted_iota(jnp.int32, (tr, LANE), 1)
    p = jnp.where(m < seq, m, 2 * seq - m).astype(F32)
    t = p / F32(seq - 1)
    w = F32(2.0 * math.pi) * p / F32(seq)
    bands = (POS_EMB_DIM - 1) // 2
    arg = fq_ref[...] * w
    arg = jnp.where(lane <= bands, arg + F32(0.5 * math.pi), -arg)
    z = jnp.where(lane == 0, t, jnp.where(lane <= 2 * bands, jnp.sin(arg), 0.0))
    fr = fr_ref[...]
    h = jnp.sin(fr[0:1] * (jnp.dot(z, w1_ref[...], precision=HI, preferred_element_type=F32) + b1_ref[...]))
    h = jnp.sin(fr[1:2] * (jnp.dot(h, w2_ref[...], precision=HI, preferred_element_type=F32) + b2_ref[...]))
    h = _dot3(h, w3_ref[...]) + b3_ref[...]
    mc = m[:, 0:1]
    tc = t[:, 0:1]
    decay = jnp.exp(-tc * dl_ref[...])
    oc = HYENA_ORDER * BRANCH_W

    @pl.when(i == 0)
    def _():
        s_ref[...] = jnp.zeros_like(s_ref)

    for o in range(HYENA_ORDER):
        fwd = h[:, o * BRANCH_W:(o + 1) * BRANCH_W]
        bwd = h[:, oc + o * BRANCH_W:oc + (o + 1) * BRANCH_W]
        k = jnp.where(mc < seq, fwd, bwd) * decay
        k = jnp.where(mc == seq, 0.0, k)
        k_ref[o] = k
        s_ref[o] += jnp.sum(jnp.abs(k), axis=0, keepdims=True)


def _filters(seq, w1, b1, w2, b2, w3, b3, freq, tr=256):
    n = 2 * seq
    tr = min(tr, n)
    bands = (POS_EMB_DIM - 1) // 2
    hid = LANE
    w1p = jnp.zeros((LANE, hid), F32).at[:POS_EMB_DIM, :FILTER_HIDDEN].set(w1.astype(F32))
    b1p = jnp.zeros((1, hid), F32).at[0, :FILTER_HIDDEN].set(b1.astype(F32))
    w2p = jnp.zeros((hid, hid), F32).at[:FILTER_HIDDEN, :FILTER_HIDDEN].set(w2.astype(F32))
    b2p = jnp.zeros((1, hid), F32).at[0, :FILTER_HIDDEN].set(b2.astype(F32))
    w3p = jnp.zeros((hid, w3.shape[1]), F32).at[:FILTER_HIDDEN].set(w3.astype(F32))
    b3p = b3.astype(F32)[None]
    frp = jnp.zeros((2, hid), F32).at[:, :FILTER_HIDDEN].set(freq.astype(F32))
    f = jnp.linspace(1e-4, bands - 1, bands, dtype=F32)
    fq = jnp.zeros((1, LANE), F32).at[0, 1:1 + bands].set(f).at[0, 1 + bands:1 + 2 * bands].set(f)
    deltas = jnp.abs(jnp.linspace(math.log(DECAY_TARGET) / FAST_DECAY_PCT,
                                  math.log(DECAY_TARGET) / SLOW_DECAY_PCT, BRANCH_W, dtype=F32))[None]
    full = lambda a: pl.BlockSpec(a.shape, lambda i: (0,) * a.ndim)
    args = (w1p, b1p, w2p, b2p, w3p, b3p, frp, fq, deltas)
    return pl.pallas_call(
        functools.partial(_filt_kernel, seq=seq, tr=tr),
        grid=(n // tr,),
        in_specs=[full(a) for a in args],
        out_specs=[pl.BlockSpec((HYENA_ORDER, tr, BRANCH_W), lambda i: (0, i, 0)),
                   pl.BlockSpec((HYENA_ORDER, 1, BRANCH_W), lambda i: (0, 0, 0))],
        out_shape=[jax.ShapeDtypeStruct((HYENA_ORDER, n, BRANCH_W), F32),
                   jax.ShapeDtypeStruct((HYENA_ORDER, 1, BRANCH_W), F32)],
        compiler_params=_cparams(("arbitrary",)),
        name="hyena_filter",
    )(*args)


def _dft_tables(seq):
    n1 = 2 * seq // FFT_N2
    n1h = seq // FFT_N2
    n = 2 * seq
    k1 = jnp.arange(n1, dtype=jnp.int32)
    th = (2.0 * math.pi / n1) * ((k1[:, None] * k1[None, :]) % n1).astype(F32)
    c1, s1 = jnp.cos(th), jnp.sin(th)
    ch, sh = c1[:, :n1h], s1[:, :n1h]
    f1 = jnp.concatenate([jnp.concatenate([ch, sh], 1), jnp.concatenate([-sh, ch], 1)], 0)
    f1_real = jnp.concatenate([c1, -s1], 0)
    cht, sht = ch.T / n, sh.T / n
    g1 = jnp.concatenate([jnp.concatenate([cht, -sht], 1), jnp.concatenate([sht, cht], 1)], 0)
    n2 = jnp.arange(FFT_N2, dtype=jnp.int32)
    kk = k1[:, None, None] + n1 * n2[None, :, None]
    ph = (2.0 * math.pi / n) * ((kk * n2[None, None, :]) % n).astype(F32)
    c2, s2 = jnp.cos(ph), jnp.sin(ph)
    m2 = jnp.concatenate([jnp.concatenate([c2, s2], 2), jnp.concatenate([-s2, c2], 2)], 1)
    return f1, f1_real, g1, m2, jnp.swapaxes(m2, 1, 2)


def _s1_kernel(z_ref, f_ref, o_ref):
    res = _dot3(f_ref[...], z_ref[0])
    half = res.shape[0] // 2
    o_ref[0, 0] = res[:half]
    o_ref[0, 1] = res[half:]


def _fft_s1(z, f, n2t=8):
    bx, lz, c = z.shape
    r = lz // FFT_N2
    n1 = f.shape[0] // 2
    w = n2t * c
    zz = z.reshape(bx, r, FFT_N2 * c)
    return pl.pallas_call(
        _s1_kernel,
        grid=(bx, FFT_N2 // n2t),
        in_specs=[pl.BlockSpec((1, r, w), lambda p, j: (p, 0, j)),
                  pl.BlockSpec(f.shape, lambda p, j: (0, 0))],
        out_specs=pl.BlockSpec((1, 2, n1, w), lambda p, j: (p, 0, 0, j)),
        out_shape=jax.ShapeDtypeStruct((bx, 2, n1, FFT_N2 * c), F32),
        compiler_params=_cparams(("parallel", "parallel")),
        name="fft_stage1",
    )(zz, f)


def _mid_fwd_kernel(a_ref, m_ref, o_ref):
    k1t = a_ref.shape[2]
    h = FFT_N2
    for j in range(k1t):
        a = jnp.concatenate([a_ref[0, 0, j], a_ref[0, 1, j]], axis=0)
        x = _dot3(m_ref[j], a)
        o_ref[0, 0, j] = x[:h]
        o_ref[0, 1, j] = x[h:]


def _fft_mid_fwd(a, m2, k1t=8):
    p, _, n1, wc = a.shape
    c = wc // FFT_N2
    k1t = min(k1t, n1)
    a5 = a.reshape(p, 2, n1, FFT_N2, c)
    blk = (1, 2, k1t, FFT_N2, c)
    return pl.pallas_call(
        _mid_fwd_kernel,
        grid=(n1 // k1t, p),
        in_specs=[pl.BlockSpec(blk, lambda k, q: (q, 0, k, 0, 0)),
                  pl.BlockSpec((k1t, 2 * FFT_N2, 2 * FFT_N2), lambda k, q: (k, 0, 0))],
        out_specs=pl.BlockSpec(blk, lambda k, q: (q, 0, k, 0, 0)),
        out_shape=jax.ShapeDtypeStruct(a5.shape, F32),
        compiler_params=_cparams(("parallel", "parallel")),
        name="fft_mid_filter",
    )(a5, m2)


ROW_PAD = 8


def _pack2(hi, lo):
    hb = lax.bitcast_convert_type(hi.astype(BF16).astype(F32), jnp.uint32)
    lb = lax.bitcast_convert_type(lo.astype(BF16).astype(F32), jnp.uint32)
    return hb | (lb >> 16)


def _unpack2(w):
    return (lax.bitcast_convert_type(w & jnp.uint32(0xFFFF0000), F32),
            lax.bitcast_convert_type(w << 16, F32))


def _rows(start, size, stride):
    return pl.ds(start, size) if stride is None else pl.ds(start, size, stride=stride)


def _ld(ref, start, size, stride=None):
    parts = [ref[l, _rows(start, size, stride), :] for l in range(ref.shape[0])]
    return parts[0] if len(parts) == 1 else jnp.concatenate(parts, axis=1)


def _st(ref, start, size, val, stride=None):
    for l in range(ref.shape[0]):
        ref[l, _rows(start, size, stride), :] = val[:, l * LANE:(l + 1) * LANE]


def _time_tile(ref, part, i, n_tiles, conv):
    h = FFT_N2
    r0 = pl.multiple_of(i * h, h)
    cur = ref[part, pl.ds(r0, h), :].astype(F32)
    if conv is None:
        return cur
    w_ref, b_ref = conv
    lo = pl.multiple_of(jnp.maximum(r0 - HALO, 0), HALO)
    hi = pl.multiple_of(jnp.minimum(r0 + h, n_tiles * h - HALO), HALO)
    prev = ref[part, pl.ds(lo, HALO), :].astype(F32)[HALO - 1:HALO]
    nxt = ref[part, pl.ds(hi, HALO), :].astype(F32)[0:1]
    prev = jnp.where(i > 0, prev, 0.0)
    nxt = jnp.where(i < n_tiles - 1, nxt, 0.0)
    row = lax.broadcasted_iota(jnp.int32, cur.shape, 0)
    up = jnp.where(row == 0, prev, pltpu.roll(cur, 1, 0))
    un = jnp.where(row == h - 1, nxt, pltpu.roll(cur, h - 1, 0))
    w = w_ref[...]
    return b_ref[...] + up * w[0:1] + cur * w[1:2] + un * w[2:3]


def _lc_kernel(v_ref, x_ref, f1_ref, g1_ref, m2_ref, m2t_ref, kf_ref, sc_ref, d_ref, cwv_ref, cbv_ref,
               cwx_ref, cbx_ref, o_ref, zst, a3, *, n1, n1h, k1t, nk, grp, conv_v, conv_x):
    k = pl.program_id(1)
    cb = v_ref.shape[2]
    zp = FFT_N2 + ROW_PAD
    ap = n1 + ROW_PAD
    h = FFT_N2
    vconv = (cwv_ref, cbv_ref) if conv_v else None
    xconv = (cwx_ref, cbx_ref) if conv_x else None

    @pl.when(k == 0)
    def _():
        def stage(i, c):
            w = _pack2(_time_tile(v_ref, 0, i, n1h, vconv), _time_tile(v_ref, 1, i, n1h, vconv))
            _st(zst, pl.multiple_of(i * zp, 8), h, w)
            return c
        lax.fori_loop(0, n1h, stage, 0)

        def level1(t, c):
            zs = []
            for u in range(grp):
                zr, zi = _unpack2(_ld(zst, t * grp + u, n1h, stride=zp))
                zs.append(jnp.concatenate([zr, zi], axis=0).astype(BF16))
            res = jnp.dot(f1_ref[...], jnp.concatenate(zs, axis=1), preferred_element_type=F32)
            for u in range(grp):
                r0 = pl.multiple_of((t * grp + u) * ap, 8)
                _st(a3, r0, n1, _pack2(res[:n1, u * cb:(u + 1) * cb], res[n1:, u * cb:(u + 1) * cb]))
            return c
        lax.fori_loop(0, h // grp, level1, 0, unroll=2)

    for jj in range(k1t):
        k1 = k * k1t + jj
        ar, ai = _unpack2(_ld(a3, k1, h, stride=ap))
        a = jnp.concatenate([ar, ai], axis=0).astype(BF16)
        x = jnp.dot(m2_ref[jj], a, preferred_element_type=F32)
        xr, xi = x[:h], x[h:]
        kr, ki = kf_ref[0, 0, jj], kf_ref[0, 1, jj]
        y = jnp.concatenate([xr * kr - xi * ki, xr * ki + xi * kr], axis=0).astype(BF16)
        b = jnp.dot(m2t_ref[jj], y, preferred_element_type=F32)
        _st(a3, k1, h, _pack2(b[:h], b[h:]), stride=ap)

    @pl.when(k == nk - 1)
    def _():
        def level3(t, c):
            bms = []
            for u in range(grp):
                r0 = pl.multiple_of((t * grp + u) * ap, 8)
                br, bi = _unpack2(_ld(a3, r0, n1))
                bms.append(jnp.concatenate([br, bi], axis=0).astype(BF16))
            y = jnp.dot(g1_ref[...], jnp.concatenate(bms, axis=1), preferred_element_type=F32)
            for u in range(grp):
                r0 = pl.multiple_of((t * grp + u) * ap, 8)
                _st(a3, r0, n1, lax.bitcast_convert_type(y[:, u * cb:(u + 1) * cb], jnp.uint32))
            return c
        lax.fori_loop(0, h // grp, level3, 0, unroll=2)

        def gate(i, c):
            r0 = pl.multiple_of(i * h, h)
            vs = _unpack2(_ld(zst, pl.multiple_of(i * zp, 8), h))
            for part in range(2):
                yv = lax.bitcast_convert_type(_ld(a3, part * n1h + i, h, stride=ap), F32)
                xx = _time_tile(x_ref, part, i, n1h, xconv)
                o_ref[part, pl.ds(r0, h), :] = (xx * (yv * sc_ref[...] + vs[part] * d_ref[...])).astype(o_ref.dtype)
            return c
        lax.fori_loop(0, n1h, gate, 0)


LC_VMEM_ROWS = 4096


def _longconv(v, v_seg, x, x_seg, conv_w, conv_b, kf, order, inv_norm, skip, tables, k1t=8):
    f1, g1, m2, m2t = tables
    bx, seq, _ = v.shape
    c = BRANCH_W
    cb = 2 * LANE if seq <= LC_VMEM_ROWS else LANE
    n1 = 2 * seq // FFT_N2
    n1h = seq // FFT_N2
    k1t = min(k1t, n1)
    nk = n1 // k1t
    ncb = c // cb
    grp = 4 if n1 >= FFT_N2 else 8
    hy0 = OFF_HY // cb

    def io(seg):
        col = 0 if seg is None else hy0 + seg * ncb
        return pl.BlockSpec((2, seq, cb), lambda g, k: (g // ncb, 0, col + g % ncb))

    def cvec(rows, seg):
        col = 0 if seg is None else seg * ncb
        return pl.BlockSpec((rows, cb), lambda g, k: (0, col + g % ncb))

    vec = pl.BlockSpec((1, cb), lambda g, k: (0, g % ncb))
    mat = pl.BlockSpec((k1t, 2 * FFT_N2, 2 * FFT_N2), lambda g, k: (k, 0, 0))
    return pl.pallas_call(
        functools.partial(_lc_kernel, n1=n1, n1h=n1h, k1t=k1t, nk=nk, grp=grp,
                          conv_v=v_seg is not None, conv_x=x_seg is not None),
        grid=((bx // 2) * ncb, nk),
        in_specs=[io(v_seg), io(x_seg),
                  pl.BlockSpec(f1.shape, lambda g, k: (0, 0)),
                  pl.BlockSpec(g1.shape, lambda g, k: (0, 0)),
                  mat, mat,
                  pl.BlockSpec((1, 2, k1t, FFT_N2, cb), lambda g, k: (order, 0, k, 0, g % ncb)),
                  vec, vec,
                  cvec(SHORT_CONV, v_seg), cvec(1, v_seg), cvec(SHORT_CONV, x_seg), cvec(1, x_seg)],
        out_specs=io(None),
        out_shape=jax.ShapeDtypeStruct((bx, seq, c), BF16),
        scratch_shapes=[pltpu.VMEM((cb // LANE, n1h * (FFT_N2 + ROW_PAD), LANE), jnp.uint32),
                        pltpu.VMEM((cb // LANE, FFT_N2 * (n1 + ROW_PAD), LANE), jnp.uint32)],
        compiler_params=_cparams(("arbitrary", "arbitrary")),
        name="hyena_longconv",
    )(v, x, f1, g1, m2, m2t, kf, inv_norm.astype(F32).reshape(1, c), skip.astype(F32).reshape(1, c),
      conv_w, conv_b, conv_w, conv_b)


def _hyena(h3, conv_w, conv_b, filt, skip):
    seq = h3.shape[1]
    f1, f1_real, g1, m2, m2t = _dft_tables(seq)
    taps, asum = _filters(seq, *filt)
    kf = _fft_mid_fwd(_fft_s1(taps, f1_real), m2)
    inv_norm = 1.0 / asum[:, 0]
    tables = (f1.astype(BF16), g1.astype(BF16), m2.astype(BF16), m2t.astype(BF16))
    z1 = _longconv(h3, 0, h3, 1, conv_w, conv_b, kf, 0, inv_norm[0], skip[0], tables)
    return _longconv(z1, None, h3, 2, conv_w, conv_b, kf, 1, inv_norm[1], skip[1], tables)


def _mix_kernel(x_ref, a_ref, hb_ref, c_ref, g0_ref, g1_ref, g2_ref, wb_ref, wo_ref,
                lig_ref, lib_ref, l1g_ref, l1b_ref, wrh_ref, wrl_ref, br_ref, x1_ref, x1b_ref, rt_ref, cnt_ref):
    acc = None
    for n, (br, gr) in enumerate(((a_ref, g0_ref), (hb_ref, g1_ref), (c_ref, g2_ref))):
        proj = jnp.dot(br[...], wb_ref[n], preferred_element_type=F32)
        term = jax.nn.sigmoid(gr[...].astype(F32)) * proj
        acc = term if acc is None else acc + term
    mix = jnp.dot(acc.astype(BF16), wo_ref[...], preferred_element_type=F32)
    xn = _ln(x_ref[...], lig_ref[...], lib_ref[...])
    x1 = _ln(DN_ALPHA * xn + mix, l1g_ref[...], l1b_ref[...])
    x1_ref[...] = x1
    xh = x1.astype(BF16)
    x1b_ref[...] = xh
    xl = (x1 - xh.astype(F32)).astype(BF16)
    nt = (((1,), (1,)), ((), ()))
    work = lax.dot_general(wrh_ref[...], xh, nt, preferred_element_type=F32)
    work = work + lax.dot_general(wrh_ref[...], xl, nt, preferred_element_type=F32)
    work = work + lax.dot_general(wrl_ref[...], xh, nt, preferred_element_type=F32)
    work = work + br_ref[...]

    tm = work.shape[1]
    erow = lax.broadcasted_iota(jnp.int32, work.shape, 0)
    ids, vals, hots = [], [], []
    for _ in range(TOP_K):
        mx = jnp.max(work, axis=0, keepdims=True)
        idx = jnp.min(jnp.where(work == mx, erow, N_EXPERTS), axis=0, keepdims=True)
        hot = erow == idx
        ids.append(idx)
        vals.append(mx)
        hots.append(hot)
        work = jnp.where(hot, F32(-3e38), work)
    exps = [jnp.exp(v - vals[0]) for v in vals]
    den = exps[0]
    for e in exps[1:]:
        den = den + e
    chosen = hots[0]
    for hot in hots[1:]:
        chosen = chosen | hot
    chosen = chosen.astype(F32)

    @pl.when(pl.program_id(0) == 0)
    def _():
        cnt_ref[...] = jnp.zeros_like(cnt_ref)

    r_i = lax.broadcasted_iota(jnp.int32, (tm, tm), 0)
    c_i = lax.broadcasted_iota(jnp.int32, (tm, tm), 1)
    earlier = (r_i < c_i).astype(BF16)
    before = jnp.dot(chosen.astype(BF16), earlier, preferred_element_type=F32)
    before = before + jnp.tile(cnt_ref[...], (1, tm // LANE))
    cnt_ref[...] += jnp.sum(chosen, axis=1, keepdims=True)
    rows = [i.astype(F32) for i in ids] + [e / den for e in exps]
    rows += [jnp.sum(jnp.where(hot, before, 0.0), axis=0, keepdims=True) for hot in hots]
    rows.append(jnp.zeros((rt_ref.shape[0] - len(rows), tm), F32))
    rt_ref[...] = jnp.concatenate(rows, axis=0)


ROUTE_ROWS = 16


def _mix(x2, a, hb, c, h2, wb, wo, lig, lib, l1g, l1b, wr, br, tm=512):
    T, D = x2.shape
    wrt = wr.T
    wrh = wrt.astype(BF16)
    wrl = (wrt - wrh.astype(F32)).astype(BF16)
    br = br.reshape(N_EXPERTS, 1)
    row = lambda wdt, col: pl.BlockSpec((tm, wdt), lambda i: (i, col))
    const = lambda arr: pl.BlockSpec(arr.shape, lambda i: (0,) * arr.ndim)
    return pl.pallas_call(
        _mix_kernel,
        grid=(T // tm,),
        in_specs=[row(D, 0), row(BRANCH_W, 0), row(BRANCH_W, 0), row(BRANCH_W, 0),
                  row(D, H_G), row(D, H_G + 1), row(D, H_G + 2),
                  const(wb), const(wo), const(lig), const(lib), const(l1g), const(l1b), const(wrh), const(wrl), const(br)],
        out_specs=[row(D, 0), row(D, 0), pl.BlockSpec((ROUTE_ROWS, tm), lambda i: (0, i)),
                   pl.BlockSpec((N_EXPERTS, LANE), lambda i: (0, 0))],
        out_shape=[jax.ShapeDtypeStruct((T, D), F32), jax.ShapeDtypeStruct((T, D), BF16),
                   jax.ShapeDtypeStruct((ROUTE_ROWS, T), F32), jax.ShapeDtypeStruct((N_EXPERTS, LANE), F32)],
        compiler_params=_cparams(("arbitrary",)),
        name="mix",
    )(x2, a, hb, c, h2, h2, h2, wb, wo, lig, lib, l1g, l1b, wrh, wrl, br)


def _ffn_kernel(ce_ref, nu_ref, x_ref, wgu_ref, bgu_ref, wd_ref, bd_ref, o_ref):
    del ce_ref
    used = pl.program_id(0) < nu_ref[0]

    @pl.when(used)
    def _():
        gu = jnp.dot(x_ref[...], wgu_ref[0], preferred_element_type=F32) + bgu_ref[0]
        f = gu.shape[1] // 2
        g = jnp.minimum(gu[:, :f], SWIGLU_LIMIT)
        u = jnp.clip(gu[:, f:], -SWIGLU_LIMIT, SWIGLU_LIMIT)
        hmid = (u + 1.0) * (g * jax.nn.sigmoid(g * SWIGLU_ALPHA))
        y = jnp.dot(hmid.astype(BF16), wd_ref[0].astype(BF16), preferred_element_type=F32) + bd_ref[0]
        o_ref[...] = y.astype(o_ref.dtype)

    @pl.when(jnp.logical_not(used))
    def _():
        o_ref[...] = jnp.zeros_like(o_ref)


def _ffn(chunk_e, n_used, x_slots, wgu, bgu, wd, bd):
    P, D = x_slots.shape
    F = wd.shape[1]
    n_chunks = P // MOE_TILE
    wspec = lambda a, b: pl.BlockSpec((1, a, b), lambda i, ce, nu: (ce[i], 0, 0))
    grid_spec = pltpu.PrefetchScalarGridSpec(
        num_scalar_prefetch=2,
        grid=(n_chunks,),
        in_specs=[pl.BlockSpec((MOE_TILE, D), lambda i, ce, nu: (i, 0)),
                  wspec(D, 2 * F), wspec(1, 2 * F), wspec(F, D), wspec(1, D)],
        out_specs=pl.BlockSpec((MOE_TILE, D), lambda i, ce, nu: (i, 0)),
    )
    return pl.pallas_call(
        _ffn_kernel,
        grid_spec=grid_spec,
        out_shape=jax.ShapeDtypeStruct((P, D), BF16),
        compiler_params=_cparams(("arbitrary",)),
        name="moe_ffn",
    )(chunk_e, n_used, x_slots, wgu, bgu, wd, bd)


DEINT_W = 256


def _deint_kernel(w_ref, s_ref, o_ref):
    w = w_ref[0].astype(BF16)
    f = w.shape[1] // 2
    half = DEINT_W // 2
    for blk in range(w.shape[1] // DEINT_W):
        r = jnp.dot(w[:, blk * DEINT_W:(blk + 1) * DEINT_W], s_ref[...], preferred_element_type=F32)
        o_ref[0, :, blk * half:(blk + 1) * half] = r[:, :half].astype(o_ref.dtype)
        o_ref[0, :, f + blk * half:f + (blk + 1) * half] = r[:, half:].astype(o_ref.dtype)


def _deinterleave(w, tr=1024):
    E, D, F2 = w.shape
    i = jnp.arange(DEINT_W, dtype=jnp.int32)
    src = jnp.where(i < DEINT_W // 2, 2 * i, 2 * (i - DEINT_W // 2) + 1)
    sel = (i[:, None] == src[None, :]).astype(BF16)
    return pl.pallas_call(
        _deint_kernel,
        grid=(E, D // tr),
        in_specs=[pl.BlockSpec((1, tr, F2), lambda e, r: (e, r, 0)),
                  pl.BlockSpec((DEINT_W, DEINT_W), lambda e, r: (0, 0))],
        out_specs=pl.BlockSpec((1, tr, F2), lambda e, r: (e, r, 0)),
        out_shape=jax.ShapeDtypeStruct((E, D, F2), BF16),
        compiler_params=_cparams(("parallel", "parallel")),
        name="deinterleave_gate_up",
    )(w, sel)


def _final_kernel(x1_ref, y_ref, gt_ref, g_ref, b_ref, o_ref):
    gt = gt_ref[...]
    moe = y_ref[0].astype(F32) * gt[:, 0:1]
    for k in range(1, TOP_K):
        moe = moe + y_ref[k].astype(F32) * gt[:, k:k + 1]
    o_ref[...] = _ln(DN_ALPHA * x1_ref[...] + moe, g_ref[...], b_ref[...])


def _final(x1, yg, gates, g, b, tm=512):
    T, D = x1.shape
    return pl.pallas_call(
        _final_kernel,
        grid=(T // tm,),
        in_specs=[pl.BlockSpec((tm, D), lambda i: (i, 0)),
                  pl.BlockSpec((TOP_K, tm, D), lambda i: (0, i, 0)),
                  pl.BlockSpec((tm, TOP_K), lambda i: (i, 0)),
                  pl.BlockSpec((1, D), lambda i: (0, 0)),
                  pl.BlockSpec((1, D), lambda i: (0, 0))],
        out_specs=pl.BlockSpec((tm, D), lambda i: (i, 0)),
        out_shape=jax.ShapeDtypeStruct((T, D), F32),
        compiler_params=_cparams(("parallel",)),
        name="final_ln",
    )(x1, yg, gates, g, b)


def _moe(x1, x1b, route, cnt, ffn_w, ln2_g, ln2_b):
    T, D = x1.shape
    A = T * TOP_K
    i32 = jnp.int32
    ids = route[:TOP_K].astype(i32)
    gates = route[TOP_K:2 * TOP_K]
    rank = route[2 * TOP_K:3 * TOP_K].astype(i32)
    counts = cnt[:, 0].astype(i32)
    pcounts = (counts + MOE_TILE - 1) // MOE_TILE * MOE_TILE
    pends = jnp.cumsum(pcounts)
    pstarts = pends - pcounts
    n_chunks = -(-A // MOE_TILE) + N_EXPERTS
    P = n_chunks * MOE_TILE
    e3 = jnp.arange(N_EXPERTS, dtype=i32)[:, None, None]
    pos = rank + jnp.sum(jnp.where(ids[None] == e3, pstarts[:, None, None], 0), axis=0)
    padc = jnp.concatenate([pcounts - counts, (P - pends[-1])[None]])
    pad_lo = jnp.concatenate([pstarts + counts, pends[-1:]])
    cum = jnp.cumsum(padc)
    j = jnp.arange(P - A, dtype=i32)
    run = jnp.sum((j[None, :] >= cum[:, None]).astype(i32), axis=0)
    hot = run[None, :] == jnp.arange(N_EXPERTS + 1, dtype=i32)[:, None]
    pad_key = j + jnp.sum(jnp.where(hot, (pad_lo - (cum - padc))[:, None], 0), axis=0)
    keys = jnp.concatenate([pos.reshape(A), pad_key])
    toks = jnp.concatenate([jnp.tile(jnp.arange(T, dtype=i32), TOP_K), jnp.zeros((P - A,), i32)])
    _, slot_tok = lax.sort((keys, toks), num_keys=1)
    chunk_lo = jnp.arange(n_chunks, dtype=i32) * MOE_TILE
    chunk_e = jnp.minimum(jnp.sum((chunk_lo[None, :] >= pends[:, None]).astype(i32), axis=0), N_EXPERTS - 1)
    x_slots = x1b[slot_tok]
    y_slots = _ffn(chunk_e, (pends[-1:] // MOE_TILE).astype(i32), x_slots, *ffn_w)
    yg = y_slots[pos.reshape(A)].reshape(TOP_K, T, D)
    return _final(x1, yg, gates.T, ln2_g, ln2_b)


def _mixer_half(x, mem, p):
    B, L, D = x.shape
    x2 = x.reshape(B * L, D)
    h2 = _inproj(x2, p["ln_in_g"], p["ln_in_b"], p["w_in"])
    h3 = h2.reshape(B, L, IN_W)
    a = _wattn(h3, p["sink"], p["bias"])
    hb = _hyena(h3, p["conv_w"], p["conv_b"], p["filt"], p["skip"])
    c = _memattn(h3, mem, p["w_mem_kv"])
    return _mix(x2, a.reshape(B * L, -1), hb.reshape(B * L, -1), c.reshape(B * L, -1), h2,
                p["w_branch"], p["w_out"], p["ln_in_g"], p["ln_in_b"], p["ln1_g"], p["ln1_b"],
                p["w_router"], p["b_router"])


def _trunk(x, mem, p):
    return _moe(*_mixer_half(x, mem, p), p["ffn"], p["ln2_g"], p["ln2_b"]).reshape(x.shape)


def _prep(ln_in_g, ln_in_b, rel_bias, w_in, attn_sink, conv_w, conv_b, filt_w1, filt_b1, filt_w2, filt_b2,
          filt_w3, filt_b3, filt_freq, hyena_skip, w_mem_kv, w_branch, w_out, ln1_g, ln1_b, w_router, b_router,
          w_gate_up, b_gate_up, w_down, b_down, ln2_g, ln2_b):
    w = w_in
    g_lo = Q_W + 2 * KV_W + 3 * BRANCH_W + BRANCH_W
    hy_lo = Q_W + 2 * KV_W
    assert HEAD_DIM ** -0.5 == 2.0 ** -3
    w_p = tuple(piece.astype(BF16) for piece in (
        w[:, g_lo:], w[:, :Q_W] * (HEAD_DIM ** -0.5), w[:, hy_lo + 3 * BRANCH_W:g_lo],
        w[:, hy_lo:hy_lo + 3 * BRANCH_W], w[:, Q_W:hy_lo]))
    row = lambda v: v.astype(F32)[None]
    wr = w_router.astype(F32)
    br = b_router.astype(F32)
    return dict(
        ln_in_g=row(ln_in_g), ln_in_b=row(ln_in_b), w_in=w_p,
        sink=attn_sink.astype(F32), bias=_bias_table(rel_bias),
        conv_w=conv_w.astype(F32), conv_b=row(conv_b),
        filt=(filt_w1, filt_b1, filt_w2, filt_b2, filt_w3, filt_b3, filt_freq), skip=hyena_skip,
        w_mem_kv=w_mem_kv.astype(BF16), w_branch=w_branch.astype(BF16), w_out=w_out.astype(BF16),
        ln1_g=row(ln1_g), ln1_b=row(ln1_b), w_router=wr, b_router=br,
        ffn=(_deinterleave(w_gate_up),
             jnp.concatenate([b_gate_up[:, 0::2], b_gate_up[:, 1::2]], axis=1)[:, None, :].astype(F32),
             w_down, b_down[:, None, :].astype(F32)),
        ln2_g=row(ln2_g), ln2_b=row(ln2_b),
    )


def kernel(x_prompt, x_sample, mem_prompt, mem_sample, ln_in_g, ln_in_b, rel_bias, w_in, attn_sink, conv_w, conv_b, filt_w1, filt_b1, filt_w2, filt_b2, filt_w3, filt_b3, filt_freq, hyena_skip, w_mem_kv, w_branch, w_out, ln1_g, ln1_b, w_router, b_router, w_gate_up, b_gate_up, w_down, b_down, ln2_g, ln2_b):
    p = _prep(ln_in_g, ln_in_b, rel_bias, w_in[0], attn_sink[0], conv_w[0], conv_b[0], filt_w1[0], filt_b1[0],
              filt_w2[0], filt_b2[0], filt_w3[0], filt_b3[0], filt_freq[0], hyena_skip[0], w_mem_kv[0],
              w_branch[0], w_out[0], ln1_g[0], ln1_b[0], w_router[0], b_router[0], w_gate_up[0], b_gate_up[0],
              w_down[0], b_down[0], ln2_g[0], ln2_b[0])
    halves = [_mixer_half(x, mem, p) for x, mem in ((x_prompt, mem_prompt), (x_sample, mem_sample))]
    outs = [_moe(*half, p["ffn"], p["ln2_g"], p["ln2_b"]) for half in halves]
    return (outs[0].reshape(x_prompt.shape), outs[1].reshape(x_sample.shape))
```

```python
import functools
import math

import jax
import jax.numpy as jnp
from jax import lax
from jax.experimental import pallas as pl
from jax.experimental.pallas import tpu as pltpu

F32 = jnp.float32
BF16 = jnp.bfloat16
HI = lax.Precision.HIGHEST

D_MODEL = 1024
BRANCH_W = 512
N_Q_HEADS = 8
N_KV_HEADS = 2
HEAD_DIM = 64
WINDOW = 128
BLOCK = 128
N_BUCKETS = 32
MAX_DISTANCE = 128
HYENA_ORDER = 2
SHORT_CONV = 3
POS_EMB_DIM = 33
FILTER_HIDDEN = 64
DECAY_TARGET = 1e-2
FAST_DECAY_PCT = 0.3
SLOW_DECAY_PCT = 1.5
N_MEM_HEADS = 4
MEM_HEAD_DIM = BRANCH_W // N_MEM_HEADS
N_BRANCHES = 3
N_EXPERTS = 32
TOP_K = 4
D_EXPERT = 1024
SWIGLU_LIMIT = 7.0
SWIGLU_ALPHA = 1.702
LN_EPS = 1e-5
DEPTH = 1
DN_ALPHA = (2 * DEPTH) ** 0.25

Q_W = N_Q_HEADS * HEAD_DIM
KV_W = N_KV_HEADS * HEAD_DIM
IN_W = Q_W + 2 * KV_W + 3 * BRANCH_W + BRANCH_W + N_BRANCHES * D_MODEL

LANE = 128
V7X_VMEM_BYTES = 64 * 1024 * 1024
VMEM_LIMIT = V7X_VMEM_BYTES - 12 * 1024 * 1024

OFF_G = 0
OFF_Q = OFF_G + N_BRANCHES * D_MODEL
OFF_QM = OFF_Q + Q_W
OFF_HY = OFF_QM + BRANCH_W
OFF_K = OFF_HY + 3 * BRANCH_W
OFF_V = OFF_K + KV_W
H_G = OFF_G // D_MODEL
H_Q = OFF_Q // Q_W
H_QM = OFF_QM // BRANCH_W
H_K = OFF_K // KV_W
H_V = OFF_V // KV_W

FFT_N2 = 128
MOE_TILE = 512


def _cparams(sem):
    return pltpu.CompilerParams(dimension_semantics=sem, vmem_limit_bytes=VMEM_LIMIT)


def _split(x):
    hi = x.astype(BF16)
    return hi, (x - hi.astype(F32)).astype(BF16)


def _dot3(a, b):
    ah, al = _split(a)
    bh, bl = _split(b)
    out = jnp.dot(ah, bh, preferred_element_type=F32)
    out = out + jnp.dot(al, bh, preferred_element_type=F32)
    return out + jnp.dot(ah, bl, preferred_element_type=F32)


def _ln(x, g, b):
    mu = jnp.mean(x, axis=-1, keepdims=True)
    xc = x - mu
    var = jnp.mean(xc * xc, axis=-1, keepdims=True)
    return xc * lax.rsqrt(var + LN_EPS) * g + b


def _inproj_kernel(x_ref, g_ref, b_ref, *refs):
    w_refs, o_ref = refs[:-1], refs[-1]
    xn = _ln(x_ref[...], g_ref[...], b_ref[...]).astype(BF16)
    col = 0
    for w_ref in w_refs:
        width = w_ref.shape[1]
        o_ref[:, col:col + width] = jnp.dot(xn, w_ref[...], preferred_element_type=F32).astype(o_ref.dtype)
        col += width


def _inproj(x2, ln_g, ln_b, w_parts, tm=512):
    T, D = x2.shape
    N = sum(w.shape[1] for w in w_parts)
    return pl.pallas_call(
        _inproj_kernel,
        grid=(T // tm,),
        in_specs=[pl.BlockSpec((tm, D), lambda i: (i, 0)),
                  pl.BlockSpec((1, D), lambda i: (0, 0)),
                  pl.BlockSpec((1, D), lambda i: (0, 0))]
                 + [pl.BlockSpec(w.shape, lambda i: (0, 0)) for w in w_parts],
        out_specs=pl.BlockSpec((tm, N), lambda i: (i, 0)),
        out_shape=jax.ShapeDtypeStruct((T, N), BF16),
        compiler_params=_cparams(("parallel",)),
        name="inproj",
    )(x2, ln_g, ln_b, *w_parts)


def _wattn_kernel(sink_ref, q_ref, kp_ref, kc_ref, kn_ref, vp_ref, vc_ref, vn_ref, bias_ref, o_ref, *, seq):
    n = pl.program_id(1)
    q = q_ref[0]
    kcat = jnp.concatenate([kp_ref[0], kc_ref[0], kn_ref[0]], axis=0)
    vcat = jnp.concatenate([vp_ref[0], vc_ref[0], vn_ref[0]], axis=0)
    qi = lax.broadcasted_iota(jnp.int32, (BLOCK, 3 * BLOCK), 0)
    kj = lax.broadcasted_iota(jnp.int32, (BLOCK, 3 * BLOCK), 1)
    rel = kj - BLOCK - qi
    kpos = (n - 1) * BLOCK + kj
    valid = (jnp.abs(rel) <= WINDOW) & (kpos >= 0) & (kpos < seq)
    group = N_Q_HEADS // N_KV_HEADS
    scores = []
    for h in range(N_Q_HEADS):
        g = h // group
        qh = q[:, h * HEAD_DIM:(h + 1) * HEAD_DIM]
        kh = kcat[:, g * HEAD_DIM:(g + 1) * HEAD_DIM]
        scores.append(lax.dot_general(qh, kh, (((1,), (1,)), ((), ())), preferred_element_type=F32))
    probs, denoms = [], []
    for h in range(N_Q_HEADS):
        s = jnp.where(valid, scores[h] + bias_ref[h], F32(-1e30))
        sk = sink_ref[h]
        m = jnp.maximum(jnp.max(s, axis=-1, keepdims=True), sk)
        p = jnp.exp(s - m)
        denoms.append(jnp.sum(p, axis=-1, keepdims=True) + jnp.exp(sk - m))
        probs.append(p.astype(BF16))
    outs = []
    for h in range(N_Q_HEADS):
        g = h // group
        vh = vcat[:, g * HEAD_DIM:(g + 1) * HEAD_DIM]
        outs.append(jnp.dot(probs[h], vh, preferred_element_type=F32) / denoms[h])
    o_ref[0] = jnp.concatenate(outs, axis=1).astype(o_ref.dtype)


def _wattn(h3, sink, bias):
    B, L, _ = h3.shape
    nb = L // BLOCK
    kv_spec = lambda col, shift: pl.BlockSpec(
        (1, BLOCK, KV_W), lambda b, n: (b, jnp.clip(n + shift, 0, nb - 1), col))
    return pl.pallas_call(
        functools.partial(_wattn_kernel, seq=L),
        grid=(B, nb),
        in_specs=[
            pl.BlockSpec(memory_space=pltpu.SMEM),
            pl.BlockSpec((1, BLOCK, Q_W), lambda b, n: (b, n, H_Q)),
            kv_spec(H_K, -1), kv_spec(H_K, 0), kv_spec(H_K, 1),
            kv_spec(H_V, -1), kv_spec(H_V, 0), kv_spec(H_V, 1),
            pl.BlockSpec((N_Q_HEADS, BLOCK, 3 * BLOCK), lambda b, n: (0, 0, 0)),
        ],
        out_specs=pl.BlockSpec((1, BLOCK, Q_W), lambda b, n: (b, n, 0)),
        out_shape=jax.ShapeDtypeStruct((B, L, Q_W), BF16),
        compiler_params=_cparams(("parallel", "parallel")),
        name="wattn",
    )(sink, h3, h3, h3, h3, h3, h3, h3, bias)


def _t5_bucket(rel):
    nb = N_BUCKETS // 2
    max_exact = nb // 2
    ret = jnp.where(rel > 0, nb, 0)
    n = jnp.abs(rel)
    nf = jnp.maximum(n, 1).astype(F32)
    large = max_exact + (jnp.log(nf / max_exact) / math.log(MAX_DISTANCE / max_exact)
                         * (nb - max_exact)).astype(jnp.int32)
    large = jnp.minimum(large, nb - 1)
    return ret + jnp.where(n < max_exact, n, large)


def _bias_table(rel_bias):
    qi = jnp.arange(BLOCK, dtype=jnp.int32)[:, None]
    kj = jnp.arange(3 * BLOCK, dtype=jnp.int32)[None, :]
    bias = rel_bias[_t5_bucket(kj - BLOCK - qi)].astype(F32)
    return jnp.transpose(bias, (2, 0, 1))


def _memattn_kernel(q_ref, mem_ref, w_ref, o_ref, kv_ref):
    @pl.when(pl.program_id(1) == 0)
    def _():
        kv_ref[...] = jnp.dot(mem_ref[0].astype(BF16), w_ref[...],
                              preferred_element_type=F32).astype(kv_ref.dtype)

    q = q_ref[0]
    outs = []
    for h in range(N_MEM_HEADS):
        lo = h * MEM_HEAD_DIM
        qh = q[:, lo:lo + MEM_HEAD_DIM]
        kh = kv_ref[:, lo:lo + MEM_HEAD_DIM]
        vh = kv_ref[:, BRANCH_W + lo:BRANCH_W + lo + MEM_HEAD_DIM]
        s = lax.dot_general(qh, kh, (((1,), (1,)), ((), ())), preferred_element_type=F32) * (MEM_HEAD_DIM ** -0.5)
        m = jnp.max(s, axis=-1, keepdims=True)
        p = jnp.exp(s - m)
        denom = jnp.sum(p, axis=-1, keepdims=True)
        outs.append(jnp.dot(p.astype(BF16), vh, preferred_element_type=F32) / denom)
    o_ref[0] = jnp.concatenate(outs, axis=1).astype(o_ref.dtype)


def _memattn(h3, mem, w_kv, tm=512):
    B, L, _ = h3.shape
    M = mem.shape[1]
    tm = min(tm, L)
    return pl.pallas_call(
        _memattn_kernel,
        grid=(B, L // tm),
        in_specs=[
            pl.BlockSpec((1, tm, BRANCH_W), lambda b, l: (b, l, H_QM)),
            pl.BlockSpec((1, M, D_MODEL), lambda b, l: (b, 0, 0)),
            pl.BlockSpec((D_MODEL, 2 * BRANCH_W), lambda b, l: (0, 0)),
        ],
        out_specs=pl.BlockSpec((1, tm, BRANCH_W), lambda b, l: (b, l, 0)),
        out_shape=jax.ShapeDtypeStruct((B, L, BRANCH_W), BF16),
        scratch_shapes=[pltpu.VMEM((M, 2 * BRANCH_W), BF16)],
        compiler_params=_cparams(("parallel", "arbitrary")),
        name="memattn",
    )(h3, mem, w_kv)


HALO = 16


def _filt_kernel(w1_ref, b1_ref, w2_ref, b2_ref, w3_ref, b3_ref, fr_ref, fq_ref, dl_ref, k_ref, s_ref, *, seq, tr):
    i = pl.program_id(0)
    m = i * tr + lax.broadcasted_iota(jnp.int32, (tr, LANE), 0)
    lane = lax.broadcasted_iota(jnp.int32, (tr, LANE), 1)
    p = jnp.where(m < seq, m, 2 * seq - m).astype(F32)
    t = p / F32(seq - 1)
    w = F32(2.0 * math.pi) * p / F32(seq)
    bands = (POS_EMB_DIM - 1) // 2
    arg = fq_ref[...] * w
    arg = jnp.where(lane <= bands, arg + F32(0.5 * math.pi), -arg)
    z = jnp.where(lane == 0, t, jnp.where(lane <= 2 * bands, jnp.sin(arg), 0.0))
    fr = fr_ref[...]
    h = jnp.sin(fr[0:1] * (jnp.dot(z, w1_ref[...], precision=HI, preferred_element_type=F32) + b1_ref[...]))
    h = jnp.sin(fr[1:2] * (jnp.dot(h, w2_ref[...], precision=HI, preferred_element_type=F32) + b2_ref[...]))
    h = _dot3(h, w3_ref[...]) + b3_ref[...]
    mc = m[:, 0:1]
    tc = t[:, 0:1]
    decay = jnp.exp(-tc * dl_ref[...])
    oc = HYENA_ORDER * BRANCH_W

    @pl.when(i == 0)
    def _():
        s_ref[...] = jnp.zeros_like(s_ref)

    for o in range(HYENA_ORDER):
        fwd = h[:, o * BRANCH_W:(o + 1) * BRANCH_W]
        bwd = h[:, oc + o * BRANCH_W:oc + (o + 1) * BRANCH_W]
        k = jnp.where(mc < seq, fwd, bwd) * decay
        k = jnp.where(mc == seq, 0.0, k)
        k_ref[o] = k
        s_ref[o] += jnp.sum(jnp.abs(k), axis=0, keepdims=True)


def _filters(seq, w1, b1, w2, b2, w3, b3, freq, tr=256):
    n = 2 * seq
    tr = min(tr, n)
    bands = (POS_EMB_DIM - 1) // 2
    hid = LANE
    w1p = jnp.zeros((LANE, hid), F32).at[:POS_EMB_DIM, :FILTER_HIDDEN].set(w1.astype(F32))
    b1p = jnp.zeros((1, hid), F32).at[0, :FILTER_HIDDEN].set(b1.astype(F32))
    w2p = jnp.zeros((hid, hid), F32).at[:FILTER_HIDDEN, :FILTER_HIDDEN].set(w2.astype(F32))
    b2p = jnp.zeros((1, hid), F32).at[0, :FILTER_HIDDEN].set(b2.astype(F32))
    w3p = jnp.zeros((hid, w3.shape[1]), F32).at[:FILTER_HIDDEN].set(w3.astype(F32))
    b3p = b3.astype(F32)[None]
    frp = jnp.zeros((2, hid), F32).at[:, :FILTER_HIDDEN].set(freq.astype(F32))
    f = jnp.linspace(1e-4, bands - 1, bands, dtype=F32)
    fq = jnp.zeros((1, LANE), F32).at[0, 1:1 + bands].set(f).at[0, 1 + bands:1 + 2 * bands].set(f)
    deltas = jnp.abs(jnp.linspace(math.log(DECAY_TARGET) / FAST_DECAY_PCT,
                                  math.log(DECAY_TARGET) / SLOW_DECAY_PCT, BRANCH_W, dtype=F32))[None]
    full = lambda a: pl.BlockSpec(a.shape, lambda i: (0,) * a.ndim)
    args = (w1p, b1p, w2p, b2p, w3p, b3p, frp, fq, deltas)
    return pl.pallas_call(
        functools.partial(_filt_kernel, seq=seq, tr=tr),
        grid=(n // tr,),
        in_specs=[full(a) for a in args],
        out_specs=[pl.BlockSpec((HYENA_ORDER, tr, BRANCH_W), lambda i: (0, i, 0)),
                   pl.BlockSpec((HYENA_ORDER, 1, BRANCH_W), lambda i: (0, 0, 0))],
        out_shape=[jax.ShapeDtypeStruct((HYENA_ORDER, n, BRANCH_W), F32),
                   jax.ShapeDtypeStruct((HYENA_ORDER, 1, BRANCH_W), F32)],
        compiler_params=_cparams(("arbitrary",)),
        name="hyena_filter",
    )(*args)


def _dft_tables(seq):
    n1 = 2 * seq // FFT_N2
    n1h = seq // FFT_N2
    n = 2 * seq
    k1 = jnp.arange(n1, dtype=jnp.int32)
    th = (2.0 * math.pi / n1) * ((k1[:, None] * k1[None, :]) % n1).astype(F32)
    c1, s1 = jnp.cos(th), jnp.sin(th)
    ch, sh = c1[:, :n1h], s1[:, :n1h]
    f1 = jnp.concatenate([jnp.concatenate([ch, sh], 1), jnp.concatenate([-sh, ch], 1)], 0)
    f1_real = jnp.concatenate([c1, -s1], 0)
    cht, sht = ch.T / n, sh.T / n
    g1 = jnp.concatenate([jnp.concatenate([cht, -sht], 1), jnp.concatenate([sht, cht], 1)], 0)
    n2 = jnp.arange(FFT_N2, dtype=jnp.int32)
    kk = k1[:, None, None] + n1 * n2[None, :, None]
    ph = (2.0 * math.pi / n) * ((kk * n2[None, None, :]) % n).astype(F32)
    c2, s2 = jnp.cos(ph), jnp.sin(ph)
    m2 = jnp.concatenate([jnp.concatenate([c2, s2], 2), jnp.concatenate([-s2, c2], 2)], 1)
    return f1, f1_real, g1, m2, jnp.swapaxes(m2, 1, 2)


def _s1_kernel(z_ref, f_ref, o_ref):
    res = _dot3(f_ref[...], z_ref[0])
    half = res.shape[0] // 2
    o_ref[0, 0] = res[:half]
    o_ref[0, 1] = res[half:]


def _fft_s1(z, f, n2t=8):
    bx, lz, c = z.shape
    r = lz // FFT_N2
    n1 = f.shape[0] // 2
    w = n2t * c
    zz = z.reshape(bx, r, FFT_N2 * c)
    return pl.pallas_call(
        _s1_kernel,
        grid=(bx, FFT_N2 // n2t),
        in_specs=[pl.BlockSpec((1, r, w), lambda p, j: (p, 0, j)),
                  pl.BlockSpec(f.shape, lambda p, j: (0, 0))],
        out_specs=pl.BlockSpec((1, 2, n1, w), lambda p, j: (p, 0, 0, j)),
        out_shape=jax.ShapeDtypeStruct((bx, 2, n1, FFT_N2 * c), F32),
        compiler_params=_cparams(("parallel", "parallel")),
        name="fft_stage1",
    )(zz, f)


def _mid_fwd_kernel(a_ref, m_ref, o_ref):
    k1t = a_ref.shape[2]
    h = FFT_N2
    for j in range(k1t):
        a = jnp.concatenate([a_ref[0, 0, j], a_ref[0, 1, j]], axis=0)
        x = _dot3(m_ref[j], a)
        o_ref[0, 0, j] = x[:h]
        o_ref[0, 1, j] = x[h:]


def _fft_mid_fwd(a, m2, k1t=8):
    p, _, n1, wc = a.shape
    c = wc // FFT_N2
    k1t = min(k1t, n1)
    a5 = a.reshape(p, 2, n1, FFT_N2, c)
    blk = (1, 2, k1t, FFT_N2, c)
    return pl.pallas_call(
        _mid_fwd_kernel,
        grid=(n1 // k1t, p),
        in_specs=[pl.BlockSpec(blk, lambda k, q: (q, 0, k, 0, 0)),
                  pl.BlockSpec((k1t, 2 * FFT_N2, 2 * FFT_N2), lambda k, q: (k, 0, 0))],
        out_specs=pl.BlockSpec(blk, lambda k, q: (q, 0, k, 0, 0)),
        out_shape=jax.ShapeDtypeStruct(a5.shape, F32),
        compiler_params=_cparams(("parallel", "parallel")),
        name="fft_mid_filter",
    )(a5, m2)


ROW_PAD = 8


def _pack2(hi, lo):
    hb = lax.bitcast_convert_type(hi.astype(BF16).astype(F32), jnp.uint32)
    lb = lax.bitcast_convert_type(lo.astype(BF16).astype(F32), jnp.uint32)
    return hb | (lb >> 16)


def _unpack2(w):
    return (lax.bitcast_convert_type(w & jnp.uint32(0xFFFF0000), F32),
            lax.bitcast_convert_type(w << 16, F32))


def _rows(start, size, stride):
    return pl.ds(start, size) if stride is None else pl.ds(start, size, stride=stride)


def _ld(ref, start, size, stride=None):
    parts = [ref[l, _rows(start, size, stride), :] for l in range(ref.shape[0])]
    return parts[0] if len(parts) == 1 else jnp.concatenate(parts, axis=1)


def _st(ref, start, size, val, stride=None):
    for l in range(ref.shape[0]):
        ref[l, _rows(start, size, stride), :] = val[:, l * LANE:(l + 1) * LANE]


def _time_tile(ref, part, i, n_tiles, conv):
    h = FFT_N2
    r0 = pl.multiple_of(i * h, h)
    cur = ref[part, pl.ds(r0, h), :].astype(F32)
    if conv is None:
        return cur
    w_ref, b_ref = conv
    lo = pl.multiple_of(jnp.maximum(r0 - HALO, 0), HALO)
    hi = pl.multiple_of(jnp.minimum(r0 + h, n_tiles * h - HALO), HALO)
    prev = ref[part, pl.ds(lo, HALO), :].astype(F32)[HALO - 1:HALO]
    nxt = ref[part, pl.ds(hi, HALO), :].astype(F32)[0:1]
    prev = jnp.where(i > 0, prev, 0.0)
    nxt = jnp.where(i < n_tiles - 1, nxt, 0.0)
    row = lax.broadcasted_iota(jnp.int32, cur.shape, 0)
    up = jnp.where(row == 0, prev, pltpu.roll(cur, 1, 0))
    un = jnp.where(row == h - 1, nxt, pltpu.roll(cur, h - 1, 0))
    w = w_ref[...]
    return b_ref[...] + up * w[0:1] + cur * w[1:2] + un * w[2:3]


def _lc_kernel(v_ref, x_ref, f1_ref, g1_ref, m2_ref, m2t_ref, kf_ref, sc_ref, d_ref, cwv_ref, cbv_ref,
               cwx_ref, cbx_ref, o_ref, zst, a3, *, n1, n1h, k1t, nk, grp, conv_v, conv_x):
    k = pl.program_id(1)
    cb = v_ref.shape[2]
    zp = FFT_N2 + ROW_PAD
    ap = n1 + ROW_PAD
    h = FFT_N2
    vconv = (cwv_ref, cbv_ref) if conv_v else None
    xconv = (cwx_ref, cbx_ref) if conv_x else None

    @pl.when(k == 0)
    def _():
        def stage(i, c):
            w = _pack2(_time_tile(v_ref, 0, i, n1h, vconv), _time_tile(v_ref, 1, i, n1h, vconv))
            _st(zst, pl.multiple_of(i * zp, 8), h, w)
            return c
        lax.fori_loop(0, n1h, stage, 0)

        def level1(t, c):
            zs = []
            for u in range(grp):
                zr, zi = _unpack2(_ld(zst, t * grp + u, n1h, stride=zp))
                zs.append(jnp.concatenate([zr, zi], axis=0).astype(BF16))
            res = jnp.dot(f1_ref[...], jnp.concatenate(zs, axis=1), preferred_element_type=F32)
            for u in range(grp):
                r0 = pl.multiple_of((t * grp + u) * ap, 8)
                _st(a3, r0, n1, _pack2(res[:n1, u * cb:(u + 1) * cb], res[n1:, u * cb:(u + 1) * cb]))
            return c
        lax.fori_loop(0, h // grp, level1, 0, unroll=2)

    for jj in range(k1t):
        k1 = k * k1t + jj
        ar, ai = _unpack2(_ld(a3, k1, h, stride=ap))
        a = jnp.concatenate([ar, ai], axis=0).astype(BF16)
        x = jnp.dot(m2_ref[jj], a, preferred_element_type=F32)
        xr, xi = x[:h], x[h:]
        kr, ki = kf_ref[0, 0, jj], kf_ref[0, 1, jj]
        y = jnp.concatenate([xr * kr - xi * ki, xr * ki + xi * kr], axis=0).astype(BF16)
        b = jnp.dot(m2t_ref[jj], y, preferred_element_type=F32)
        _st(a3, k1, h, _pack2(b[:h], b[h:]), stride=ap)

    @pl.when(k == nk - 1)
    def _():
        def level3(t, c):
            bms = []
            for u in range(grp):
                r0 = pl.multiple_of((t * grp + u) * ap, 8)
                br, bi = _unpack2(_ld(a3, r0, n1))
                bms.append(jnp.concatenate([br, bi], axis=0).astype(BF16))
            y = jnp.dot(g1_ref[...], jnp.concatenate(bms, axis=1), preferred_element_type=F32)
            for u in range(grp):
                r0 = pl.multiple_of((t * grp + u) * ap, 8)
                _st(a3, r0, n1, lax.bitcast_convert_type(y[:, u * cb:(u + 1) * cb], jnp.uint32))
            return c
        lax.fori_loop(0, h // grp, level3, 0, unroll=2)

        def gate(i, c):
            r0 = pl.multiple_of(i * h, h)
            vs = _unpack2(_ld(zst, pl.multiple_of(i * zp, 8), h))
            for part in range(2):
                yv = lax.bitcast_convert_type(_ld(a3, part * n1h + i, h, stride=ap), F32)
                xx = _time_tile(x_ref, part, i, n1h, xconv)
                o_ref[part, pl.ds(r0, h), :] = (xx * (yv * sc_ref[...] + vs[part] * d_ref[...])).astype(o_ref.dtype)
            return c
        lax.fori_loop(0, n1h, gate, 0)


LC_VMEM_ROWS = 4096


def _longconv(v, v_seg, x, x_seg, conv_w, conv_b, kf, order, inv_norm, skip, tables, k1t=16):
    f1, g1, m2, m2t = tables
    bx, seq, _ = v.shape
    c = BRANCH_W
    cb = 2 * LANE if seq <= LC_VMEM_ROWS else LANE
    n1 = 2 * seq // FFT_N2
    n1h = seq // FFT_N2
    k1t = min(k1t, n1)
    nk = n1 // k1t
    ncb = c // cb
    grp = 4 if n1 >= FFT_N2 else 8
    hy0 = OFF_HY // cb

    def io(seg):
        col = 0 if seg is None else hy0 + seg * ncb
        return pl.BlockSpec((2, seq, cb), lambda g, k: (g // ncb, 0, col + g % ncb))

    def cvec(rows, seg):
        col = 0 if seg is None else seg * ncb
        return pl.BlockSpec((rows, cb), lambda g, k: (0, col + g % ncb))

    vec = pl.BlockSpec((1, cb), lambda g, k: (0, g % ncb))
    mat = pl.BlockSpec((k1t, 2 * FFT_N2, 2 * FFT_N2), lambda g, k: (k, 0, 0))
    return pl.pallas_call(
        functools.partial(_lc_kernel, n1=n1, n1h=n1h, k1t=k1t, nk=nk, grp=grp,
                          conv_v=v_seg is not None, conv_x=x_seg is not None),
        grid=((bx // 2) * ncb, nk),
        in_specs=[io(v_seg), io(x_seg),
                  pl.BlockSpec(f1.shape, lambda g, k: (0, 0)),
                  pl.BlockSpec(g1.shape, lambda g, k: (0, 0)),
                  mat, mat,
                  pl.BlockSpec((1, 2, k1t, FFT_N2, cb), lambda g, k: (order, 0, k, 0, g % ncb)),
                  vec, vec,
                  cvec(SHORT_CONV, v_seg), cvec(1, v_seg), cvec(SHORT_CONV, x_seg), cvec(1, x_seg)],
        out_specs=io(None),
        out_shape=jax.ShapeDtypeStruct((bx, seq, c), BF16),
        scratch_shapes=[pltpu.VMEM((cb // LANE, n1h * (FFT_N2 + ROW_PAD), LANE), jnp.uint32),
                        pltpu.VMEM((cb // LANE, FFT_N2 * (n1 + ROW_PAD), LANE), jnp.uint32)],
        compiler_params=_cparams(("arbitrary", "arbitrary")),
        name="hyena_longconv",
    )(v, x, f1, g1, m2, m2t, kf, inv_norm.astype(F32).reshape(1, c), skip.astype(F32).reshape(1, c),
      conv_w, conv_b, conv_w, conv_b)


def _hyena(h3, conv_w, conv_b, filt, skip):
    seq = h3.shape[1]
    f1, f1_real, g1, m2, m2t = _dft_tables(seq)
    taps, asum = _filters(seq, *filt)
    kf = _fft_mid_fwd(_fft_s1(taps, f1_real), m2)
    inv_norm = 1.0 / asum[:, 0]
    tables = (f1.astype(BF16), g1.astype(BF16), m2.astype(BF16), m2t.astype(BF16))
    z1 = _longconv(h3, 0, h3, 1, conv_w, conv_b, kf, 0, inv_norm[0], skip[0], tables)
    return _longconv(z1, None, h3, 2, conv_w, conv_b, kf, 1, inv_norm[1], skip[1], tables)


def _mix_kernel(x_ref, a_ref, hb_ref, c_ref, g0_ref, g1_ref, g2_ref, wb_ref, wo_ref,
                lig_ref, lib_ref, l1g_ref, l1b_ref, wrh_ref, wrl_ref, br_ref, x1_ref, x1b_ref, rt_ref, cnt_ref):
    acc = None
    for n, (br, gr) in enumerate(((a_ref, g0_ref), (hb_ref, g1_ref), (c_ref, g2_ref))):
        proj = jnp.dot(br[...], wb_ref[n], preferred_element_type=F32)
        term = jax.nn.sigmoid(gr[...].astype(F32)) * proj
        acc = term if acc is None else acc + term
    mix = jnp.dot(acc.astype(BF16), wo_ref[...], preferred_element_type=F32)
    xn = _ln(x_ref[...], lig_ref[...], lib_ref[...])
    x1 = _ln(DN_ALPHA * xn + mix, l1g_ref[...], l1b_ref[...])
    x1_ref[...] = x1
    xh = x1.astype(BF16)
    x1b_ref[...] = xh
    xl = (x1 - xh.astype(F32)).astype(BF16)
    nt = (((1,), (1,)), ((), ()))
    work = lax.dot_general(wrh_ref[...], xh, nt, preferred_element_type=F32)
    work = work + lax.dot_general(wrh_ref[...], xl, nt, preferred_element_type=F32)
    work = work + lax.dot_general(wrl_ref[...], xh, nt, preferred_element_type=F32)
    work = work + br_ref[...]

    tm = work.shape[1]
    erow = lax.broadcasted_iota(jnp.int32, work.shape, 0)
    ids, vals, hots = [], [], []
    for _ in range(TOP_K):
        mx = jnp.max(work, axis=0, keepdims=True)
        idx = jnp.min(jnp.where(work == mx, erow, N_EXPERTS), axis=0, keepdims=True)
        hot = erow == idx
        ids.append(idx)
        vals.append(mx)
        hots.append(hot)
        work = jnp.where(hot, F32(-3e38), work)
    exps = [jnp.exp(v - vals[0]) for v in vals]
    den = exps[0]
    for e in exps[1:]:
        den = den + e
    chosen = hots[0]
    for hot in hots[1:]:
        chosen = chosen | hot
    chosen = chosen.astype(F32)

    @pl.when(pl.program_id(0) == 0)
    def _():
        cnt_ref[...] = jnp.zeros_like(cnt_ref)

    r_i = lax.broadcasted_iota(jnp.int32, (tm, tm), 0)
    c_i = lax.broadcasted_iota(jnp.int32, (tm, tm), 1)
    earlier = (r_i < c_i).astype(BF16)
    before = jnp.dot(chosen.astype(BF16), earlier, preferred_element_type=F32)
    before = before + jnp.tile(cnt_ref[...], (1, tm // LANE))
    cnt_ref[...] += jnp.sum(chosen, axis=1, keepdims=True)
    rows = [i.astype(F32) for i in ids] + [e / den for e in exps]
    rows += [jnp.sum(jnp.where(hot, before, 0.0), axis=0, keepdims=True) for hot in hots]
    rows.append(jnp.zeros((rt_ref.shape[0] - len(rows), tm), F32))
    rt_ref[...] = jnp.concatenate(rows, axis=0)


ROUTE_ROWS = 16


def _mix(x2, a, hb, c, h2, wb, wo, lig, lib, l1g, l1b, wr, br, tm=512):
    T, D = x2.shape
    wrt = wr.T
    wrh = wrt.astype(BF16)
    wrl = (wrt - wrh.astype(F32)).astype(BF16)
    br = br.reshape(N_EXPERTS, 1)
    row = lambda wdt, col: pl.BlockSpec((tm, wdt), lambda i: (i, col))
    const = lambda arr: pl.BlockSpec(arr.shape, lambda i: (0,) * arr.ndim)
    return pl.pallas_call(
        _mix_kernel,
        grid=(T // tm,),
        in_specs=[row(D, 0), row(BRANCH_W, 0), row(BRANCH_W, 0), row(BRANCH_W, 0),
                  row(D, H_G), row(D, H_G + 1), row(D, H_G + 2),
                  const(wb), const(wo), const(lig), const(lib), const(l1g), const(l1b), const(wrh), const(wrl), const(br)],
        out_specs=[row(D, 0), row(D, 0), pl.BlockSpec((ROUTE_ROWS, tm), lambda i: (0, i)),
                   pl.BlockSpec((N_EXPERTS, LANE), lambda i: (0, 0))],
        out_shape=[jax.ShapeDtypeStruct((T, D), F32), jax.ShapeDtypeStruct((T, D), BF16),
                   jax.ShapeDtypeStruct((ROUTE_ROWS, T), F32), jax.ShapeDtypeStruct((N_EXPERTS, LANE), F32)],
        compiler_params=_cparams(("arbitrary",)),
        name="mix",
    )(x2, a, hb, c, h2, h2, h2, wb, wo, lig, lib, l1g, l1b, wrh, wrl, br)


def _ffn_kernel(ce_ref, nu_ref, x_ref, wgu_ref, bgu_ref, wd_ref, bd_ref, o_ref):
    del ce_ref
    used = pl.program_id(0) < nu_ref[0]

    @pl.when(used)
    def _():
        gu = jnp.dot(x_ref[...], wgu_ref[0], preferred_element_type=F32) + bgu_ref[0]
        f = gu.shape[1] // 2
        g = jnp.minimum(gu[:, :f], SWIGLU_LIMIT)
        u = jnp.clip(gu[:, f:], -SWIGLU_LIMIT, SWIGLU_LIMIT)
        hmid = (u + 1.0) * (g * jax.nn.sigmoid(g * SWIGLU_ALPHA))
        y = jnp.dot(hmid.astype(BF16), wd_ref[0].astype(BF16), preferred_element_type=F32) + bd_ref[0]
        o_ref[...] = y.astype(o_ref.dtype)

    @pl.when(jnp.logical_not(used))
    def _():
        o_ref[...] = jnp.zeros_like(o_ref)


def _ffn(chunk_e, n_used, x_slots, wgu, bgu, wd, bd):
    P, D = x_slots.shape
    F = wd.shape[1]
    n_chunks = P // MOE_TILE
    wspec = lambda a, b: pl.BlockSpec((1, a, b), lambda i, ce, nu: (ce[i], 0, 0))
    grid_spec = pltpu.PrefetchScalarGridSpec(
        num_scalar_prefetch=2,
        grid=(n_chunks,),
        in_specs=[pl.BlockSpec((MOE_TILE, D), lambda i, ce, nu: (i, 0)),
                  wspec(D, 2 * F), wspec(1, 2 * F), wspec(F, D), wspec(1, D)],
        out_specs=pl.BlockSpec((MOE_TILE, D), lambda i, ce, nu: (i, 0)),
    )
    return pl.pallas_call(
        _ffn_kernel,
        grid_spec=grid_spec,
        out_shape=jax.ShapeDtypeStruct((P, D), BF16),
        compiler_params=_cparams(("arbitrary",)),
        name="moe_ffn",
    )(chunk_e, n_used, x_slots, wgu, bgu, wd, bd)


DEINT_W = 256


def _deint_kernel(w_ref, s_ref, o_ref):
    w = w_ref[0].astype(BF16)
    f = w.shape[1] // 2
    half = DEINT_W // 2
    for blk in range(w.shape[1] // DEINT_W):
        r = jnp.dot(w[:, blk * DEINT_W:(blk + 1) * DEINT_W], s_ref[...], preferred_element_type=F32)
        o_ref[0, :, blk * half:(blk + 1) * half] = r[:, :half].astype(o_ref.dtype)
        o_ref[0, :, f + blk * half:f + (blk + 1) * half] = r[:, half:].astype(o_ref.dtype)


def _deinterleave(w, tr=1024):
    E, D, F2 = w.shape
    i = jnp.arange(DEINT_W, dtype=jnp.int32)
    src = jnp.where(i < DEINT_W // 2, 2 * i, 2 * (i - DEINT_W // 2) + 1)
    sel = (i[:, None] == src[None, :]).astype(BF16)
    return pl.pallas_call(
        _deint_kernel,
        grid=(E, D // tr),
        in_specs=[pl.BlockSpec((1, tr, F2), lambda e, r: (e, r, 0)),
                  pl.BlockSpec((DEINT_W, DEINT_W), lambda e, r: (0, 0))],
        out_specs=pl.BlockSpec((1, tr, F2), lambda e, r: (e, r, 0)),
        out_shape=jax.ShapeDtypeStruct((E, D, F2), BF16),
        compiler_params=_cparams(("parallel", "parallel")),
        name="deinterleave_gate_up",
    )(w, sel)


def _final_kernel(x1_ref, y_ref, gt_ref, g_ref, b_ref, o_ref):
    gt = gt_ref[...]
    moe = y_ref[0].astype(F32) * gt[:, 0:1]
    for k in range(1, TOP_K):
        moe = moe + y_ref[k].astype(F32) * gt[:, k:k + 1]
    o_ref[...] = _ln(DN_ALPHA * x1_ref[...] + moe, g_ref[...], b_ref[...])


def _final(x1, yg, gates, g, b, tm=512):
    T, D = x1.shape
    return pl.pallas_call(
        _final_kernel,
        grid=(T // tm,),
        in_specs=[pl.BlockSpec((tm, D), lambda i: (i, 0)),
                  pl.BlockSpec((TOP_K, tm, D), lambda i: (0, i, 0)),
                  pl.BlockSpec((tm, TOP_K), lambda i: (i, 0)),
                  pl.BlockSpec((1, D), lambda i: (0, 0)),
                  pl.BlockSpec((1, D), lambda i: (0, 0))],
        out_specs=pl.BlockSpec((tm, D), lambda i: (i, 0)),
        out_shape=jax.ShapeDtypeStruct((T, D), F32),
        compiler_params=_cparams(("parallel",)),
        name="final_ln",
    )(x1, yg, gates, g, b)


def _moe(x1, x1b, route, cnt, ffn_w, ln2_g, ln2_b):
    T, D = x1.shape
    A = T * TOP_K
    i32 = jnp.int32
    ids = route[:TOP_K].astype(i32)
    gates = route[TOP_K:2 * TOP_K]
    rank = route[2 * TOP_K:3 * TOP_K].astype(i32)
    counts = cnt[:, 0].astype(i32)
    pcounts = (counts + MOE_TILE - 1) // MOE_TILE * MOE_TILE
    pends = jnp.cumsum(pcounts)
    pstarts = pends - pcounts
    n_chunks = -(-A // MOE_TILE) + N_EXPERTS
    P = n_chunks * MOE_TILE
    e3 = jnp.arange(N_EXPERTS, dtype=i32)[:, None, None]
    pos = rank + jnp.sum(jnp.where(ids[None] == e3, pstarts[:, None, None], 0), axis=0)
    padc = jnp.concatenate([pcounts - counts, (P - pends[-1])[None]])
    pad_lo = jnp.concatenate([pstarts + counts, pends[-1:]])
    cum = jnp.cumsum(padc)
    j = jnp.arange(P - A, dtype=i32)
    run = jnp.sum((j[None, :] >= cum[:, None]).astype(i32), axis=0)
    hot = run[None, :] == jnp.arange(N_EXPERTS + 1, dtype=i32)[:, None]
    pad_key = j + jnp.sum(jnp.where(hot, (pad_lo - (cum - padc))[:, None], 0), axis=0)
    keys = jnp.concatenate([pos.reshape(A), pad_key])
    toks = jnp.concatenate([jnp.tile(jnp.arange(T, dtype=i32), TOP_K), jnp.zeros((P - A,), i32)])
    _, slot_tok = lax.sort((keys, toks), num_keys=1)
    chunk_lo = jnp.arange(n_chunks, dtype=i32) * MOE_TILE
    chunk_e = jnp.minimum(jnp.sum((chunk_lo[None, :] >= pends[:, None]).astype(i32), axis=0), N_EXPERTS - 1)
    x_slots = x1b[slot_tok]
    y_slots = _ffn(chunk_e, (pends[-1:] // MOE_TILE).astype(i32), x_slots, *ffn_w)
    yg = y_slots[pos.reshape(A)].reshape(TOP_K, T, D)
    return _final(x1, yg, gates.T, ln2_g, ln2_b)


def _mixer_half(x, mem, p):
    B, L, D = x.shape
    x2 = x.reshape(B * L, D)
    h2 = _inproj(x2, p["ln_in_g"], p["ln_in_b"], p["w_in"])
    h3 = h2.reshape(B, L, IN_W)
    a = _wattn(h3, p["sink"], p["bias"])
    hb = _hyena(h3, p["conv_w"], p["conv_b"], p["filt"], p["skip"])
    c = _memattn(h3, mem, p["w_mem_kv"])
    return _mix(x2, a.reshape(B * L, -1), hb.reshape(B * L, -1), c.reshape(B * L, -1), h2,
                p["w_branch"], p["w_out"], p["ln_in_g"], p["ln_in_b"], p["ln1_g"], p["ln1_b"],
                p["w_router"], p["b_router"])


def _trunk(x, mem, p):
    return _moe(*_mixer_half(x, mem, p), p["ffn"], p["ln2_g"], p["ln2_b"]).reshape(x.shape)


def _prep(ln_in_g, ln_in_b, rel_bias, w_in, attn_sink, conv_w, conv_b, filt_w1, filt_b1, filt_w2, filt_b2,
          filt_w3, filt_b3, filt_freq, hyena_skip, w_mem_kv, w_branch, w_out, ln1_g, ln1_b, w_router, b_router,
          w_gate_up, b_gate_up, w_down, b_down, ln2_g, ln2_b):
    w = w_in
    g_lo = Q_W + 2 * KV_W + 3 * BRANCH_W + BRANCH_W
    hy_lo = Q_W + 2 * KV_W
    assert HEAD_DIM ** -0.5 == 2.0 ** -3
    w_p = tuple(piece.astype(BF16) for piece in (
        w[:, g_lo:], w[:, :Q_W] * (HEAD_DIM ** -0.5), w[:, hy_lo + 3 * BRANCH_W:g_lo],
        w[:, hy_lo:hy_lo + 3 * BRANCH_W], w[:, Q_W:hy_lo]))
    row = lambda v: v.astype(F32)[None]
    wr = w_router.astype(F32)
    br = b_router.astype(F32)
    return dict(
        ln_in_g=row(ln_in_g), ln_in_b=row(ln_in_b), w_in=w_p,
        sink=attn_sink.astype(F32), bias=_bias_table(rel_bias),
        conv_w=conv_w.astype(F32), conv_b=row(conv_b),
        filt=(filt_w1, filt_b1, filt_w2, filt_b2, filt_w3, filt_b3, filt_freq), skip=hyena_skip,
        w_mem_kv=w_mem_kv.astype(BF16), w_branch=w_branch.astype(BF16), w_out=w_out.astype(BF16),
        ln1_g=row(ln1_g), ln1_b=row(ln1_b), w_router=wr, b_router=br,
        ffn=(_deinterleave(w_gate_up),
             jnp.concatenate([b_gate_up[:, 0::2], b_gate_up[:, 1::2]], axis=1)[:, None, :].astype(F32),
             w_down, b_down[:, None, :].astype(F32)),
        ln2_g=row(ln2_g), ln2_b=row(ln2_b),
    )


def kernel(x_prompt, x_sample, mem_prompt, mem_sample, ln_in_g, ln_in_b, rel_bias, w_in, attn_sink, conv_w, conv_b, filt_w1, filt_b1, filt_w2, filt_b2, filt_w3, filt_b3, filt_freq, hyena_skip, w_mem_kv, w_branch, w_out, ln1_g, ln1_b, w_router, b_router, w_gate_up, b_gate_up, w_down, b_down, ln2_g, ln2_b):
    p = _prep(ln_in_g, ln_in_b, rel_bias, w_in[0], attn_sink[0], conv_w[0], conv_b[0], filt_w1[0], filt_b1[0],
              filt_w2[0], filt_b2[0], filt_w3[0], filt_b3[0], filt_freq[0], hyena_skip[0], w_mem_kv[0],
              w_branch[0], w_out[0], ln1_g[0], ln1_b[0], w_router[0], b_router[0], w_gate_up[0], b_gate_up[0],
              w_down[0], b_down[0], ln2_g[0], ln2_b[0])
    halves = [_mixer_half(x, mem, p) for x, mem in ((x_prompt, mem_prompt), (x_sample, mem_sample))]
    outs = [_moe(*half, p["ffn"], p["ln2_g"], p["ln2_b"]) for half in halves]
    return (outs[0].reshape(x_prompt.shape), outs[1].reshape(x_sample.shape))
```
